```python
import math
import jax
import jax.numpy as jnp
from jax import lax
import numpy as np

D_MODEL = 2048
BATCH = 1
SEQ = 8192
DEPTH = 2
DEC_BATCH = 32
DEC_SEQ = 1
PAST_LEN = 8192
PAGE_SIZE = 128

N_META = 16
N_EVEN = (DEPTH + 1) // 2
N_ODD = DEPTH // 2
NORM_EPS = 1e-6
SUBLN_EPS = 1e-5
ROPE_THETA = 10000.0
NEG_INF = -1e30
Q_BLOCK = 128

A_HEADS = 8
A_DH = 64
A_VD = 2 * A_DH
A_WIDTH = A_HEADS * A_VD

B_HEADS = 16
B_DH = 64
B_WIDTH = B_HEADS * B_DH
B_W_RANK = 64
B_A_RANK = 64
B_G_RANK = 128
B_PROJ = 3 * B_WIDTH + B_W_RANK + B_A_RANK + B_G_RANK
B_GN_EPS = 64e-5

EVEN_PROJ = 3 * A_WIDTH + B_PROJ
EVEN_OUT = A_WIDTH + B_WIDTH

C_K_HEADS = 16
C_V_HEADS = 32
C_DK = 128
C_DV = 128
C_KEY = C_K_HEADS * C_DK
C_VAL = C_V_HEADS * C_DV
C_CONV = 4
C_CONV_CH = 2 * C_KEY + C_VAL
C_CHUNK = 64
ODD_PROJ = C_CONV_CH + C_VAL + 2 * C_V_HEADS

P_HEADS = 8
P_NKEYS = 128
P_EXPERTS = P_NKEYS * P_NKEYS
P_TOPK = 16
P_DQ = 256
P_BLOCK = 64

EVEN_KEYS = ('w_in_even', 'w_out_even', 'lam_q1', 'lam_k1', 'lam_q2', 'lam_k2', 'subln',
             'rw_mu', 'rw_w0', 'rw_w_up', 'rw_a0', 'rw_a_up', 'rw_g_up', 'rw_k_k', 'rw_k_a',
             'rw_r_k', 'rw_ln_g', 'rw_ln_b')
ODD_KEYS = ('w_in_odd', 'conv_w', 'a_log', 'dt_bias', 'gdn_norm', 'w_out_odd')

kernel_name = 'hybrid_diffattn_rwkv7_gdn_peer_step'


def rms_norm(x, g, eps=NORM_EPS):
    xf = x.astype(jnp.float32)
    y = xf * lax.rsqrt(jnp.mean(xf * xf, axis=-1, keepdims=True) + eps)
    return (y * g.astype(jnp.float32)).astype(x.dtype)


def l2norm(x, eps=1e-6):
    return x * lax.rsqrt(jnp.sum(x * x, axis=-1, keepdims=True) + eps)


def rope(x, pos):
    half = x.shape[-1] // 2
    inv_freq = ROPE_THETA ** (-jnp.arange(half, dtype=jnp.float32) / half)
    ang = pos.astype(jnp.float32)[:, None] * inv_freq[None, :]
    shape = (1, pos.shape[0]) + (1,) * (x.ndim - 3) + (half,)
    cos = jnp.cos(ang).reshape(shape)
    sin = jnp.sin(ang).reshape(shape)
    xf = x.astype(jnp.float32)
    x1, x2 = xf[..., :half], xf[..., half:]
    return jnp.concatenate([x1 * cos - x2 * sin, x2 * cos + x1 * sin], axis=-1).astype(x.dtype)


def lambda_init(layer):
    return 0.8 - 0.6 * math.exp(-0.3 * layer)


def diff_attn_prompt(q, k, v, lam):
    B, T = q.shape[:2]
    pad = (-T) % Q_BLOCK
    padf = lambda a: jnp.pad(a.astype(jnp.float32), [(0, 0), (pad, 0)] + [(0, 0)] * (a.ndim - 2))
    qp, kp, vp = padf(q) * (A_DH ** -0.5), padf(k), padf(v)
    tp = T + pad
    kidx = jnp.arange(tp)

    def block(bi):
        start = bi * Q_BLOCK
        qb = lax.dynamic_slice_in_dim(qp, start, Q_BLOCK, axis=1)
        s = jnp.einsum('bqhcd,bkhcd->bhcqk', qb, kp)
        qidx = start + jnp.arange(Q_BLOCK)
        mask = (kidx[None, :] <= qidx[:, None]) & (kidx[None, :] >= pad)
        p = jax.nn.softmax(jnp.where(mask, s, NEG_INF), axis=-1)
        w = p[:, :, 0] - lam * p[:, :, 1]
        return jnp.einsum('bhqk,bkhe->bqhe', w, vp)

    out = lax.map(block, jnp.arange(tp // Q_BLOCK))
    return jnp.moveaxis(out, 0, 1).reshape(B, tp, A_HEADS, A_VD)[:, pad:]


def diff_attn_decode(q, k, v, lam, cache_k, cache_v, page_table, layer):
    bd, s_len = q.shape[:2]
    qf = q.astype(jnp.float32) * (A_DH ** -0.5)

    def update(carry, s, vals):
        m, l, acc = carry
        m_new = jnp.maximum(m, s.max(-1))
        corr = jnp.exp(m - m_new)
        p = jnp.exp(s - m_new[..., None])
        return (m_new, l * corr + p.sum(-1),
                acc * corr[..., None] + jnp.einsum('bhcqk,bkhe->bhcqe', p, vals))

    def page_step(carry, phys):
        kp = cache_k[layer, phys].astype(jnp.float32).reshape(bd, PAGE_SIZE, A_HEADS, 2, A_DH)
        vp = cache_v[layer, phys].astype(jnp.float32)
        s = jnp.einsum('bqhcd,bkhcd->bhcqk', qf, kp)
        return update(carry, s, vp), None

    init = (jnp.full((bd, A_HEADS, 2, s_len), NEG_INF, jnp.float32),
            jnp.zeros((bd, A_HEADS, 2, s_len), jnp.float32),
            jnp.zeros((bd, A_HEADS, 2, s_len, A_VD), jnp.float32))
    carry, _ = lax.scan(page_step, init, page_table.T)
    s = jnp.einsum('bqhcd,bkhcd->bhcqk', qf, k.astype(jnp.float32))
    causal = jnp.arange(s_len)[:, None] >= jnp.arange(s_len)[None, :]
    m, l, acc = update(carry, jnp.where(causal, s, NEG_INF), v.astype(jnp.float32))
    o = acc / l[..., None]
    return jnp.transpose(o[:, :, 0] - lam * o[:, :, 1], (0, 2, 1, 3))


def rwkv7_mix(zb, p, prev, s0):
    B, T, _ = zb.shape
    zf = zb.astype(jnp.float32)
    zprev = jnp.concatenate([prev.astype(jnp.float32)[:, None], zf[:, :-1]], axis=1)
    zs = zf + p['rw_mu'] * (zprev - zf)
    o1, o2, o3 = B_WIDTH, 2 * B_WIDTH, 3 * B_WIDTH
    o4 = o3 + B_W_RANK
    o5 = o4 + B_A_RANK
    r, k, v = zs[..., :o1], zs[..., o1:o2], zs[..., o2:o3]
    xw, xa, xg = zs[..., o3:o4], zs[..., o4:o5], zs[..., o5:]
    w = -jax.nn.softplus(-(p['rw_w0'] + jnp.tanh(xw) @ p['rw_w_up'])) - 0.5
    decay = jnp.exp(-jnp.exp(w))
    a = jax.nn.sigmoid(p['rw_a0'] + xa @ p['rw_a_up'])
    g = jax.nn.sigmoid(xg) @ p['rw_g_up']
    heads = lambda t: t.reshape(B, T, B_HEADS, B_DH)
    kk = heads(k * p['rw_k_k'])
    kk = kk / jnp.maximum(jnp.sqrt(jnp.sum(kk * kk, axis=-1, keepdims=True)), 1e-12)
    k = k * (1.0 + (a - 1.0) * p['rw_k_a'])
    r, k, v, decay, a = heads(r), heads(k), heads(v), heads(decay), heads(a)

    def step(S, inp):
        r_t, w_t, k_t, v_t, kk_t, a_t = inp
        sa = jnp.einsum('bhij,bhj->bhi', S, -kk_t)
        S = (S * w_t[:, :, None, :] + sa[..., None] * (kk_t * a_t)[:, :, None, :]
             + v_t[..., None] * k_t[:, :, None, :])
        return S, jnp.einsum('bhij,bhj->bhi', S, r_t)

    seq = tuple(jnp.moveaxis(t, 1, 0) for t in (r, decay, k, v, kk, a))
    s_new, y = lax.scan(step, s0.astype(jnp.float32), seq)
    y = jnp.moveaxis(y, 0, 1)
    mu = jnp.mean(y, axis=-1, keepdims=True)
    var = jnp.mean(jnp.square(y - mu), axis=-1, keepdims=True)
    y = ((y - mu) * lax.rsqrt(var + B_GN_EPS)).reshape(B, T, B_WIDTH) * p['rw_ln_g'] + p['rw_ln_b']
    bonus = jnp.sum(r * k * p['rw_r_k'], axis=-1, keepdims=True) * v
    y = (y + bonus.reshape(B, T, B_WIDTH)) * g
    return y, zb[:, -1], s_new


def even_mix(h, p, lam_init, pos, past, i):
    B, T, _ = h.shape
    z = h @ p['w_in_even']
    q = rope(z[..., :A_WIDTH].reshape(B, T, A_HEADS, 2, A_DH), pos)
    k = rope(z[..., A_WIDTH:2 * A_WIDTH].reshape(B, T, A_HEADS, 2, A_DH), pos)
    v = z[..., 2 * A_WIDTH:3 * A_WIDTH].reshape(B, T, A_HEADS, A_VD)
    zb = z[..., 3 * A_WIDTH:]
    lam = (jnp.exp(jnp.sum(p['lam_q1'] * p['lam_k1'])) - jnp.exp(jnp.sum(p['lam_q2'] * p['lam_k2']))).astype(jnp.float32) + lam_init
    if past is None:
        att = diff_attn_prompt(q, k, v, lam)
        prev = jnp.zeros((B, B_PROJ), h.dtype)
        s0 = jnp.zeros((B, B_HEADS, B_DH, B_DH), jnp.float32)
    else:
        att = diff_attn_decode(q, k, v, lam, past['cache_k'], past['cache_v'], past['page_table'], i)
        prev = past['state_shift'][i]
        s0 = past['state_wkv'][i]
    att = rms_norm(att, p['subln'], SUBLN_EPS) * (1.0 - lam_init)
    rw, shift_new, s_new = rwkv7_mix(zb, p, prev, s0)
    y = jnp.concatenate([att.reshape(B, T, A_WIDTH), rw], axis=-1).astype(h.dtype) @ p['w_out_even']
    new = (k.reshape(B, T, A_HEADS, 2 * A_DH), v, s_new.astype(h.dtype), shift_new.astype(h.dtype))
    return y, new


def gdn_inputs(z, p, buf):
    B, T, _ = z.shape
    mixed = z[..., :C_CONV_CH]
    gate = z[..., C_CONV_CH:C_CONV_CH + C_VAL]
    b = z[..., C_CONV_CH + C_VAL:C_CONV_CH + C_VAL + C_V_HEADS]
    a = z[..., C_CONV_CH + C_VAL + C_V_HEADS:]
    xp = jnp.concatenate([buf.astype(mixed.dtype), mixed], axis=1)
    conv = lax.conv_general_dilated(xp, p['conv_w'].astype(xp.dtype)[:, None, :], window_strides=(1,),
                                    padding='VALID', dimension_numbers=('NWC', 'WIO', 'NWC'),
                                    feature_group_count=C_CONV_CH)
    act = jax.nn.silu(conv.astype(jnp.float32))
    rep = C_V_HEADS // C_K_HEADS
    q = jnp.repeat(l2norm(act[..., :C_KEY].reshape(B, T, C_K_HEADS, C_DK)) * (C_DK ** -0.5), rep, axis=2)
    k = jnp.repeat(l2norm(act[..., C_KEY:2 * C_KEY].reshape(B, T, C_K_HEADS, C_DK)), rep, axis=2)
    v = act[..., 2 * C_KEY:].reshape(B, T, C_V_HEADS, C_DV)
    beta = jax.nn.sigmoid(b.astype(jnp.float32))
    g = -jnp.exp(p['a_log'].astype(jnp.float32)) * jax.nn.softplus(a.astype(jnp.float32) + p['dt_bias'].astype(jnp.float32))
    return q, k, v, beta, g, gate, xp[:, -(C_CONV - 1):]


def gdn_chunked(q, k, v, beta, g, s0):
    B, T, H, _ = q.shape
    pad = (-T) % C_CHUNK
    padf = lambda a: jnp.pad(a, [(0, 0), (pad, 0)] + [(0, 0)] * (a.ndim - 2))
    n = (T + pad) // C_CHUNK

    def to_chunks(a):
        return jnp.moveaxis(padf(a).reshape((B, n, C_CHUNK) + a.shape[2:]), 3, 2)

    q, k, v, beta, g = (to_chunks(t) for t in (q, k, v, beta, g))
    gc = jnp.cumsum(g, axis=-1)
    idx = jnp.arange(C_CHUNK)
    causal = idx[:, None] >= idx[None, :]
    strict = idx[:, None] > idx[None, :]
    decay = jnp.where(causal, jnp.exp(jnp.where(causal, gc[..., :, None] - gc[..., None, :], 0.0)), 0.0)
    kb = k * beta[..., None]
    low = jnp.where(strict, jnp.einsum('bnhid,bnhjd->bnhij', kb, k) * decay, 0.0)
    eye = jnp.eye(C_CHUNK, dtype=jnp.float32)
    tinv = lax.linalg.triangular_solve(eye + low, jnp.broadcast_to(eye, low.shape),
                                       left_side=True, lower=True, unit_diagonal=True)
    u = tinv @ (v * beta[..., None])
    wk = tinv @ (kb * jnp.exp(gc)[..., None])
    attn = jnp.where(causal, jnp.einsum('bnhid,bnhjd->bnhij', q, k) * decay, 0.0)
    qg = q * jnp.exp(gc)[..., None]
    kdec = k * jnp.exp(gc[..., -1:] - gc)[..., None]
    glast = jnp.exp(gc[..., -1])

    def step(S, inp):
        u_c, w_c, a_c, qg_c, kd_c, gl_c = inp
        v_new = u_c - jnp.einsum('bhck,bhkv->bhcv', w_c, S)
        o = jnp.einsum('bhck,bhkv->bhcv', qg_c, S) + jnp.einsum('bhij,bhjv->bhiv', a_c, v_new)
        S = S * gl_c[..., None, None] + jnp.einsum('bhck,bhcv->bhkv', kd_c, v_new)
        return S, o

    seq = tuple(jnp.moveaxis(t, 1, 0) for t in (u, wk, attn, qg, kdec, glast))
    s_new, o = lax.scan(step, s0, seq)
    o = jnp.moveaxis(jnp.moveaxis(o, 0, 1), 2, 3).reshape(B, n * C_CHUNK, H, C_DV)[:, pad:]
    return o, s_new


def gdn_recurrent(q, k, v, beta, g, s0):
    def step(S, inp):
        q_t, k_t, v_t, b_t, g_t = inp
        S = S * jnp.exp(g_t)[..., None, None]
        kv = jnp.einsum('bhkv,bhk->bhv', S, k_t)
        S = S + jnp.einsum('bhk,bhv->bhkv', k_t, (v_t - kv) * b_t[..., None])
        return S, jnp.einsum('bhkv,bhk->bhv', S, q_t)

    seq = tuple(jnp.moveaxis(t, 1, 0) for t in (q, k, v, beta, g))
    s_new, o = lax.scan(step, s0, seq)
    return jnp.moveaxis(o, 0, 1), s_new


def odd_mix(h, p, past, i):
    B, T, _ = h.shape
    z = h @ p['w_in_odd']
    if past is None:
        buf = jnp.zeros((B, C_CONV - 1, C_CONV_CH), h.dtype)
        s0 = jnp.zeros((B, C_V_HEADS, C_DK, C_DV), jnp.float32)
    else:
        buf = past['state_conv'][i]
        s0 = past['state_gdn'][i].astype(jnp.float32)
    q, k, v, beta, g, gate, buf_new = gdn_inputs(z, p, buf)
    if past is None:
        o, s_new = gdn_chunked(q, k, v, beta, g, s0)
    else:
        o, s_new = gdn_recurrent(q, k, v, beta, g, s0)
    o = rms_norm(o, p['gdn_norm']) * jax.nn.silu(gate.astype(jnp.float32).reshape(B, T, C_V_HEADS, C_DV))
    y = o.reshape(B, T, C_VAL).astype(h.dtype) @ p['w_out_odd']
    return y, (s_new.astype(h.dtype), buf_new.astype(h.dtype))


def peer_ffn(h, wq, keys, u_tab, v_tab):
    B, T, D = h.shape
    n = B * T
    blk = min(P_BLOCK, n)
    pad = (-n) % blk
    xs = jnp.pad(h.reshape(n, D), ((0, pad), (0, 0))).reshape(-1, blk, D)

    def block(xb):
        q = (xb @ wq).astype(jnp.float32).reshape(blk, P_HEADS, 2, P_DQ // 2)
        s = jnp.einsum('nhcd,chkd->nhck', q, keys.astype(jnp.float32))
        s1, i1 = lax.top_k(s[:, :, 0], P_TOPK)
        s2, i2 = lax.top_k(s[:, :, 1], P_TOPK)
        cand = (s1[..., :, None] + s2[..., None, :]).reshape(blk, P_HEADS, P_TOPK * P_TOPK)
        cidx = (i1[..., :, None] * P_NKEYS + i2[..., None, :]).reshape(blk, P_HEADS, P_TOPK * P_TOPK)
        top, sel = lax.top_k(cand, P_TOPK)
        eidx = jnp.take_along_axis(cidx, sel, axis=-1)
        gate = jax.nn.softmax(top, axis=-1)
        act = jax.nn.gelu(jnp.einsum('nhkd,nd->nhk', u_tab[eidx], xb).astype(jnp.float32), approximate=False)
        return jnp.einsum('nhk,nhkd->nd', (gate * act).astype(xb.dtype), v_tab[eidx])

    y = lax.map(block, xs).reshape(-1, D)[:n]
    return y.reshape(B, T, D)


def trunk(x, pos, W, past):
    even_new, odd_new = [], []
    for l in range(DEPTH):
        i = l // 2
        h = rms_norm(x, W['norm_mix'][l])
        if l % 2 == 0:
            p = {name: W[name][i] for name in EVEN_KEYS}
            y, new = even_mix(h, p, lambda_init(l), pos, past, i)
            even_new.append(new)
        else:
            p = {name: W[name][i] for name in ODD_KEYS}
            y, new = odd_mix(h, p, past, i)
            odd_new.append(new)
        x = x + y
        x = x + peer_ffn(rms_norm(x, W['norm_ffn'][l]), W['peer_wq'][l], W['peer_keys'][l],
                         W['peer_u'][l], W['peer_v'][l])
    x = rms_norm(x, W['norm_final'])
    even = [jnp.stack(t) for t in zip(*even_new)]
    odd = [jnp.stack(t) for t in zip(*odd_new)]
    return x, even, odd


def setup_inputs(seed: int = 0) -> dict:
    key = jax.random.key(seed)
    ks = iter(jax.random.split(key, 64))

    def nrm(shape, scale):
        return jax.random.normal(next(ks), shape, jnp.float32) * scale

    def unif(shape, lo, hi):
        return jax.random.uniform(next(ks), shape, jnp.float32, lo, hi)

    n_pages = PAST_LEN // PAGE_SIZE
    n_used = DEC_BATCH * n_pages
    n_phys = n_used + max(1, n_used // 4)
    page_table = jax.random.permutation(next(ks), n_phys)[:n_used].reshape(DEC_BATCH, n_pages).astype(jnp.int32)
    return {
        'x_prompt': nrm((BATCH, SEQ, D_MODEL), 1.0),
        'x_sample': nrm((DEC_BATCH, DEC_SEQ, D_MODEL), 1.0),
        'cache_k': nrm((N_EVEN, n_phys, PAGE_SIZE, A_HEADS, 2 * A_DH), 1.0),
        'cache_v': nrm((N_EVEN, n_phys, PAGE_SIZE, A_HEADS, A_VD), 1.0),
        'page_table': page_table,
        'state_wkv': nrm((N_EVEN, DEC_BATCH, B_HEADS, B_DH, B_DH), 0.5),
        'state_shift': nrm((N_EVEN, DEC_BATCH, B_PROJ), 1.0),
        'state_gdn': nrm((N_ODD, DEC_BATCH, C_V_HEADS, C_DK, C_DV), 0.5),
        'state_conv': nrm((N_ODD, DEC_BATCH, C_CONV - 1, C_CONV_CH), 1.0),
        'meta': nrm((N_META, D_MODEL), 1.0),
        'norm_mix': 1.0 + nrm((DEPTH, D_MODEL), 0.02),
        'norm_ffn': 1.0 + nrm((DEPTH, D_MODEL), 0.02),
        'norm_final': 1.0 + nrm((D_MODEL,), 0.02),
        'w_in_even': nrm((N_EVEN, D_MODEL, EVEN_PROJ), D_MODEL ** -0.5),
        'w_out_even': nrm((N_EVEN, EVEN_OUT, D_MODEL), EVEN_OUT ** -0.5),
        'lam_q1': nrm((N_EVEN, A_DH), 0.1),
        'lam_k1': nrm((N_EVEN, A_DH), 0.1),
        'lam_q2': nrm((N_EVEN, A_DH), 0.1),
        'lam_k2': nrm((N_EVEN, A_DH), 0.1),
        'subln': 1.0 + nrm((N_EVEN, A_VD), 0.02),
        'rw_mu': unif((N_EVEN, B_PROJ), 0.0, 1.0),
        'rw_w0': unif((N_EVEN, B_WIDTH), -6.0, 0.0),
        'rw_w_up': nrm((N_EVEN, B_W_RANK, B_WIDTH), 0.1),
        'rw_a0': nrm((N_EVEN, B_WIDTH), 0.1),
        'rw_a_up': nrm((N_EVEN, B_A_RANK, B_WIDTH), 0.5 * B_A_RANK ** -0.5),
        'rw_g_up': nrm((N_EVEN, B_G_RANK, B_WIDTH), B_G_RANK ** -0.5),
        'rw_k_k': 0.85 + nrm((N_EVEN, B_WIDTH), 0.02),
        'rw_k_a': 1.0 + nrm((N_EVEN, B_WIDTH), 0.02),
        'rw_r_k': nrm((N_EVEN, B_HEADS, B_DH), 0.1),
        'rw_ln_g': 1.0 + nrm((N_EVEN, B_WIDTH), 0.02),
        'rw_ln_b': nrm((N_EVEN, B_WIDTH), 0.02),
        'w_in_odd': nrm((N_ODD, D_MODEL, ODD_PROJ), D_MODEL ** -0.5),
        'conv_w': nrm((N_ODD, C_CONV, C_CONV_CH), C_CONV ** -0.5),
        'a_log': jnp.log(unif((N_ODD, C_V_HEADS), 1.0, 16.0)),
        'dt_bias': nrm((N_ODD, C_V_HEADS), 0.1),
        'gdn_norm': 1.0 + nrm((N_ODD, C_DV), 0.02),
        'w_out_odd': nrm((N_ODD, C_VAL, D_MODEL), C_VAL ** -0.5),
        'peer_wq': nrm((DEPTH, D_MODEL, P_HEADS * P_DQ), D_MODEL ** -0.5),
        'peer_keys': nrm((DEPTH, 2, P_HEADS, P_NKEYS, P_DQ // 2), (P_DQ // 2) ** -0.5),
        'peer_u': nrm((DEPTH, P_EXPERTS, D_MODEL), D_MODEL ** -0.5),
        'peer_v': nrm((DEPTH, P_EXPERTS, D_MODEL), P_HEADS ** -0.5),
    }


def reference(x_prompt, x_sample, cache_k, cache_v, page_table, state_wkv, state_shift, state_gdn, state_conv,
              meta, norm_mix, norm_ffn, norm_final,
              w_in_even, w_out_even, lam_q1, lam_k1, lam_q2, lam_k2, subln,
              rw_mu, rw_w0, rw_w_up, rw_a0, rw_a_up, rw_g_up, rw_k_k, rw_k_a, rw_r_k, rw_ln_g, rw_ln_b,
              w_in_odd, conv_w, a_log, dt_bias, gdn_norm, w_out_odd,
              peer_wq, peer_keys, peer_u, peer_v):
    W = dict(norm_mix=norm_mix, norm_ffn=norm_ffn, norm_final=norm_final,
             w_in_even=w_in_even, w_out_even=w_out_even, lam_q1=lam_q1, lam_k1=lam_k1,
             lam_q2=lam_q2, lam_k2=lam_k2, subln=subln,
             rw_mu=rw_mu, rw_w0=rw_w0, rw_w_up=rw_w_up, rw_a0=rw_a0, rw_a_up=rw_a_up, rw_g_up=rw_g_up,
             rw_k_k=rw_k_k, rw_k_a=rw_k_a, rw_r_k=rw_r_k, rw_ln_g=rw_ln_g, rw_ln_b=rw_ln_b,
             w_in_odd=w_in_odd, conv_w=conv_w, a_log=a_log, dt_bias=dt_bias, gdn_norm=gdn_norm,
             w_out_odd=w_out_odd, peer_wq=peer_wq, peer_keys=peer_keys, peer_u=peer_u, peer_v=peer_v)
    b_p, s_p, _ = x_prompt.shape
    meta_b = jnp.broadcast_to(meta.astype(x_prompt.dtype)[None], (b_p, N_META, D_MODEL))
    xp = jnp.concatenate([meta_b, x_prompt], axis=1)
    hp, even_p, odd_p = trunk(xp, jnp.arange(N_META + s_p), W, None)
    past_len = page_table.shape[1] * PAGE_SIZE
    past = dict(cache_k=cache_k, cache_v=cache_v, page_table=page_table, state_wkv=state_wkv,
                state_shift=state_shift, state_gdn=state_gdn, state_conv=state_conv)
    hs, even_s, odd_s = trunk(x_sample, past_len + jnp.arange(x_sample.shape[1]), W, past)
    y_prompt = hp[:, N_META:]
    k_p, v_p, wkv_p, shift_p = even_p
    k_s, v_s, wkv_s, shift_s = even_s
    gdn_p, conv_p = odd_p
    gdn_s, conv_s = odd_s
    return (y_prompt, hs, k_p, v_p, k_s, v_s, wkv_p, wkv_s, shift_p, shift_s, gdn_p, gdn_s, conv_p, conv_s)
```

```python
import functools
import math

import jax
import jax.numpy as jnp
from jax import lax
from jax.experimental import pallas as pl
from jax.experimental.pallas import tpu as pltpu

D_MODEL = 2048
SEQ = 8192
DEC_BATCH = 32
N_META = 16
PAGE_SIZE = 128
NORM_EPS = 1e-6
SUBLN_EPS = 1e-5
ROPE_THETA = 10000.0
NEG_INF = -1e30

A_HEADS = 8
A_DH = 64
A_VD = 128
A_WIDTH = 1024

B_HEADS = 16
B_DH = 64
B_WIDTH = 1024
B_W_RANK = 64
B_A_RANK = 64
B_G_RANK = 128
B_PROJ = 3328
B_GN_EPS = 64e-5

C_K_HEADS = 16
C_V_HEADS = 32
C_DK = 128
C_DV = 128
C_KEY = 2048
C_VAL = 4096
C_CONV = 4
C_CONV_CH = 8192

P_HEADS = 8
P_NKEYS = 128
P_TOPK = 16
P_DQ = 256

T_PROMPT = N_META + SEQ
R_ROWS = 8320
SEQ0 = R_ROWS - T_PROMPT
CHUNK = 64

VMEM_LIMIT = 56 * 1024 * 1024

F32 = jnp.float32
BF16 = jnp.bfloat16
HIGHEST = lax.Precision.HIGHEST


def _cparams(*sem):
    return pltpu.CompilerParams(dimension_semantics=sem, vmem_limit_bytes=VMEM_LIMIT)


def _rmsnorm_kernel(x_ref, g_ref, o_ref):
    x = x_ref[...]
    y = x * lax.rsqrt(jnp.mean(x * x, axis=-1, keepdims=True) + NORM_EPS)
    o_ref[...] = (y * g_ref[...]).astype(o_ref.dtype)


def rmsnorm_rows(x, g, out_dtype=BF16, tm=640):
    m, d = x.shape
    return pl.pallas_call(
        _rmsnorm_kernel,
        grid=(m // tm,),
        in_specs=[pl.BlockSpec((tm, d), lambda i: (i, 0)), pl.BlockSpec((1, d), lambda i: (0, 0))],
        out_specs=pl.BlockSpec((tm, d), lambda i: (i, 0)),
        out_shape=jax.ShapeDtypeStruct((m, d), out_dtype),
        compiler_params=_cparams("parallel"),
        name="rmsnorm_rows",
    )(x, g.reshape(1, d))


def _matmul_kernel(a_ref, w_ref, *rest, has_res):
    if has_res:
        r_ref, o_ref, wb_ref = rest
    else:
        o_ref, wb_ref = rest

    @pl.when(pl.program_id(1) == 0)
    def _():
        wb_ref[...] = w_ref[...].astype(BF16)

    acc = jnp.dot(a_ref[...], wb_ref[...], preferred_element_type=F32)
    if has_res:
        acc = acc + r_ref[...]
    o_ref[...] = acc


def matmul_cols(a, w, col0, n, tn, tm=640, residual=None):
    m, k = a.shape
    assert w.shape[0] == k and n % tn == 0 and col0 % tn == 0 and m % tm == 0
    cb0 = col0 // tn
    in_specs = [pl.BlockSpec((tm, k), lambda j, i: (i, 0)),
                pl.BlockSpec((k, tn), lambda j, i: (0, cb0 + j))]
    args = [a, w]
    if residual is not None:
        in_specs.append(pl.BlockSpec((tm, tn), lambda j, i: (i, j)))
        args.append(residual)
    return pl.pallas_call(
        functools.partial(_matmul_kernel, has_res=residual is not None),
        grid=(n // tn, m // tm),
        in_specs=in_specs,
        out_specs=pl.BlockSpec((tm, tn), lambda j, i: (i, j)),
        out_shape=jax.ShapeDtypeStruct((m, n), F32),
        scratch_shapes=[pltpu.VMEM((k, tn), BF16)],
        compiler_params=_cparams("parallel", "arbitrary"),
        name="matmul_cols",
    )(*args)


def _rope_tables(pos):
    half = A_DH // 2
    inv_freq = ROPE_THETA ** (-jnp.arange(half, dtype=F32) / half)
    ang = pos.astype(F32)[:, None] * inv_freq[None, :]
    cos = jnp.tile(jnp.cos(ang), (1, 4))
    sin = jnp.sin(ang)
    sin = jnp.tile(jnp.concatenate([-sin, sin], axis=1), (1, 2))
    return cos, sin


def _rope_kernel(z_ref, cos_ref, sin_ref, qb_ref, kf_ref, kb_ref, vb_ref):
    cos = cos_ref[...]
    sin = sin_ref[...]
    lane = lax.broadcasted_iota(jnp.int32, cos.shape, 1)
    first_half = (lane % A_DH) < (A_DH // 2)

    def rope(x):
        partner = jnp.where(first_half, pltpu.roll(x, 128 - A_DH // 2, 1), pltpu.roll(x, A_DH // 2, 1))
        return x * cos + partner * sin

    for h in range(A_HEADS):
        sl = slice(h * 128, (h + 1) * 128)
        q = rope(z_ref[:, sl])
        qb_ref[:, sl] = (q * (A_DH ** -0.5)).astype(BF16)
        k = rope(z_ref[:, A_WIDTH + h * 128:A_WIDTH + (h + 1) * 128])
        kf_ref[:, sl] = k
        kb_ref[:, sl] = k.astype(BF16)
    vb_ref[...] = z_ref[:, 2 * A_WIDTH:3 * A_WIDTH].astype(BF16)


def rope_qkv(z_att, cos, sin, tm=640):
    m = z_att.shape[0]
    row = lambda w: pl.BlockSpec((tm, w), lambda i: (i, 0))
    return pl.pallas_call(
        _rope_kernel,
        grid=(m // tm,),
        in_specs=[row(3 * A_WIDTH), row(128), row(128)],
        out_specs=[row(A_WIDTH)] * 4,
        out_shape=[jax.ShapeDtypeStruct((m, A_WIDTH), BF16), jax.ShapeDtypeStruct((m, A_WIDTH), F32),
                   jax.ShapeDtypeStruct((m, A_WIDTH), BF16), jax.ShapeDtypeStruct((m, A_WIDTH), BF16)],
        compiler_params=_cparams("parallel"),
        name="rope_qkv",
    )(z_att, cos, sin)


def _lambda(lam_ref, lam_init):
    l1 = jnp.exp(jnp.sum(lam_ref[0:1, :] * lam_ref[1:2, :], axis=-1, keepdims=True))
    l2 = jnp.exp(jnp.sum(lam_ref[2:3, :] * lam_ref[3:4, :], axis=-1, keepdims=True))
    return l1 - l2 + lam_init


def _subln(o, subln_ref, lam_init):
    y = o * lax.rsqrt(jnp.mean(o * o, axis=-1, keepdims=True) + SUBLN_EPS)
    return y * subln_ref[...] * (1.0 - lam_init)


def _attn_prompt_kernel(q_ref, k_ref, v_ref, lam_ref, subln_ref, o_ref, *, tq, lam_init):
    qb = pl.program_id(1)
    q = q_ref[...]
    qidx = qb * tq + lax.broadcasted_iota(jnp.int32, (tq, tq), 0)
    kiota = lax.broadcasted_iota(jnp.int32, (tq, tq), 1)
    dn = (((1,), (1,)), ((), ()))

    def body(kb, carry):
        off = pl.multiple_of(kb * tq, tq)
        k = k_ref[pl.ds(off, tq), :]
        v = v_ref[pl.ds(off, tq), :]
        kidx = off + kiota
        mask = (kidx <= qidx) & (kidx >= SEQ0)
        out = []
        for c in range(2):
            m, l, acc = carry[c]
            s = lax.dot_general(q[:, c * A_DH:(c + 1) * A_DH], k[:, c * A_DH:(c + 1) * A_DH], dn,
                                preferred_element_type=F32)
            s = jnp.where(mask, s, NEG_INF)
            m_new = jnp.maximum(m, jnp.max(s, axis=-1, keepdims=True))
            corr = jnp.exp(m - m_new)
            p = jnp.exp(s - m_new)
            l = l * corr + jnp.sum(p, axis=-1, keepdims=True)
            acc = acc * corr + jnp.dot(p.astype(BF16), v, preferred_element_type=F32)
            out.append((m_new, l, acc))
        return tuple(out)

    init = tuple((jnp.full((tq, 1), NEG_INF, F32), jnp.zeros((tq, 1), F32), jnp.zeros((tq, A_VD), F32))
                 for _ in range(2))
    (_, l1, a1), (_, l2, a2) = lax.fori_loop(0, qb + 1, body, init)
    o = a1 / l1 - _lambda(lam_ref, lam_init) * (a2 / l2)
    o_ref[...] = _subln(o, subln_ref, lam_init).astype(o_ref.dtype)


def attn_prompt(qb, kb, vb, lamvec, subln, lam_init, tq=640):
    m = qb.shape[0]
    return pl.pallas_call(
        functools.partial(_attn_prompt_kernel, tq=tq, lam_init=lam_init),
        grid=(A_HEADS, m // tq),
        in_specs=[pl.BlockSpec((tq, 128), lambda h, i: (i, h)),
                  pl.BlockSpec((m, 128), lambda h, i: (0, h)),
                  pl.BlockSpec((m, 128), lambda h, i: (0, h)),
                  pl.BlockSpec((4, A_DH), lambda h, i: (0, 0)),
                  pl.BlockSpec((1, A_VD), lambda h, i: (0, 0))],
        out_specs=pl.BlockSpec((tq, 128), lambda h, i: (i, h)),
        out_shape=jax.ShapeDtypeStruct((m, A_WIDTH), BF16),
        compiler_params=_cparams("parallel", "arbitrary"),
        name="attn_prompt",
    )(qb, kb, vb, lamvec, subln.reshape(1, A_VD))


def _top_values(s, k):
    n = s.shape[0]
    row = lax.broadcasted_iota(jnp.int32, s.shape, 0).astype(F32)
    vals = []
    for _ in range(k):
        m = jnp.max(s, axis=0, keepdims=True)
        first = jnp.min(jnp.where(s == m, row, float(n)), axis=0, keepdims=True)
        s = jnp.where(row == first, -jnp.inf, s)
        vals.append(m)
    return jnp.concatenate(vals, axis=0)


def _peer_select_kernel(q_ref, keys_ref, c1_ref, e2_ref, thr_ref):
    dn = (((1,), (1,)), ((), ()))

    def head(h, carry):
        qo = pl.multiple_of(h * P_DQ, P_DQ)
        s1 = lax.dot_general(keys_ref[0, h], q_ref[:, pl.ds(qo, 128)], dn,
                             precision=HIGHEST, preferred_element_type=F32)
        s2 = lax.dot_general(keys_ref[1, h], q_ref[:, pl.ds(qo + 128, 128)], dn,
                             precision=HIGHEST, preferred_element_type=F32)
        t1 = _top_values(s1, P_TOPK)
        t2 = _top_values(s2, P_TOPK)
        e1 = jnp.exp(s1 - t1[0:1])
        e2 = jnp.exp(s2 - t2[0:1])
        et1 = jnp.exp(t1 - t1[0:1])
        et2 = jnp.exp(t2 - t2[0:1])
        nb = [P_TOPK // (a + 1) for a in range(P_TOPK)]
        cand = jnp.concatenate([et1[a:a + 1] * et2[:nb[a]] for a in range(P_TOPK)], axis=0)
        top = _top_values(cand, P_TOPK)
        inv_z = 1.0 / jnp.sum(top, axis=0, keepdims=True)
        scaled = jnp.concatenate([(et1[a:a + 1] * inv_z) * et2[:nb[a]] for a in range(P_TOPK)], axis=0)
        thr = jnp.min(jnp.where(cand >= top[P_TOPK - 1:P_TOPK], scaled, jnp.inf), axis=0, keepdims=True)
        ro = pl.multiple_of(h * 128, 128)
        c1_ref[pl.ds(ro, 128), :] = e1 * inv_z
        e2_ref[pl.ds(ro, 128), :] = e2
        thr_ref[pl.ds(h, 1), :] = thr
        return carry

    lax.fori_loop(0, P_HEADS, head, 0)


def peer_select(qp, keys, tm=128):
    m = qp.shape[0]
    col = lambda r: pl.BlockSpec((r, tm), lambda i: (0, i))
    return pl.pallas_call(
        _peer_select_kernel,
        grid=(m // tm,),
        in_specs=[pl.BlockSpec((tm, P_HEADS * P_DQ), lambda i: (i, 0)),
                  pl.BlockSpec((2, P_HEADS, P_NKEYS, 128), lambda i: (0, 0, 0, 0))],
        out_specs=[col(P_HEADS * 128), col(P_HEADS * 128), col(P_HEADS)],
        out_shape=[jax.ShapeDtypeStruct((P_HEADS * 128, m), F32), jax.ShapeDtypeStruct((P_HEADS * 128, m), F32),
                   jax.ShapeDtypeStruct((P_HEADS, m), F32)],
        compiler_params=_cparams("parallel"),
        name="peer_select",
    )(qp, keys)


def _peer_dense_kernel(h_ref, c1_ref, e2_ref, thr_ref, u_ref, v_ref, o_ref, *, te):
    j = pl.program_id(1)

    @pl.when(j == 0)
    def _():
        o_ref[...] = jnp.zeros_like(o_ref)

    ub = u_ref[...].astype(BF16)
    g = lax.dot_general(ub, h_ref[...], (((1,), (1,)), ((), ())), preferred_element_type=F32)
    act = 0.5 * g * (1.0 + lax.erf(g * (2.0 ** -0.5)))
    parts = []
    for ib in range(te // 128):
        i1 = j * (te // 128) + ib
        w = jnp.zeros((128, g.shape[1]), F32)
        for h in range(P_HEADS):
            p = e2_ref[h * 128:(h + 1) * 128, :] * c1_ref[pl.ds(h * 128 + i1, 1), :]
            w = w + jnp.where(p >= thr_ref[h:h + 1, :], p, 0.0)
        parts.append(act[ib * 128:(ib + 1) * 128, :] * w)
    a = jnp.concatenate(parts, axis=0).T.astype(BF16)
    o_ref[...] += jnp.dot(a, v_ref[...].astype(BF16), preferred_element_type=F32)


def peer_dense(hn, c1, e2, thr, u_tab, v_tab, tm=640, te=512):
    m, d = hn.shape
    ne = u_tab.shape[0]
    col = lambda r: pl.BlockSpec((r, tm), lambda i, j: (0, i))
    return pl.pallas_call(
        functools.partial(_peer_dense_kernel, te=te),
        grid=(m // tm, ne // te),
        in_specs=[pl.BlockSpec((tm, d), lambda i, j: (i, 0)),
                  col(P_HEADS * 128), col(P_HEADS * 128), col(P_HEADS),
                  pl.BlockSpec((te, d), lambda i, j: (j, 0)),
                  pl.BlockSpec((te, d), lambda i, j: (j, 0))],
        out_specs=pl.BlockSpec((tm, d), lambda i, j: (i, 0)),
        out_shape=jax.ShapeDtypeStruct((m, d), F32),
        compiler_params=_cparams("parallel", "arbitrary"),
        name="peer_dense",
    )(hn, c1, e2, thr, u_tab, v_tab)


SEQ_PRECISION = HIGHEST


def _dotp(a, b, dims=(((1,), (0,)), ((), ()))):
    return lax.dot_general(a, b, dims, precision=SEQ_PRECISION, preferred_element_type=F32)


_NT = (((1,), (1,)), ((), ()))
_TN = (((0,), (0,)), ((), ()))


def _tri_masks():
    i = lax.broadcasted_iota(jnp.int32, (CHUNK, CHUNK), 0)
    j = lax.broadcasted_iota(jnp.int32, (CHUNK, CHUNK), 1)
    return i >= j, i > j


def _invert_unit_lower_batch(a_ref, at_ref, xt_ref, ti_ref, nmat):
    zpad = jnp.zeros((128 - nmat, 128), F32)
    for t in range(CHUNK):
        slab = a_ref[pl.ds(t, nmat, stride=CHUNK), :]
        at_ref[t * CHUNK:(t + 1) * CHUNK, :] = jnp.concatenate([slab, zpad], axis=0).T[:CHUNK, :]
    sub = lax.broadcasted_iota(jnp.int32, (8, 128), 0)
    for tb in range(CHUNK // 8):
        def row(ti, carry, tb=tb):
            t = tb * 8 + ti
            acc = [jnp.zeros((8, 128), F32) for _ in range(tb)] + [(sub == ti).astype(F32)]
            for sb in range(tb + 1):
                for s in range(8):
                    sg = sb * 8 + s
                    a = at_ref[pl.ds(t * CHUNK + sg, 1), :]
                    for cb in range(sb + 1):
                        acc[cb] = acc[cb] - a * xt_ref[sg * CHUNK + cb * 8:sg * CHUNK + cb * 8 + 8, :]
            for cb in range(tb + 1):
                xt_ref[pl.ds(pl.multiple_of(t * CHUNK + cb * 8, 8), 8), :] = acc[cb]
            return carry
        lax.fori_loop(0, 8, row, 0)
    zrow = jnp.zeros((128 - CHUNK, 128), F32)
    for t in range(CHUNK):
        slab = xt_ref[t * CHUNK:(t + 1) * CHUNK, :]
        ti_ref[pl.ds(t, nmat, stride=CHUNK), :] = jnp.concatenate([slab, zrow], axis=0).T[:nmat, :]


def _softplus(x):
    return jnp.maximum(x, 0.0) + jnp.log(1.0 + jnp.exp(-jnp.abs(x)))


def _sigmoid(x):
    return 1.0 / (1.0 + jnp.exp(-x))


GDN_ROWS = 2 * CHUNK
GDN_NMAT = 2 * C_V_HEADS


def _gdn_kernel(z_ref, ba_ref, bat_ref, convw_ref, alog_ref, dtb_ref, alogt_ref, dtbt_ref, gnorm_ref,
                o_ref, s_ref,
                halo_ref, q_s, k_s, v_s, gcb_ref, bb_ref, gct_ref, a_ref, at_ref, xt_ref, ti_ref, att_ref):
    step = pl.program_id(0)
    nrow = GDN_ROWS

    @pl.when(step == 0)
    def _():
        s_ref[...] = jnp.zeros_like(s_ref)
        halo_ref[...] = jnp.zeros_like(halo_ref)
        xt_ref[...] = jnp.zeros_like(xt_ref)
        a_ref[...] = jnp.zeros_like(a_ref)

    valid = (step * nrow + lax.broadcasted_iota(jnp.int32, (nrow, 1), 0)) >= SEQ0
    valid_t = (step * nrow + lax.broadcasted_iota(jnp.int32, (1, nrow), 1)) >= SEQ0
    beta = jnp.where(valid, _sigmoid(ba_ref[:, :C_V_HEADS]), 0.0)
    g = jnp.where(valid, -jnp.exp(alog_ref[...]) * _softplus(ba_ref[:, C_V_HEADS:] + dtb_ref[...]), 0.0)
    g_t = jnp.where(valid_t, -jnp.exp(alogt_ref[...]) * _softplus(bat_ref[C_V_HEADS:, :] + dtbt_ref[...]), 0.0)
    causal, strict = _tri_masks()
    tri = causal.astype(F32)
    hi = lambda a, b, dims=(((1,), (0,)), ((), ())): lax.dot_general(a, b, dims, precision=HIGHEST,
                                                                      preferred_element_type=F32)
    gc = jnp.concatenate([hi(tri, g[c * CHUNK:(c + 1) * CHUNK]) for c in range(2)], axis=0)
    gct_ref[...] = jnp.concatenate([hi(g_t[:, c * CHUNK:(c + 1) * CHUNK], tri, _NT) for c in range(2)], axis=1)
    for h in range(C_V_HEADS):
        gcb_ref[h] = jnp.broadcast_to(gc[:, h:h + 1], (nrow, 128))
        bb_ref[h] = jnp.broadcast_to(beta[:, h:h + 1], (nrow, 128))

    def conv_act(col0):
        x = z_ref[:, col0:col0 + 128]
        xe = jnp.concatenate([halo_ref[:, col0:col0 + 128], x], axis=0)
        w = convw_ref[:, col0:col0 + 128]
        y = (w[3:4] * x + w[2:3] * pltpu.roll(xe, 1, 0)[8:] + w[1:2] * pltpu.roll(xe, 2, 0)[8:]
             + w[0:1] * pltpu.roll(xe, 3, 0)[8:])
        return y * _sigmoid(y)

    def l2n(x):
        return x * lax.rsqrt(jnp.sum(x * x, axis=-1, keepdims=True) + 1e-6)

    for h in range(C_K_HEADS):
        q_s[:, h * 128:(h + 1) * 128] = l2n(conv_act(h * 128)) * (C_DK ** -0.5)
        k_s[:, h * 128:(h + 1) * 128] = l2n(conv_act(C_KEY + h * 128))
    for h in range(C_V_HEADS):
        v_s[:, h * 128:(h + 1) * 128] = conv_act(2 * C_KEY + h * 128)
    halo_ref[...] = z_ref[nrow - 8:nrow, :C_CONV_CH]

    def build(kh, carry):
        ko = pl.multiple_of(kh * 128, 128)
        for c in range(2):
            rows = slice(c * CHUNK, (c + 1) * CHUNK)
            k = k_s[rows, pl.ds(ko, 128)]
            q = q_s[rows, pl.ds(ko, 128)]
            kk = _dotp(k, k, _NT)
            qk = _dotp(q, k, _NT)
            for r in range(2):
                vh = 2 * kh + r
                diff = gcb_ref[vh][rows, :CHUNK] - gct_ref[pl.ds(vh, 1), c * CHUNK:(c + 1) * CHUNK]
                dec = jnp.where(causal, jnp.exp(jnp.where(causal, diff, 0.0)), 0.0)
                m0 = pl.multiple_of((c * C_V_HEADS + vh) * CHUNK, CHUNK)
                a_ref[pl.ds(m0, CHUNK), :CHUNK] = jnp.where(strict, bb_ref[vh][rows, :CHUNK] * kk * dec, 0.0)
                att_ref[pl.ds(m0, CHUNK), :CHUNK] = jnp.where(causal, qk * dec, 0.0)
        return carry

    lax.fori_loop(0, C_K_HEADS, build, 0)
    _invert_unit_lower_batch(a_ref, at_ref, xt_ref, ti_ref, GDN_NMAT)

    def head(vh, carry):
        kh = vh // 2
        ko = pl.multiple_of(kh * 128, 128)
        vo = pl.multiple_of(vh * 128, 128)
        s = s_ref[vh]
        for c in range(2):
            rows = slice(c * CHUNK, (c + 1) * CHUNK)
            k = k_s[rows, pl.ds(ko, 128)]
            q = q_s[rows, pl.ds(ko, 128)]
            v = v_s[rows, pl.ds(vo, 128)]
            gcc = gcb_ref[vh][rows, :]
            bet = bb_ref[vh][rows, :]
            gl = gcc[CHUNK - 1:CHUNK, :]
            eg = jnp.exp(gcc)
            m0 = pl.multiple_of((c * C_V_HEADS + vh) * CHUNK, CHUNK)
            tinv = ti_ref[pl.ds(m0, CHUNK), :CHUNK]
            att = att_ref[pl.ds(m0, CHUNK), :CHUNK]
            kb = k * bet
            uw = _dotp(tinv, jnp.concatenate([v * bet, kb * eg], axis=1))
            ws = _dotp(jnp.concatenate([uw[:, C_DV:], q * eg], axis=0), s)
            v_new = uw[:, :C_DV] - ws[:CHUNK]
            o = ws[CHUNK:] + _dotp(att, v_new)
            s = s * jnp.exp(gl) + _dotp(k * jnp.exp(gl - gcc), v_new, _TN)
            gate = z_ref[rows, pl.ds(pl.multiple_of(C_CONV_CH + vh * 128, 128), 128)]
            on = o * lax.rsqrt(jnp.mean(o * o, axis=-1, keepdims=True) + NORM_EPS) * gnorm_ref[...]
            o_ref[rows, pl.ds(vo, 128)] = (on * (gate * _sigmoid(gate))).astype(o_ref.dtype)
        s_ref[vh] = s
        return carry

    lax.fori_loop(0, C_V_HEADS, head, 0)


def gdn_prompt(z, ba, conv_w, a_log, dt_bias, gdn_norm):
    m = z.shape[0]
    nrow = GDN_ROWS
    full = lambda shape: pl.BlockSpec(shape, lambda i: (0,) * len(shape))
    mat = lambda: pltpu.VMEM((GDN_NMAT * CHUNK, 128), F32)
    return pl.pallas_call(
        _gdn_kernel,
        grid=(m // nrow,),
        in_specs=[pl.BlockSpec((nrow, C_CONV_CH + C_VAL), lambda i: (i, 0)),
                  pl.BlockSpec((nrow, 2 * C_V_HEADS), lambda i: (i, 0)),
                  pl.BlockSpec((2 * C_V_HEADS, nrow), lambda i: (0, i)),
                  full((C_CONV, C_CONV_CH)), full((1, C_V_HEADS)), full((1, C_V_HEADS)),
                  full((C_V_HEADS, 1)), full((C_V_HEADS, 1)), full((1, C_DV))],
        out_specs=[pl.BlockSpec((nrow, C_VAL), lambda i: (i, 0)),
                   full((C_V_HEADS, C_DK, C_DV))],
        out_shape=[jax.ShapeDtypeStruct((m, C_VAL), BF16),
                   jax.ShapeDtypeStruct((C_V_HEADS, C_DK, C_DV), F32)],
        scratch_shapes=[pltpu.VMEM((8, C_CONV_CH), F32),
                        pltpu.VMEM((nrow, C_KEY), F32), pltpu.VMEM((nrow, C_KEY), F32), pltpu.VMEM((nrow, C_VAL), F32),
                        pltpu.VMEM((C_V_HEADS, nrow, 128), F32), pltpu.VMEM((C_V_HEADS, nrow, 128), F32),
                        pltpu.VMEM((C_V_HEADS, nrow), F32),
                        mat(), mat(), mat(), mat(), mat()],
        compiler_params=_cparams("arbitrary"),
        name="gdn_prompt",
    )(z, ba, ba.T, conv_w, a_log.reshape(1, -1), dt_bias.reshape(1, -1),
      a_log.reshape(-1, 1), dt_bias.reshape(-1, 1), gdn_norm.reshape(1, -1))


RW_CHUNKS = 5
RW_ROWS = RW_CHUNKS * CHUNK
RW_NMAT = RW_CHUNKS * B_HEADS


def _rwkv_kernel(z_ref, mu_ref, w0_ref, wup_ref, a0_ref, aup_ref, gup_ref, kk_ref, ka_ref, rk_ref, lng_ref, lnb_ref,
                 y_ref, s_ref,
                 halo_ref, at_s, bt_s, kt_s, rt_s, v_s, g_s, bon_s, pc_s,
                 a_ref, att_ref, xt_ref, ti_ref, lak_ref, arbk_ref):
    step = pl.program_id(0)
    nrow = RW_ROWS

    @pl.when(step == 0)
    def _():
        s_ref[...] = jnp.zeros_like(s_ref)
        halo_ref[...] = jnp.zeros_like(halo_ref)
        xt_ref[...] = jnp.zeros_like(xt_ref)
        a_ref[...] = jnp.zeros_like(a_ref)

    valid = (step * nrow + lax.broadcasted_iota(jnp.int32, (nrow, 1), 0)) >= SEQ0
    z = z_ref[...]
    zprev = pltpu.roll(jnp.concatenate([halo_ref[...], z], axis=0), 1, 0)[8:]
    halo_ref[...] = z[nrow - 8:nrow]
    zs = z + mu_ref[...] * (zprev - z)
    o1, o2, o3 = B_WIDTH, 2 * B_WIDTH, 3 * B_WIDTH
    o4 = o3 + B_W_RANK
    o5 = o4 + B_A_RANK
    r = zs[:, :o1]
    k = zs[:, o1:o2]
    v = jnp.where(valid, zs[:, o2:o3], 0.0)
    w = -_softplus(-(w0_ref[...] + _dotp(jnp.tanh(zs[:, o3:o4]), wup_ref[...]))) - 0.5
    ld = jnp.where(valid, -jnp.exp(w), 0.0)
    a = _sigmoid(a0_ref[...] + _dotp(zs[:, o4:o5], aup_ref[...]))
    g_s[...] = _dotp(_sigmoid(zs[:, o5:]), gup_ref[...])
    k2 = jnp.where(valid, k * (1.0 + (a - 1.0) * ka_ref[...]), 0.0)
    bon_s[...] = r * k2 * rk_ref[...]
    v_s[...] = v
    kkraw = k * kk_ref[...]
    causal, strict = _tri_masks()
    tri = causal.astype(F32)
    cum = jnp.concatenate(
        [lax.dot_general(tri, ld[c * CHUNK:(c + 1) * CHUNK], (((1,), (0,)), ((), ())), precision=HIGHEST,
                         preferred_element_type=F32) for c in range(RW_CHUNKS)], axis=0)
    for c in range(RW_CHUNKS):
        pc_s[c * 8:(c + 1) * 8, :] = jnp.broadcast_to(jnp.exp(cum[(c + 1) * CHUNK - 1:(c + 1) * CHUNK]), (8, B_WIDTH))
    em = jnp.exp(-cum)
    kt_s[...] = k2 * em
    rt_s[...] = r * jnp.exp(cum)
    ea = jnp.exp(cum - ld)
    for h in range(B_HEADS):
        sl = slice(h * B_DH, (h + 1) * B_DH)
        kh = kkraw[:, sl]
        nrm = jnp.maximum(jnp.sqrt(jnp.sum(kh * kh, axis=-1, keepdims=True)), 1e-12)
        kkn = jnp.where(valid, kh / nrm, 0.0)
        at_s[:, sl] = -kkn * ea[:, sl]
        bt_s[:, sl] = kkn * a[:, sl] * em[:, sl]

    causal2 = (lax.broadcasted_iota(jnp.int32, (CHUNK, 2 * CHUNK), 0)
               >= lax.broadcasted_iota(jnp.int32, (CHUNK, 2 * CHUNK), 1) % CHUNK)

    def operands(po, c, rr):
        rows = slice(c * CHUNK, (c + 1) * CHUNK)
        sl = slice(rr * B_DH, (rr + 1) * B_DH)
        ar = jnp.concatenate([at_s[rows, pl.ds(po, 128)][:, sl], rt_s[rows, pl.ds(po, 128)][:, sl]], axis=0)
        bk = jnp.concatenate([bt_s[rows, pl.ds(po, 128)][:, sl], kt_s[rows, pl.ds(po, 128)][:, sl]], axis=0)
        return ar, bk

    def build(p, carry):
        po = pl.multiple_of(p * 128, 128)
        for c in range(RW_CHUNKS):
            for rr in range(2):
                ar, bk = operands(po, c, rr)
                gm = _dotp(ar, bk, _NT)
                m0 = pl.multiple_of((c * B_HEADS + 2 * p + rr) * CHUNK, CHUNK)
                a_ref[pl.ds(m0, CHUNK), :CHUNK] = jnp.where(strict, -gm[:CHUNK, :CHUNK], 0.0)
                lak_ref[pl.ds(m0, CHUNK), :CHUNK] = jnp.where(strict, gm[:CHUNK, CHUNK:], 0.0)
                arbk_ref[pl.ds(m0, CHUNK), :] = jnp.where(causal2, gm[CHUNK:, :], 0.0)
        return carry

    lax.fori_loop(0, B_HEADS // 2, build, 0)
    _invert_unit_lower_batch(a_ref, att_ref, xt_ref, ti_ref, RW_NMAT)

    def pair(p, carry):
        po = pl.multiple_of(p * 128, 128)
        states = [s_ref[2 * p], s_ref[2 * p + 1]]
        for c in range(RW_CHUNKS):
            rows = slice(c * CHUNK, (c + 1) * CHUNK)
            vp = v_s[rows, pl.ds(po, 128)]
            gp = g_s[rows, pl.ds(po, 128)]
            bp = bon_s[rows, pl.ds(po, 128)]
            pcp = pc_s[c * 8:c * 8 + 1, pl.ds(po, 128)]
            lg = lng_ref[:, pl.ds(po, 128)]
            lb = lnb_ref[:, pl.ds(po, 128)]
            outs = []
            for rr in range(2):
                sl = slice(rr * B_DH, (rr + 1) * B_DH)
                s = states[rr]
                ar, bk = operands(po, c, rr)
                m0 = pl.multiple_of((c * B_HEADS + 2 * p + rr) * CHUNK, CHUNK)
                vh = vp[:, sl]
                ars = _dotp(ar, s, _NT)
                u = _dotp(ti_ref[pl.ds(m0, CHUNK), :CHUNK],
                          ars[:CHUNK] + _dotp(lak_ref[pl.ds(m0, CHUNK), :CHUNK], vh))
                uv = jnp.concatenate([u, vh], axis=0)
                y = ars[CHUNK:] + _dotp(arbk_ref[pl.ds(m0, CHUNK), :], uv)
                pc = pcp[:, sl]
                states[rr] = s * pc + _dotp(uv, bk * pc, _TN)
                mean = jnp.mean(y, axis=-1, keepdims=True)
                var = jnp.mean(jnp.square(y - mean), axis=-1, keepdims=True)
                yn = (y - mean) * lax.rsqrt(var + B_GN_EPS) * lg[:, sl] + lb[:, sl]
                bonus = jnp.sum(bp[:, sl], axis=-1, keepdims=True) * vh
                outs.append((yn + bonus) * gp[:, sl])
            y_ref[rows, pl.ds(po, 128)] = jnp.concatenate(outs, axis=1).astype(y_ref.dtype)
        s_ref[2 * p] = states[0]
        s_ref[2 * p + 1] = states[1]
        return carry

    lax.fori_loop(0, B_HEADS // 2, pair, 0)


def rwkv_prompt(zb, mu, w0, w_up, a0, a_up, g_up, k_k, k_a, r_k, ln_g, ln_b):
    m = zb.shape[0]
    nrow = RW_ROWS
    full = lambda shape: pl.BlockSpec(shape, lambda i: (0,) * len(shape))
    vec = lambda x: x.reshape(1, -1)
    wide = lambda: pltpu.VMEM((nrow, B_WIDTH), F32)
    mat = lambda rows: pltpu.VMEM((rows, 128), F32)
    return pl.pallas_call(
        _rwkv_kernel,
        grid=(m // nrow,),
        in_specs=[pl.BlockSpec((nrow, B_PROJ), lambda i: (i, 0)),
                  full((1, B_PROJ)), full((1, B_WIDTH)), full((B_W_RANK, B_WIDTH)), full((1, B_WIDTH)),
                  full((B_A_RANK, B_WIDTH)), full((B_G_RANK, B_WIDTH)), full((1, B_WIDTH)), full((1, B_WIDTH)),
                  full((1, B_WIDTH)), full((1, B_WIDTH)), full((1, B_WIDTH))],
        out_specs=[pl.BlockSpec((nrow, B_WIDTH), lambda i: (i, 0)), full((B_HEADS, B_DH, B_DH))],
        out_shape=[jax.ShapeDtypeStruct((m, B_WIDTH), BF16), jax.ShapeDtypeStruct((B_HEADS, B_DH, B_DH), F32)],
        scratch_shapes=[pltpu.VMEM((8, B_PROJ), F32),
                        wide(), wide(), wide(), wide(), wide(), wide(), wide(),
                        pltpu.VMEM((8 * RW_CHUNKS, B_WIDTH), F32),
                        mat(RW_NMAT * CHUNK), mat(CHUNK * CHUNK), mat(CHUNK * CHUNK), mat(RW_NMAT * CHUNK),
                        mat(RW_NMAT * CHUNK), mat(RW_NMAT * CHUNK)],
        compiler_params=_cparams("arbitrary"),
        name="rwkv_prompt",
    )(zb, vec(mu), vec(w0), w_up, vec(a0), a_up, g_up, vec(k_k), vec(k_a), vec(r_k), vec(ln_g), vec(ln_b))


def lambda_init(layer):
    return 0.8 - 0.6 * math.exp(-0.3 * layer)


def _decode_attention_glue(qs, k, v, lamvec, subln, lam_init, cache_k, cache_v, page_table):
    bd = qs.shape[0]
    qf = qs.astype(F32).reshape(bd, 1, A_HEADS, 2, A_DH)
    k5 = k.reshape(bd, 1, A_HEADS, 2, A_DH)
    v4 = v.reshape(bd, 1, A_HEADS, A_VD)

    def update(carry, s, vals):
        m, l, acc = carry
        m_new = jnp.maximum(m, s.max(-1))
        corr = jnp.exp(m - m_new)
        p = jnp.exp(s - m_new[..., None])
        return (m_new, l * corr + p.sum(-1),
                acc * corr[..., None] + jnp.einsum('bhcqk,bkhe->bhcqe', p, vals))

    def page_step(carry, phys):
        kp = cache_k[phys].reshape(bd, PAGE_SIZE, A_HEADS, 2, A_DH)
        s = jnp.einsum('bqhcd,bkhcd->bhcqk', qf, kp)
        return update(carry, s, cache_v[phys]), None

    init = (jnp.full((bd, A_HEADS, 2, 1), NEG_INF, F32), jnp.zeros((bd, A_HEADS, 2, 1), F32),
            jnp.zeros((bd, A_HEADS, 2, 1, A_VD), F32))
    carry, _ = lax.scan(page_step, init, page_table.T)
    m, l, acc = update(carry, jnp.einsum('bqhcd,bkhcd->bhcqk', qf, k5), v4)
    o = acc / l[..., None]
    lam = jnp.exp(jnp.sum(lamvec[0] * lamvec[1])) - jnp.exp(jnp.sum(lamvec[2] * lamvec[3])) + lam_init
    o = (o[:, :, 0] - lam * o[:, :, 1]).reshape(bd, A_HEADS, A_VD)
    o = o * lax.rsqrt(jnp.mean(o * o, axis=-1, keepdims=True) + SUBLN_EPS) * subln * (1.0 - lam_init)
    return o.reshape(bd, A_WIDTH)


def _decode_rwkv_glue(zb, prev, s0, mu, w0, w_up, a0, a_up, g_up, k_k, k_a, r_k, ln_g, ln_b):
    bd = zb.shape[0]
    zs = zb + mu * (prev - zb)
    o1, o2, o3 = B_WIDTH, 2 * B_WIDTH, 3 * B_WIDTH
    o4 = o3 + B_W_RANK
    o5 = o4 + B_A_RANK
    r, k, v = zs[:, :o1], zs[:, o1:o2], zs[:, o2:o3]
    w = -jax.nn.softplus(-(w0 + jnp.tanh(zs[:, o3:o4]) @ w_up)) - 0.5
    decay = jnp.exp(-jnp.exp(w))
    a = jax.nn.sigmoid(a0 + zs[:, o4:o5] @ a_up)
    g = jax.nn.sigmoid(zs[:, o5:]) @ g_up
    heads = lambda t: t.reshape(bd, B_HEADS, B_DH)
    kk = heads(k * k_k)
    kk = kk / jnp.maximum(jnp.sqrt(jnp.sum(kk * kk, axis=-1, keepdims=True)), 1e-12)
    k = k * (1.0 + (a - 1.0) * k_a)
    r, k, v, decay, a = heads(r), heads(k), heads(v), heads(decay), heads(a)
    sa = jnp.einsum('bhij,bhj->bhi', s0, -kk)
    s = s0 * decay[:, :, None, :] + sa[..., None] * (kk * a)[:, :, None, :] + v[..., None] * k[:, :, None, :]
    y = jnp.einsum('bhij,bhj->bhi', s, r)
    mean = jnp.mean(y, axis=-1, keepdims=True)
    var = jnp.mean(jnp.square(y - mean), axis=-1, keepdims=True)
    y = ((y - mean) * lax.rsqrt(var + B_GN_EPS)).reshape(bd, B_WIDTH) * ln_g + ln_b
    bonus = jnp.sum(r * k * r_k, axis=-1, keepdims=True) * v
    return (y + bonus.reshape(bd, B_WIDTH)) * g, s


def _decode_gdn_glue(z, ba, buf, s0, conv_w, a_log, dt_bias, gnorm):
    bd = z.shape[0]
    xp = jnp.concatenate([buf, z[:, None, :C_CONV_CH]], axis=1)
    act = jax.nn.silu(jnp.sum(xp * conv_w[None], axis=1))
    l2n = lambda t: t * lax.rsqrt(jnp.sum(t * t, axis=-1, keepdims=True) + 1e-6)
    rep = C_V_HEADS // C_K_HEADS
    q = jnp.repeat(l2n(act[:, :C_KEY].reshape(bd, C_K_HEADS, C_DK)) * (C_DK ** -0.5), rep, axis=1)
    k = jnp.repeat(l2n(act[:, C_KEY:2 * C_KEY].reshape(bd, C_K_HEADS, C_DK)), rep, axis=1)
    v = act[:, 2 * C_KEY:].reshape(bd, C_V_HEADS, C_DV)
    beta = jax.nn.sigmoid(ba[:, :C_V_HEADS])
    g = -jnp.exp(a_log) * jax.nn.softplus(ba[:, C_V_HEADS:] + dt_bias)
    s = s0 * jnp.exp(g)[..., None, None]
    kv = jnp.einsum('bhkv,bhk->bhv', s, k)
    s = s + jnp.einsum('bhk,bhv->bhkv', k, (v - kv) * beta[..., None])
    o = jnp.einsum('bhkv,bhk->bhv', s, q)
    gate = z[:, C_CONV_CH:].reshape(bd, C_V_HEADS, C_DV)
    o = o * lax.rsqrt(jnp.mean(o * o, axis=-1, keepdims=True) + NORM_EPS) * gnorm * jax.nn.silu(gate)
    return o.reshape(bd, C_VAL), s, xp[:, 1:]


def kernel(x_prompt, x_sample, cache_k, cache_v, page_table, state_wkv, state_shift, state_gdn, state_conv, meta, norm_mix, norm_ffn, norm_final, w_in_even, w_out_even, lam_q1, lam_k1, lam_q2, lam_k2, subln, rw_mu, rw_w0, rw_w_up, rw_a0, rw_a_up, rw_g_up, rw_k_k, rw_k_a, rw_r_k, rw_ln_g, rw_ln_b, w_in_odd, conv_w, a_log, dt_bias, gdn_norm, w_out_odd, peer_wq, peer_keys, peer_u, peer_v):
    x = jnp.concatenate([x_sample.reshape(DEC_BATCH, D_MODEL),
                         jnp.zeros((SEQ0 - DEC_BATCH, D_MODEL), F32),
                         meta.astype(F32), x_prompt.reshape(SEQ, D_MODEL)], axis=0)
    past_len = page_table.shape[1] * PAGE_SIZE
    pos = jnp.concatenate([jnp.full((DEC_BATCH,), past_len, jnp.int32),
                           jnp.zeros((SEQ0 - DEC_BATCH,), jnp.int32),
                           jnp.arange(T_PROMPT, dtype=jnp.int32)])
    cos, sin = _rope_tables(pos)
    nd = DEC_BATCH

    def peer(xin, layer):
        hn = rmsnorm_rows(xin, norm_ffn[layer])
        qp = matmul_cols(hn, peer_wq[layer], 0, P_HEADS * P_DQ, tn=1024)
        c1, e2, thr = peer_select(qp, peer_keys[layer])
        return peer_dense(hn, c1, e2, thr, peer_u[layer], peer_v[layer])

    h = rmsnorm_rows(x, norm_mix[0])
    z_att = matmul_cols(h, w_in_even[0], 0, 3 * A_WIDTH, tn=1024)
    zb = matmul_cols(h, w_in_even[0], 3 * A_WIDTH, B_PROJ, tn=256)
    qb, kf, kb, vb = rope_qkv(z_att, cos, sin)
    lamvec = jnp.stack([lam_q1[0], lam_k1[0], lam_q2[0], lam_k2[0]])
    att = attn_prompt(qb, kb, vb, lamvec, subln[0], lambda_init(0))
    rw, wkv_p = rwkv_prompt(zb, rw_mu[0], rw_w0[0], rw_w_up[0], rw_a0[0], rw_a_up[0], rw_g_up[0],
                            rw_k_k[0], rw_k_a[0], rw_r_k[0], rw_ln_g[0], rw_ln_b[0])
    v_f = z_att[:, 2 * A_WIDTH:]
    att_d = _decode_attention_glue(qb[:nd], kf[:nd], v_f[:nd], lamvec, subln[0], lambda_init(0),
                                   cache_k[0], cache_v[0], page_table)
    rw_d, wkv_s = _decode_rwkv_glue(zb[:nd], state_shift[0], state_wkv[0], rw_mu[0], rw_w0[0], rw_w_up[0], rw_a0[0],
                                    rw_a_up[0], rw_g_up[0], rw_k_k[0], rw_k_a[0], rw_r_k[0], rw_ln_g[0], rw_ln_b[0])
    mix = jnp.concatenate([att.at[:nd].set(att_d.astype(BF16)), rw.at[:nd].set(rw_d.astype(BF16))], axis=1)
    x = matmul_cols(mix, w_out_even[0], 0, D_MODEL, tn=1024, residual=x)
    x = x + peer(x, 0)

    h = rmsnorm_rows(x, norm_mix[1])
    z1 = matmul_cols(h, w_in_odd[0], 0, C_CONV_CH + C_VAL, tn=1024)
    ba = matmul_cols(h, w_in_odd[0][:, C_CONV_CH + C_VAL:], 0, 2 * C_V_HEADS, tn=2 * C_V_HEADS)
    o, gdn_p = gdn_prompt(z1, ba, conv_w[0], a_log[0], dt_bias[0], gdn_norm[0])
    o_d, gdn_s, conv_s = _decode_gdn_glue(z1[:nd], ba[:nd], state_conv[0], state_gdn[0], conv_w[0], a_log[0],
                                          dt_bias[0], gdn_norm[0])
    x = matmul_cols(o.at[:nd].set(o_d.astype(BF16)), w_out_odd[0], 0, D_MODEL, tn=512, residual=x)
    x = x + peer(x, 1)
    xf = rmsnorm_rows(x, norm_final, out_dtype=F32)

    y_prompt = xf[SEQ0 + N_META:].reshape(1, SEQ, D_MODEL)
    y_sample = xf[:nd].reshape(nd, 1, D_MODEL)
    k_p = kf[SEQ0:].reshape(1, 1, T_PROMPT, A_HEADS, 2 * A_DH)
    v_p = v_f[SEQ0:].reshape(1, 1, T_PROMPT, A_HEADS, A_VD)
    k_s = kf[:nd].reshape(1, nd, 1, A_HEADS, 2 * A_DH)
    v_s = v_f[:nd].reshape(1, nd, 1, A_HEADS, A_VD)
    shift_p = zb[R_ROWS - 1:].reshape(1, 1, B_PROJ)
    shift_s = zb[:nd].reshape(1, nd, B_PROJ)
    conv_p = z1[R_ROWS - (C_CONV - 1):, :C_CONV_CH].reshape(1, 1, C_CONV - 1, C_CONV_CH)
    return (y_prompt, y_sample, k_p, v_p, k_s, v_s,
            wkv_p.reshape(1, 1, B_HEADS, B_DH, B_DH), wkv_s.reshape(1, nd, B_HEADS, B_DH, B_DH),
            shift_p, shift_s,
            gdn_p.reshape(1, 1, C_V_HEADS, C_DK, C_DV), gdn_s.reshape(1, nd, C_V_HEADS, C_DK, C_DV),
            conv_p, conv_s.reshape(1, nd, C_CONV - 1, C_CONV_CH))
```

```python
import functools
import math

import jax
import jax.numpy as jnp
from jax import lax
from jax.experimental import pallas as pl
from jax.experimental.pallas import tpu as pltpu

D_MODEL = 2048
SEQ = 8192
DEC_BATCH = 32
N_META = 16
PAGE_SIZE = 128
NORM_EPS = 1e-6
SUBLN_EPS = 1e-5
ROPE_THETA = 10000.0
NEG_INF = -1e30

A_HEADS = 8
A_DH = 64
A_VD = 128
A_WIDTH = 1024

B_HEADS = 16
B_DH = 64
B_WIDTH = 1024
B_W_RANK = 64
B_A_RANK = 64
B_G_RANK = 128
B_PROJ = 3328
B_GN_EPS = 64e-5

C_K_HEADS = 16
C_V_HEADS = 32
C_DK = 128
C_DV = 128
C_KEY = 2048
C_VAL = 4096
C_CONV = 4
C_CONV_CH = 8192

P_HEADS = 8
P_NKEYS = 128
P_TOPK = 16
P_DQ = 256

T_PROMPT = N_META + SEQ
R_ROWS = 8320
SEQ0 = R_ROWS - T_PROMPT
CHUNK = 64

VMEM_LIMIT = 56 * 1024 * 1024

F32 = jnp.float32
BF16 = jnp.bfloat16
HIGHEST = lax.Precision.HIGHEST


def _cparams(*sem):
    return pltpu.CompilerParams(dimension_semantics=sem, vmem_limit_bytes=VMEM_LIMIT)


def _rmsnorm_kernel(*refs, has_add, with_sum):
    x = refs[0][...]
    if has_add:
        x = x + refs[1][...]
    g_ref = refs[1 + has_add]
    outs = refs[2 + has_add:]
    if with_sum:
        outs[0][...] = x
    y = x * lax.rsqrt(jnp.mean(x * x, axis=-1, keepdims=True) + NORM_EPS)
    outs[-1][...] = (y * g_ref[...]).astype(outs[-1].dtype)


def rmsnorm_rows(x, g, add=None, with_sum=False, out_dtype=BF16, tm=640):
    m, d = x.shape
    row = pl.BlockSpec((tm, d), lambda i: (i, 0))
    ins = [x] + ([add] if add is not None else [])
    out_shape = [jax.ShapeDtypeStruct((m, d), out_dtype)]
    if with_sum:
        out_shape.insert(0, jax.ShapeDtypeStruct((m, d), F32))
    res = pl.pallas_call(
        functools.partial(_rmsnorm_kernel, has_add=add is not None, with_sum=with_sum),
        grid=(m // tm,),
        in_specs=[row] * len(ins) + [pl.BlockSpec((1, d), lambda i: (0, 0))],
        out_specs=[row] * len(out_shape),
        out_shape=out_shape,
        compiler_params=_cparams("parallel"),
        name="rmsnorm_rows",
    )(*ins, g.reshape(1, d))
    return res if with_sum else res[0]


def _matmul_kernel(a_ref, w_ref, *rest, has_res):
    if has_res:
        r_ref, o_ref, wb_ref = rest
    else:
        o_ref, wb_ref = rest

    @pl.when(pl.program_id(1) == 0)
    def _():
        wb_ref[...] = w_ref[...].astype(BF16)

    acc = jnp.dot(a_ref[...], wb_ref[...], preferred_element_type=F32)
    if has_res:
        acc = acc + r_ref[...]
    o_ref[...] = acc


def matmul_cols(a, w, col0, n, tn, tm=640, residual=None):
    m, k = a.shape
    assert w.shape[0] == k and n % tn == 0 and col0 % tn == 0 and m % tm == 0
    cb0 = col0 // tn
    in_specs = [pl.BlockSpec((tm, k), lambda j, i: (i, 0)),
                pl.BlockSpec((k, tn), lambda j, i: (0, cb0 + j))]
    args = [a, w]
    if residual is not None:
        in_specs.append(pl.BlockSpec((tm, tn), lambda j, i: (i, j)))
        args.append(residual)
    return pl.pallas_call(
        functools.partial(_matmul_kernel, has_res=residual is not None),
        grid=(n // tn, m // tm),
        in_specs=in_specs,
        out_specs=pl.BlockSpec((tm, tn), lambda j, i: (i, j)),
        out_shape=jax.ShapeDtypeStruct((m, n), F32),
        scratch_shapes=[pltpu.VMEM((k, tn), BF16)],
        compiler_params=_cparams("parallel", "arbitrary"),
        name="matmul_cols",
    )(*args)


def _rope_tables(pos):
    half = A_DH // 2
    inv_freq = ROPE_THETA ** (-jnp.arange(half, dtype=F32) / half)
    ang = pos.astype(F32)[:, None] * inv_freq[None, :]
    cos = jnp.tile(jnp.cos(ang), (1, 4))
    sin = jnp.sin(ang)
    sin = jnp.tile(jnp.concatenate([-sin, sin], axis=1), (1, 2))
    return cos, sin


def _rope_kernel(z_ref, cos_ref, sin_ref, qb_ref, kf_ref, kb_ref, vb_ref):
    cos = cos_ref[...]
    sin = sin_ref[...]
    lane = lax.broadcasted_iota(jnp.int32, cos.shape, 1)
    first_half = (lane % A_DH) < (A_DH // 2)

    def rope(x):
        partner = jnp.where(first_half, pltpu.roll(x, 128 - A_DH // 2, 1), pltpu.roll(x, A_DH // 2, 1))
        return x * cos + partner * sin

    for h in range(A_HEADS):
        sl = slice(h * 128, (h + 1) * 128)
        q = rope(z_ref[:, sl])
        qb_ref[:, sl] = (q * (A_DH ** -0.5)).astype(BF16)
        k = rope(z_ref[:, A_WIDTH + h * 128:A_WIDTH + (h + 1) * 128])
        kf_ref[:, sl] = k
        kb_ref[:, sl] = k.astype(BF16)
    vb_ref[...] = z_ref[:, 2 * A_WIDTH:3 * A_WIDTH].astype(BF16)


def rope_qkv(z_att, cos, sin, tm=640):
    m = z_att.shape[0]
    row = lambda w: pl.BlockSpec((tm, w), lambda i: (i, 0))
    return pl.pallas_call(
        _rope_kernel,
        grid=(m // tm,),
        in_specs=[row(3 * A_WIDTH), row(128), row(128)],
        out_specs=[row(A_WIDTH)] * 4,
        out_shape=[jax.ShapeDtypeStruct((m, A_WIDTH), BF16), jax.ShapeDtypeStruct((m, A_WIDTH), F32),
                   jax.ShapeDtypeStruct((m, A_WIDTH), BF16), jax.ShapeDtypeStruct((m, A_WIDTH), BF16)],
        compiler_params=_cparams("parallel"),
        name="rope_qkv",
    )(z_att, cos, sin)


def _lambda(lam_ref, lam_init):
    l1 = jnp.exp(jnp.sum(lam_ref[0:1, :] * lam_ref[1:2, :], axis=-1, keepdims=True))
    l2 = jnp.exp(jnp.sum(lam_ref[2:3, :] * lam_ref[3:4, :], axis=-1, keepdims=True))
    return l1 - l2 + lam_init


def _subln(o, subln_ref, lam_init):
    y = o * lax.rsqrt(jnp.mean(o * o, axis=-1, keepdims=True) + SUBLN_EPS)
    return y * subln_ref[...] * (1.0 - lam_init)


def _attn_prompt_kernel(q_ref, k_ref, v_ref, lam_ref, subln_ref, o_ref, *, tq, lam_init):
    qb = pl.program_id(1)
    q = q_ref[...]
    qidx = qb * tq + lax.broadcasted_iota(jnp.int32, (tq, tq), 0)
    kiota = lax.broadcasted_iota(jnp.int32, (tq, tq), 1)
    dn = (((1,), (1,)), ((), ()))

    def body(kb, carry):
        off = pl.multiple_of(kb * tq, tq)
        k = k_ref[pl.ds(off, tq), :]
        v = v_ref[pl.ds(off, tq), :]
        kidx = off + kiota
        mask = (kidx <= qidx) & (kidx >= SEQ0)
        out = []
        for c in range(2):
            m, l, acc = carry[c]
            s = lax.dot_general(q[:, c * A_DH:(c + 1) * A_DH], k[:, c * A_DH:(c + 1) * A_DH], dn,
                                preferred_element_type=F32)
            s = jnp.where(mask, s, NEG_INF)
            m_new = jnp.maximum(m, jnp.max(s, axis=-1, keepdims=True))
            corr = jnp.exp(m - m_new)
            p = jnp.exp(s - m_new)
            l = l * corr + jnp.sum(p, axis=-1, keepdims=True)
            acc = acc * corr + jnp.dot(p.astype(BF16), v, preferred_element_type=F32)
            out.append((m_new, l, acc))
        return tuple(out)

    init = tuple((jnp.full((tq, 1), NEG_INF, F32), jnp.zeros((tq, 1), F32), jnp.zeros((tq, A_VD), F32))
                 for _ in range(2))
    (_, l1, a1), (_, l2, a2) = lax.fori_loop(0, qb + 1, body, init)
    o = a1 / l1 - _lambda(lam_ref, lam_init) * (a2 / l2)
    o_ref[...] = _subln(o, subln_ref, lam_init).astype(o_ref.dtype)


def attn_prompt(qb, kb, vb, lamvec, subln, lam_init, tq=640):
    m = qb.shape[0]
    return pl.pallas_call(
        functools.partial(_attn_prompt_kernel, tq=tq, lam_init=lam_init),
        grid=(A_HEADS, m // tq),
        in_specs=[pl.BlockSpec((tq, 128), lambda h, i: (i, h)),
                  pl.BlockSpec((m, 128), lambda h, i: (0, h)),
                  pl.BlockSpec((m, 128), lambda h, i: (0, h)),
                  pl.BlockSpec((4, A_DH), lambda h, i: (0, 0)),
                  pl.BlockSpec((1, A_VD), lambda h, i: (0, 0))],
        out_specs=pl.BlockSpec((tq, 128), lambda h, i: (i, h)),
        out_shape=jax.ShapeDtypeStruct((m, A_WIDTH), BF16),
        compiler_params=_cparams("parallel", "arbitrary"),
        name="attn_prompt",
    )(qb, kb, vb, lamvec, subln.reshape(1, A_VD))


PAGES_PER_STEP = 4


def _attn_decode_kernel(pt_ref, q_ref, kn_ref, vn_ref, *rest, lam_init):
    pages = rest[:2 * PAGES_PER_STEP]
    lam_ref, subln_ref, o_ref, m_s, l_s, acc_s = rest[2 * PAGES_PER_STEP:]
    p = pl.program_id(1)

    @pl.when(p == 0)
    def _():
        m_s[...] = jnp.full_like(m_s, NEG_INF)
        l_s[...] = jnp.zeros_like(l_s)
        acc_s[...] = jnp.zeros_like(acc_s)

    q = q_ref[0]
    m, l, acc = m_s[...], l_s[...], acc_s[...]
    ss = [lax.dot_general(q, pages[2 * i][0].astype(BF16), _NT_DIMS, preferred_element_type=F32)
          for i in range(PAGES_PER_STEP)]
    m_new = m
    for s in ss:
        m_new = jnp.maximum(m_new, jnp.max(s, axis=-1, keepdims=True))
    corr = jnp.exp(m - m_new)
    l = l * corr
    acc = acc * corr
    for i, s in enumerate(ss):
        pr = jnp.exp(s - m_new)
        l = l + jnp.sum(pr, axis=-1, keepdims=True)
        acc = acc + jnp.dot(pr.astype(BF16), pages[2 * i + 1][0].astype(BF16), preferred_element_type=F32)
    m = m_new
    m_s[...], l_s[...], acc_s[...] = m, l, acc

    @pl.when(p == pl.num_programs(1) - 1)
    def _():
        kn = kn_ref[0].astype(BF16).astype(F32)
        vn = vn_ref[0].astype(BF16).astype(F32)
        s = jnp.sum(q.astype(F32) * kn, axis=-1, keepdims=True)
        m_new = jnp.maximum(m, s)
        corr = jnp.exp(m - m_new)
        pr = jnp.exp(s - m_new)
        lf = l * corr + pr
        o = (acc * corr + pr.astype(BF16).astype(F32) * vn) / lf
        lam = _lambda(lam_ref, lam_init)
        outs = []
        for h in range(A_HEADS):
            sl = slice(h * A_VD, (h + 1) * A_VD)
            outs.append(_subln(o[2 * h:2 * h + 1, sl] - lam * o[2 * h + 1:2 * h + 2, sl], subln_ref, lam_init))
        o_ref[0] = jnp.concatenate(outs, axis=1).astype(o_ref.dtype)


def attn_decode(qbd, k_new, v_new, cache_k, cache_v, page_table, lamvec, subln, lam_init):
    bd = qbd.shape[0]
    n_pages = page_table.shape[1]
    assert n_pages % PAGES_PER_STEP == 0
    page_specs = []
    for i in range(PAGES_PER_STEP):
        idx = lambda b, p, pt, i=i: (pt[b, p * PAGES_PER_STEP + i], 0, 0)
        page_specs += [pl.BlockSpec((1, PAGE_SIZE, A_WIDTH), idx), pl.BlockSpec((1, PAGE_SIZE, A_WIDTH), idx)]
    per_seq = lambda r: pl.BlockSpec((1, r, A_WIDTH), lambda b, p, pt: (b, 0, 0))
    grid_spec = pltpu.PrefetchScalarGridSpec(
        num_scalar_prefetch=1,
        grid=(bd, n_pages // PAGES_PER_STEP),
        in_specs=[per_seq(2 * A_HEADS), per_seq(1), per_seq(1)] + page_specs
                 + [pl.BlockSpec((4, A_DH), lambda b, p, pt: (0, 0)), pl.BlockSpec((1, A_VD), lambda b, p, pt: (0, 0))],
        out_specs=per_seq(1),
        scratch_shapes=[pltpu.VMEM((2 * A_HEADS, 1), F32), pltpu.VMEM((2 * A_HEADS, 1), F32),
                        pltpu.VMEM((2 * A_HEADS, A_WIDTH), F32)],
    )
    args = [qbd, k_new, v_new] + [cache_k, cache_v] * PAGES_PER_STEP + [lamvec, subln.reshape(1, A_VD)]
    return pl.pallas_call(
        functools.partial(_attn_decode_kernel, lam_init=lam_init),
        grid_spec=grid_spec,
        out_shape=jax.ShapeDtypeStruct((bd, 1, A_WIDTH), BF16),
        compiler_params=_cparams("parallel", "arbitrary"),
        name="attn_decode",
    )(page_table, *args)


def _top_values(s, k):
    n = s.shape[0]
    row = lax.broadcasted_iota(jnp.int32, s.shape, 0).astype(F32)
    vals = []
    for _ in range(k):
        m = jnp.max(s, axis=0, keepdims=True)
        first = jnp.min(jnp.where(s == m, row, float(n)), axis=0, keepdims=True)
        s = jnp.where(row == first, -jnp.inf, s)
        vals.append(m)
    return jnp.concatenate(vals, axis=0)


def _peer_select_kernel(q_ref, keys_ref, c1_ref, e2_ref, thr_ref):
    dn = (((1,), (1,)), ((), ()))

    def head(h, carry):
        qo = pl.multiple_of(h * P_DQ, P_DQ)
        s1 = lax.dot_general(keys_ref[0, h], q_ref[:, pl.ds(qo, 128)], dn,
                             precision=HIGHEST, preferred_element_type=F32)
        s2 = lax.dot_general(keys_ref[1, h], q_ref[:, pl.ds(qo + 128, 128)], dn,
                             precision=HIGHEST, preferred_element_type=F32)
        t1 = _top_values(s1, P_TOPK)
        t2 = _top_values(s2, P_TOPK)
        e1 = jnp.exp(s1 - t1[0:1])
        e2 = jnp.exp(s2 - t2[0:1])
        et1 = jnp.exp(t1 - t1[0:1])
        et2 = jnp.exp(t2 - t2[0:1])
        nb = [P_TOPK // (a + 1) for a in range(P_TOPK)]
        cand = jnp.concatenate([et1[a:a + 1] * et2[:nb[a]] for a in range(P_TOPK)], axis=0)
        top = _top_values(cand, P_TOPK)
        inv_z = 1.0 / jnp.sum(top, axis=0, keepdims=True)
        scaled = jnp.concatenate([(et1[a:a + 1] * inv_z) * et2[:nb[a]] for a in range(P_TOPK)], axis=0)
        thr = jnp.min(jnp.where(cand >= top[P_TOPK - 1:P_TOPK], scaled, jnp.inf), axis=0, keepdims=True)
        ro = pl.multiple_of(h * 128, 128)
        c1_ref[pl.ds(ro, 128), :] = e1 * inv_z
        e2_ref[pl.ds(ro, 128), :] = e2
        thr_ref[pl.ds(h, 1), :] = thr
        return carry

    lax.fori_loop(0, P_HEADS, head, 0)


def peer_select(qp, keys, layer, tm=640):
    m = qp.shape[0]
    col = lambda r: pl.BlockSpec((r, tm), lambda i: (0, i))
    return pl.pallas_call(
        _peer_select_kernel,
        grid=(m // tm,),
        in_specs=[pl.BlockSpec((tm, P_HEADS * P_DQ), lambda i: (i, 0)),
                  pl.BlockSpec((None, 2, P_HEADS, P_NKEYS, 128), lambda i: (layer, 0, 0, 0, 0))],
        out_specs=[col(P_HEADS * 128), col(P_HEADS * 128), col(P_HEADS)],
        out_shape=[jax.ShapeDtypeStruct((P_HEADS * 128, m), F32), jax.ShapeDtypeStruct((P_HEADS * 128, m), F32),
                   jax.ShapeDtypeStruct((P_HEADS, m), F32)],
        compiler_params=_cparams("parallel"),
        name="peer_select",
    )(qp, keys)


def _peer_dense_kernel(h_ref, c1_ref, e2_ref, thr_ref, u_ref, v_ref, o_ref, *, te):
    j = pl.program_id(1)

    @pl.when(j == 0)
    def _():
        o_ref[...] = jnp.zeros_like(o_ref)

    sub = 256
    total = None
    for cb in range(te // sub):
        rows = slice(cb * sub, (cb + 1) * sub)
        g = lax.dot_general(u_ref[rows, :].astype(BF16), h_ref[...], _NT_DIMS, preferred_element_type=F32)
        act = 0.5 * g * (1.0 + lax.erf(g * (2.0 ** -0.5)))
        parts = []
        for ib in range(sub // 128):
            i1 = j * (te // 128) + cb * (sub // 128) + ib
            w = jnp.zeros((128, g.shape[1]), F32)
            for h in range(P_HEADS):
                p = e2_ref[h * 128:(h + 1) * 128, :] * c1_ref[pl.ds(h * 128 + i1, 1), :]
                w = w + jnp.where(p >= thr_ref[h:h + 1, :], p, 0.0)
            parts.append(act[ib * 128:(ib + 1) * 128, :] * w)
        a = jnp.concatenate(parts, axis=0).T.astype(BF16)
        y = jnp.dot(a, v_ref[rows, :].astype(BF16), preferred_element_type=F32)
        total = y if total is None else total + y
    o_ref[...] += total


_NT_DIMS = (((1,), (1,)), ((), ()))


def peer_dense(hn, c1, e2, thr, u_tabs, v_tabs, layer, tm=640, te=512):
    m, d = hn.shape
    ne = u_tabs.shape[1]
    col = lambda r: pl.BlockSpec((r, tm), lambda i, j: (0, i))
    tab = pl.BlockSpec((None, te, d), lambda i, j: (layer, j, 0))
    return pl.pallas_call(
        functools.partial(_peer_dense_kernel, te=te),
        grid=(m // tm, ne // te),
        in_specs=[pl.BlockSpec((tm, d), lambda i, j: (i, 0)),
                  col(P_HEADS * 128), col(P_HEADS * 128), col(P_HEADS), tab, tab],
        out_specs=pl.BlockSpec((tm, d), lambda i, j: (i, 0)),
        out_shape=jax.ShapeDtypeStruct((m, d), F32),
        compiler_params=_cparams("parallel", "arbitrary"),
        name="peer_dense",
    )(hn, c1, e2, thr, u_tabs, v_tabs)


SEQ_PASSES = 1


def _dotp(a, b, dims=(((1,), (0,)), ((), ())), passes=None):
    passes = SEQ_PASSES if passes is None else passes
    dot = lambda x, y: lax.dot_general(x, y, dims, preferred_element_type=F32)
    a_hi = a.astype(BF16)
    b_hi = b.astype(BF16)
    if passes == 1:
        return dot(a_hi, b_hi)
    a_lo = (a - a_hi.astype(F32)).astype(BF16)
    b_lo = (b - b_hi.astype(F32)).astype(BF16)
    return dot(a_hi, b_hi) + dot(a_lo, b_hi) + dot(a_hi, b_lo)


_NT = (((1,), (1,)), ((), ()))
_TN = (((0,), (0,)), ((), ()))


def _tri_masks():
    i = lax.broadcasted_iota(jnp.int32, (CHUNK, CHUNK), 0)
    j = lax.broadcasted_iota(jnp.int32, (CHUNK, CHUNK), 1)
    return i >= j, i > j


def _invert_unit_lower_batch(a_ref, at_ref, xt_ref, ti_ref, nmat):
    zpad = jnp.zeros((128 - nmat, 128), F32)
    for t in range(CHUNK):
        slab = a_ref[pl.ds(t, nmat, stride=CHUNK), :]
        at_ref[t * CHUNK:(t + 1) * CHUNK, :] = jnp.concatenate([slab, zpad], axis=0).T[:CHUNK, :]
    sub = lax.broadcasted_iota(jnp.int32, (8, 128), 0)
    for tb in range(CHUNK // 8):
        def row(ti, carry, tb=tb):
            t = tb * 8 + ti
            acc = [jnp.zeros((8, 128), F32) for _ in range(tb)] + [(sub == ti).astype(F32)]
            for sb in range(tb + 1):
                for s in range(8):
                    sg = sb * 8 + s
                    a = at_ref[pl.ds(t * CHUNK + sg, 1), :]
                    for cb in range(sb + 1):
                        acc[cb] = acc[cb] - a * xt_ref[sg * CHUNK + cb * 8:sg * CHUNK + cb * 8 + 8, :]
            for cb in range(tb + 1):
                xt_ref[pl.ds(pl.multiple_of(t * CHUNK + cb * 8, 8), 8), :] = acc[cb]
            return carry
        lax.fori_loop(0, 8, row, 0)
    zrow = jnp.zeros((128 - CHUNK, 128), F32)
    for t in range(CHUNK):
        slab = xt_ref[t * CHUNK:(t + 1) * CHUNK, :]
        ti_ref[pl.ds(t, nmat, stride=CHUNK), :] = jnp.concatenate([slab, zrow], axis=0).T[:nmat, :]


def _softplus(x):
    return jnp.maximum(x, 0.0) + jnp.log(1.0 + jnp.exp(-jnp.abs(x)))


def _sigmoid(x):
    return 1.0 / (1.0 + jnp.exp(-x))


GDN_ROWS = 2 * CHUNK
GDN_NMAT = 2 * C_V_HEADS
GDN_GROUP = 4


def _gdn_kernel(z_ref, ba_ref, bat_ref, convw_ref, alog_ref, dtb_ref, alogt_ref, dtbt_ref, gnorm_ref,
                o_ref, s_ref,
                halo_ref, q_s, k_s, v_s, gcb_ref, bb_ref, gct_ref, a_ref, at_ref, xt_ref, ti_ref, att_ref):
    step = pl.program_id(0)
    nrow = GDN_ROWS

    @pl.when(step == 0)
    def _():
        s_ref[...] = jnp.zeros_like(s_ref)
        halo_ref[...] = jnp.zeros_like(halo_ref)
        xt_ref[...] = jnp.zeros_like(xt_ref)
        a_ref[...] = jnp.zeros_like(a_ref)

    valid = (step * nrow + lax.broadcasted_iota(jnp.int32, (nrow, 1), 0)) >= SEQ0
    valid_t = (step * nrow + lax.broadcasted_iota(jnp.int32, (1, nrow), 1)) >= SEQ0
    beta = jnp.where(valid, _sigmoid(ba_ref[:, :C_V_HEADS]), 0.0)
    g = jnp.where(valid, -jnp.exp(alog_ref[...]) * _softplus(ba_ref[:, C_V_HEADS:] + dtb_ref[...]), 0.0)
    g_t = jnp.where(valid_t, -jnp.exp(alogt_ref[...]) * _softplus(bat_ref[C_V_HEADS:, :] + dtbt_ref[...]), 0.0)
    causal, strict = _tri_masks()
    tri = causal.astype(F32)
    hi = lambda a, b, dims=(((1,), (0,)), ((), ())): lax.dot_general(a, b, dims, precision=HIGHEST,
                                                                      preferred_element_type=F32)
    gc = jnp.concatenate([hi(tri, g[c * CHUNK:(c + 1) * CHUNK]) for c in range(2)], axis=0)
    gct_ref[...] = jnp.concatenate([hi(g_t[:, c * CHUNK:(c + 1) * CHUNK], tri, _NT) for c in range(2)], axis=1)
    for h in range(C_V_HEADS):
        gcb_ref[h] = jnp.broadcast_to(gc[:, h:h + 1], (nrow, 128))
        bb_ref[h] = jnp.broadcast_to(beta[:, h:h + 1], (nrow, 128))

    def conv_act(col0):
        x = z_ref[:, col0:col0 + 128]
        xe = jnp.concatenate([halo_ref[:, col0:col0 + 128], x], axis=0)
        w = convw_ref[:, col0:col0 + 128]
        y = (w[3:4] * x + w[2:3] * pltpu.roll(xe, 1, 0)[8:] + w[1:2] * pltpu.roll(xe, 2, 0)[8:]
             + w[0:1] * pltpu.roll(xe, 3, 0)[8:])
        return y * _sigmoid(y)

    def l2n(x):
        return x * lax.rsqrt(jnp.sum(x * x, axis=-1, keepdims=True) + 1e-6)

    for h in range(C_K_HEADS):
        q_s[:, h * 128:(h + 1) * 128] = l2n(conv_act(h * 128)) * (C_DK ** -0.5)
        k_s[:, h * 128:(h + 1) * 128] = l2n(conv_act(C_KEY + h * 128))
    for h in range(C_V_HEADS):
        v_s[:, h * 128:(h + 1) * 128] = conv_act(2 * C_KEY + h * 128)
    halo_ref[...] = z_ref[nrow - 8:nrow, :C_CONV_CH]

    def build(kh, carry):
        ko = pl.multiple_of(kh * 128, 128)
        for c in range(2):
            rows = slice(c * CHUNK, (c + 1) * CHUNK)
            k = k_s[rows, pl.ds(ko, 128)]
            q = q_s[rows, pl.ds(ko, 128)]
            kk = _dotp(k, k, _NT)
            qk = _dotp(q, k, _NT)
            for r in range(2):
                vh = 2 * kh + r
                diff = gcb_ref[vh][rows, :CHUNK] - gct_ref[pl.ds(vh, 1), c * CHUNK:(c + 1) * CHUNK]
                dec = jnp.where(causal, jnp.exp(jnp.where(causal, diff, 0.0)), 0.0)
                m0 = pl.multiple_of((c * C_V_HEADS + vh) * CHUNK, CHUNK)
                a_ref[pl.ds(m0, CHUNK), :CHUNK] = jnp.where(strict, bb_ref[vh][rows, :CHUNK] * kk * dec, 0.0)
                att_ref[pl.ds(m0, CHUNK), :CHUNK] = jnp.where(causal, qk * dec, 0.0)
        return carry

    lax.fori_loop(0, C_K_HEADS, build, 0)
    _invert_unit_lower_batch(a_ref, at_ref, xt_ref, ti_ref, GDN_NMAT)

    def head_chain(vh, s):
        kh = vh // 2
        ko = pl.multiple_of(kh * 128, 128)
        vo = pl.multiple_of(vh * 128, 128)
        outs = []
        for c in range(2):
            rows = slice(c * CHUNK, (c + 1) * CHUNK)
            k = k_s[rows, pl.ds(ko, 128)]
            q = q_s[rows, pl.ds(ko, 128)]
            v = v_s[rows, pl.ds(vo, 128)]
            gcc = gcb_ref[vh][rows, :]
            bet = bb_ref[vh][rows, :]
            gl = gcc[CHUNK - 1:CHUNK, :]
            eg = jnp.exp(gcc)
            m0 = pl.multiple_of((c * C_V_HEADS + vh) * CHUNK, CHUNK)
            tinv = ti_ref[pl.ds(m0, CHUNK), :CHUNK]
            att = att_ref[pl.ds(m0, CHUNK), :CHUNK]
            kb = k * bet
            uw = _dotp(tinv, jnp.concatenate([v * bet, kb * eg], axis=1))
            ws = _dotp(jnp.concatenate([uw[:, C_DV:], q * eg], axis=0), s)
            v_new = uw[:, :C_DV] - ws[:CHUNK]
            o = ws[CHUNK:] + _dotp(att, v_new)
            s = s * jnp.exp(gl) + _dotp(k * jnp.exp(gl - gcc), v_new, _TN)
            gate = z_ref[rows, pl.ds(pl.multiple_of(C_CONV_CH + vh * 128, 128), 128)]
            on = o * lax.rsqrt(jnp.mean(o * o, axis=-1, keepdims=True) + NORM_EPS) * gnorm_ref[...]
            outs.append((on * (gate * _sigmoid(gate))).astype(o_ref.dtype))
        return s, outs

    def group(gi, carry):
        heads = [gi * GDN_GROUP + r for r in range(GDN_GROUP)]
        res = [head_chain(vh, s_ref[vh]) for vh in heads]
        for vh, (s, outs) in zip(heads, res):
            s_ref[vh] = s
            for c in range(2):
                o_ref[c * CHUNK:(c + 1) * CHUNK, pl.ds(pl.multiple_of(vh * 128, 128), 128)] = outs[c]
        return carry

    lax.fori_loop(0, C_V_HEADS // GDN_GROUP, group, 0)


def gdn_prompt(z, ba, conv_w, a_log, dt_bias, gdn_norm):
    m = z.shape[0]
    nrow = GDN_ROWS
    full = lambda shape: pl.BlockSpec(shape, lambda i: (0,) * len(shape))
    mat = lambda: pltpu.VMEM((GDN_NMAT * CHUNK, 128), F32)
    return pl.pallas_call(
        _gdn_kernel,
        grid=(m // nrow,),
        in_specs=[pl.BlockSpec((nrow, C_CONV_CH + C_VAL), lambda i: (i, 0)),
                  pl.BlockSpec((nrow, 2 * C_V_HEADS), lambda i: (i, 0)),
                  pl.BlockSpec((2 * C_V_HEADS, nrow), lambda i: (0, i)),
                  full((C_CONV, C_CONV_CH)), full((1, C_V_HEADS)), full((1, C_V_HEADS)),
                  full((C_V_HEADS, 1)), full((C_V_HEADS, 1)), full((1, C_DV))],
        out_specs=[pl.BlockSpec((nrow, C_VAL), lambda i: (i, 0)),
                   full((C_V_HEADS, C_DK, C_DV))],
        out_shape=[jax.ShapeDtypeStruct((m, C_VAL), BF16),
                   jax.ShapeDtypeStruct((C_V_HEADS, C_DK, C_DV), F32)],
        scratch_shapes=[pltpu.VMEM((8, C_CONV_CH), F32),
                        pltpu.VMEM((nrow, C_KEY), F32), pltpu.VMEM((nrow, C_KEY), F32), pltpu.VMEM((nrow, C_VAL), F32),
                        pltpu.VMEM((C_V_HEADS, nrow, 128), F32), pltpu.VMEM((C_V_HEADS, nrow, 128), F32),
                        pltpu.VMEM((C_V_HEADS, nrow), F32),
                        mat(), mat(), mat(), mat(), mat()],
        compiler_params=_cparams("arbitrary"),
        name="gdn_prompt",
    )(z, ba, ba.T, conv_w, a_log.reshape(1, -1), dt_bias.reshape(1, -1),
      a_log.reshape(-1, 1), dt_bias.reshape(-1, 1), gdn_norm.reshape(1, -1))


RW_CHUNKS = 5
RW_ROWS = RW_CHUNKS * CHUNK
RW_NMAT = RW_CHUNKS * B_HEADS
RW_GROUP = 2


def _rwkv_kernel(z_ref, mu_ref, w0_ref, wup_ref, a0_ref, aup_ref, gup_ref, kk_ref, ka_ref, rk_ref, lng_ref, lnb_ref,
                 y_ref, s_ref,
                 halo_ref, at_s, bt_s, kt_s, rt_s, v_s, g_s, bon_s, pc_s,
                 a_ref, att_ref, xt_ref, ti_ref, lak_ref, arbk_ref):
    step = pl.program_id(0)
    nrow = RW_ROWS

    @pl.when(step == 0)
    def _():
        s_ref[...] = jnp.zeros_like(s_ref)
        halo_ref[...] = jnp.zeros_like(halo_ref)
        xt_ref[...] = jnp.zeros_like(xt_ref)
        a_ref[...] = jnp.zeros_like(a_ref)

    valid = (step * nrow + lax.broadcasted_iota(jnp.int32, (nrow, 1), 0)) >= SEQ0
    z = z_ref[...]
    zprev = pltpu.roll(jnp.concatenate([halo_ref[...], z], axis=0), 1, 0)[8:]
    halo_ref[...] = z[nrow - 8:nrow]
    zs = z + mu_ref[...] * (zprev - z)
    o1, o2, o3 = B_WIDTH, 2 * B_WIDTH, 3 * B_WIDTH
    o4 = o3 + B_W_RANK
    o5 = o4 + B_A_RANK
    r = zs[:, :o1]
    k = zs[:, o1:o2]
    v = jnp.where(valid, zs[:, o2:o3], 0.0)
    w = -_softplus(-(w0_ref[...] + _dotp(jnp.tanh(zs[:, o3:o4]), wup_ref[...]))) - 0.5
    ld = jnp.where(valid, -jnp.exp(w), 0.0)
    a = _sigmoid(a0_ref[...] + _dotp(zs[:, o4:o5], aup_ref[...]))
    g_s[...] = _dotp(_sigmoid(zs[:, o5:]), gup_ref[...])
    k2 = jnp.where(valid, k * (1.0 + (a - 1.0) * ka_ref[...]), 0.0)
    bon_s[...] = r * k2 * rk_ref[...]
    v_s[...] = v
    kkraw = k * kk_ref[...]
    causal, strict = _tri_masks()
    tri = causal.astype(F32)
    cum = jnp.concatenate(
        [lax.dot_general(tri, ld[c * CHUNK:(c + 1) * CHUNK], (((1,), (0,)), ((), ())), precision=HIGHEST,
                         preferred_element_type=F32) for c in range(RW_CHUNKS)], axis=0)
    for c in range(RW_CHUNKS):
        pc_s[c * 8:(c + 1) * 8, :] = jnp.broadcast_to(jnp.exp(cum[(c + 1) * CHUNK - 1:(c + 1) * CHUNK]), (8, B_WIDTH))
    em = jnp.exp(-cum)
    kt_s[...] = k2 * em
    rt_s[...] = r * jnp.exp(cum)
    ea = jnp.exp(cum - ld)
    for h in range(B_HEADS):
        sl = slice(h * B_DH, (h + 1) * B_DH)
        kh = kkraw[:, sl]
        nrm = jnp.maximum(jnp.sqrt(jnp.sum(kh * kh, axis=-1, keepdims=True)), 1e-12)
        kkn = jnp.where(valid, kh / nrm, 0.0)
        at_s[:, sl] = -kkn * ea[:, sl]
        bt_s[:, sl] = kkn * a[:, sl] * em[:, sl]

    causal2 = (lax.broadcasted_iota(jnp.int32, (CHUNK, 2 * CHUNK), 0)
               >= lax.broadcasted_iota(jnp.int32, (CHUNK, 2 * CHUNK), 1) % CHUNK)

    def operands(po, c, rr):
        rows = slice(c * CHUNK, (c + 1) * CHUNK)
        sl = slice(rr * B_DH, (rr + 1) * B_DH)
        ar = jnp.concatenate([at_s[rows, pl.ds(po, 128)][:, sl], rt_s[rows, pl.ds(po, 128)][:, sl]], axis=0)
        bk = jnp.concatenate([bt_s[rows, pl.ds(po, 128)][:, sl], kt_s[rows, pl.ds(po, 128)][:, sl]], axis=0)
        return ar, bk

    def build(p, carry):
        po = pl.multiple_of(p * 128, 128)
        for c in range(RW_CHUNKS):
            for rr in range(2):
                ar, bk = operands(po, c, rr)
                gm = _dotp(ar, bk, _NT)
                m0 = pl.multiple_of((c * B_HEADS + 2 * p + rr) * CHUNK, CHUNK)
                a_ref[pl.ds(m0, CHUNK), :CHUNK] = jnp.where(strict, -gm[:CHUNK, :CHUNK], 0.0)
                lak_ref[pl.ds(m0, CHUNK), :CHUNK] = jnp.where(strict, gm[:CHUNK, CHUNK:], 0.0)
                arbk_ref[pl.ds(m0, CHUNK), :] = jnp.where(causal2, gm[CHUNK:, :], 0.0)
        return carry

    lax.fori_loop(0, B_HEADS // 2, build, 0)
    _invert_unit_lower_batch(a_ref, att_ref, xt_ref, ti_ref, RW_NMAT)

    def pair_chain(p, states):
        po = pl.multiple_of(p * 128, 128)
        ys = []
        for c in range(RW_CHUNKS):
            rows = slice(c * CHUNK, (c + 1) * CHUNK)
            vp = v_s[rows, pl.ds(po, 128)]
            gp = g_s[rows, pl.ds(po, 128)]
            bp = bon_s[rows, pl.ds(po, 128)]
            pcp = pc_s[c * 8:c * 8 + 1, pl.ds(po, 128)]
            lg = lng_ref[:, pl.ds(po, 128)]
            lb = lnb_ref[:, pl.ds(po, 128)]
            outs = []
            for rr in range(2):
                sl = slice(rr * B_DH, (rr + 1) * B_DH)
                s = states[rr]
                ar, bk = operands(po, c, rr)
                m0 = pl.multiple_of((c * B_HEADS + 2 * p + rr) * CHUNK, CHUNK)
                vh = vp[:, sl]
                ars = _dotp(ar, s, _NT)
                u = _dotp(ti_ref[pl.ds(m0, CHUNK), :CHUNK],
                          ars[:CHUNK] + _dotp(lak_ref[pl.ds(m0, CHUNK), :CHUNK], vh))
                uv = jnp.concatenate([u, vh], axis=0)
                y = ars[CHUNK:] + _dotp(arbk_ref[pl.ds(m0, CHUNK), :], uv)
                pc = pcp[:, sl]
                states[rr] = s * pc + _dotp(uv, bk * pc, _TN)
                mean = jnp.mean(y, axis=-1, keepdims=True)
                var = jnp.mean(jnp.square(y - mean), axis=-1, keepdims=True)
                yn = (y - mean) * lax.rsqrt(var + B_GN_EPS) * lg[:, sl] + lb[:, sl]
                bonus = jnp.sum(bp[:, sl], axis=-1, keepdims=True) * vh
                outs.append((yn + bonus) * gp[:, sl])
            ys.append(jnp.concatenate(outs, axis=1).astype(y_ref.dtype))
        return states, ys

    def group(gi, carry):
        pairs = [gi * RW_GROUP + r for r in range(RW_GROUP)]
        res = [pair_chain(p, [s_ref[2 * p], s_ref[2 * p + 1]]) for p in pairs]
        for p, (states, ys) in zip(pairs, res):
            s_ref[2 * p] = states[0]
            s_ref[2 * p + 1] = states[1]
            for c in range(RW_CHUNKS):
                y_ref[c * CHUNK:(c + 1) * CHUNK, pl.ds(pl.multiple_of(p * 128, 128), 128)] = ys[c]
        return carry

    lax.fori_loop(0, B_HEADS // 2 // RW_GROUP, group, 0)


def rwkv_prompt(zb, mu, w0, w_up, a0, a_up, g_up, k_k, k_a, r_k, ln_g, ln_b):
    m = zb.shape[0]
    nrow = RW_ROWS
    full = lambda shape: pl.BlockSpec(shape, lambda i: (0,) * len(shape))
    vec = lambda x: x.reshape(1, -1)
    wide = lambda: pltpu.VMEM((nrow, B_WIDTH), F32)
    mat = lambda rows: pltpu.VMEM((rows, 128), F32)
    return pl.pallas_call(
        _rwkv_kernel,
        grid=(m // nrow,),
        in_specs=[pl.BlockSpec((nrow, B_PROJ), lambda i: (i, 0)),
                  full((1, B_PROJ)), full((1, B_WIDTH)), full((B_W_RANK, B_WIDTH)), full((1, B_WIDTH)),
                  full((B_A_RANK, B_WIDTH)), full((B_G_RANK, B_WIDTH)), full((1, B_WIDTH)), full((1, B_WIDTH)),
                  full((1, B_WIDTH)), full((1, B_WIDTH)), full((1, B_WIDTH))],
        out_specs=[pl.BlockSpec((nrow, B_WIDTH), lambda i: (i, 0)), full((B_HEADS, B_DH, B_DH))],
        out_shape=[jax.ShapeDtypeStruct((m, B_WIDTH), BF16), jax.ShapeDtypeStruct((B_HEADS, B_DH, B_DH), F32)],
        scratch_shapes=[pltpu.VMEM((8, B_PROJ), F32),
                        wide(), wide(), wide(), wide(), wide(), wide(), wide(),
                        pltpu.VMEM((8 * RW_CHUNKS, B_WIDTH), F32),
                        mat(RW_NMAT * CHUNK), mat(CHUNK * CHUNK), mat(CHUNK * CHUNK), mat(RW_NMAT * CHUNK),
                        mat(RW_NMAT * CHUNK), mat(RW_NMAT * CHUNK)],
        compiler_params=_cparams("arbitrary"),
        name="rwkv_prompt",
    )(zb, vec(mu), vec(w0), w_up, vec(a0), a_up, g_up, vec(k_k), vec(k_a), vec(r_k), vec(ln_g), vec(ln_b))


def lambda_init(layer):
    return 0.8 - 0.6 * math.exp(-0.3 * layer)


def _row_to_col(row):
    n = row.shape[1]
    eye = lax.broadcasted_iota(jnp.int32, (n, n), 0) == lax.broadcasted_iota(jnp.int32, (n, n), 1)
    return jnp.sum(jnp.where(eye, row, 0.0), axis=1, keepdims=True)


def _col_to_row(col):
    n = col.shape[0]
    eye = lax.broadcasted_iota(jnp.int32, (n, n), 0) == lax.broadcasted_iota(jnp.int32, (n, n), 1)
    return jnp.sum(jnp.where(eye, col, 0.0), axis=0, keepdims=True)


def _gdn_step_kernel(z_ref, ba_ref, buf_ref, s_ref, convw_ref, alog_ref, dtb_ref, gnorm_ref,
                     o_ref, so_ref, bufo_ref):
    mixed = z_ref[0][:, :C_CONV_CH]
    buf = buf_ref[0]
    w = convw_ref[...]
    conv = jnp.sum(buf * w[:C_CONV - 1], axis=0, keepdims=True) + mixed * w[C_CONV - 1:]
    act = conv * _sigmoid(conv)
    bufo_ref[0] = jnp.concatenate([buf[1:], mixed], axis=0)
    ba = ba_ref[0]
    beta = _sigmoid(ba[:, :C_V_HEADS])
    eg = jnp.exp(-jnp.exp(alog_ref[...]) * _softplus(ba[:, C_V_HEADS:] + dtb_ref[...]))
    l2n = lambda t: t * lax.rsqrt(jnp.sum(t * t, axis=-1, keepdims=True) + 1e-6)
    outs = []
    for h in range(C_V_HEADS):
        kh = h // (C_V_HEADS // C_K_HEADS)
        q = _row_to_col(l2n(act[:, kh * C_DK:(kh + 1) * C_DK]) * (C_DK ** -0.5))
        k = _row_to_col(l2n(act[:, C_KEY + kh * C_DK:C_KEY + (kh + 1) * C_DK]))
        v = act[:, 2 * C_KEY + h * C_DV:2 * C_KEY + (h + 1) * C_DV]
        s = s_ref[0, h] * eg[:, h:h + 1]
        kv = jnp.sum(s * k, axis=0, keepdims=True)
        s = s + k * ((v - kv) * beta[:, h:h + 1])
        so_ref[0, h] = s
        o = jnp.sum(s * q, axis=0, keepdims=True)
        gate = z_ref[0][:, C_CONV_CH + h * C_DV:C_CONV_CH + (h + 1) * C_DV]
        on = o * lax.rsqrt(jnp.mean(o * o, axis=-1, keepdims=True) + NORM_EPS) * gnorm_ref[...]
        outs.append(on * (gate * _sigmoid(gate)))
    o_ref[0] = jnp.concatenate(outs, axis=1).astype(o_ref.dtype)


def gdn_step(z, ba, buf, s0, conv_w, a_log, dt_bias, gnorm):
    bd = z.shape[0]
    full = lambda shape: pl.BlockSpec(shape, lambda b: (0,) * len(shape))
    per = lambda *shape: pl.BlockSpec((1,) + shape, lambda b: (b,) + (0,) * len(shape))
    return pl.pallas_call(
        _gdn_step_kernel,
        grid=(bd,),
        in_specs=[per(1, C_CONV_CH + C_VAL), per(1, 2 * C_V_HEADS), per(C_CONV - 1, C_CONV_CH),
                  per(C_V_HEADS, C_DK, C_DV), full((C_CONV, C_CONV_CH)), full((1, C_V_HEADS)),
                  full((1, C_V_HEADS)), full((1, C_DV))],
        out_specs=[per(1, C_VAL), per(C_V_HEADS, C_DK, C_DV), per(C_CONV - 1, C_CONV_CH)],
        out_shape=[jax.ShapeDtypeStruct((bd, 1, C_VAL), BF16),
                   jax.ShapeDtypeStruct((bd, C_V_HEADS, C_DK, C_DV), F32),
                   jax.ShapeDtypeStruct((bd, C_CONV - 1, C_CONV_CH), F32)],
        compiler_params=_cparams("parallel"),
        name="gdn_step",
    )(z.reshape(bd, 1, -1), ba.reshape(bd, 1, -1), buf, s0, conv_w, a_log.reshape(1, -1),
      dt_bias.reshape(1, -1), gnorm.reshape(1, -1))


def _rwkv_step_kernel(z_ref, prev_ref, s_ref, mu_ref, w0_ref, wup_ref, a0_ref, aup_ref, gup_ref, kk_ref, ka_ref,
                      rk_ref, lng_ref, lnb_ref, y_ref, so_ref, r_s, k_s, v_s, kk_s, a_s, w_s):
    z = z_ref[...]
    zs = z + mu_ref[...] * (prev_ref[...] - z)
    o1, o2, o3 = B_WIDTH, 2 * B_WIDTH, 3 * B_WIDTH
    o4 = o3 + B_W_RANK
    o5 = o4 + B_A_RANK
    k = zs[:, o1:o2]
    w = -_softplus(-(w0_ref[...] + _dotp(jnp.tanh(zs[:, o3:o4]), wup_ref[...]))) - 0.5
    a = _sigmoid(a0_ref[...] + _dotp(zs[:, o4:o5], aup_ref[...]))
    g = _dotp(_sigmoid(zs[:, o5:]), gup_ref[...])
    r_s[...] = zs[:, :o1]
    k_s[...] = k * (1.0 + (a - 1.0) * ka_ref[...])
    v_s[...] = zs[:, o2:o3]
    kk_s[...] = k * kk_ref[...]
    a_s[...] = a
    w_s[...] = jnp.exp(-jnp.exp(w))

    def seq(b, carry):
        row = lambda ref: ref[pl.ds(b, 1), :]
        r, k2, v, kkraw, ab, dec = row(r_s), row(k_s), row(v_s), row(kk_s), row(a_s), row(w_s)
        ys = []
        for h in range(B_HEADS):
            sl = slice(h * B_DH, (h + 1) * B_DH)
            kkh = kkraw[:, sl]
            kkh = kkh / jnp.maximum(jnp.sqrt(jnp.sum(kkh * kkh, axis=-1, keepdims=True)), 1e-12)
            s = s_ref[b, h]
            sa = jnp.sum(s * (-kkh), axis=1, keepdims=True)
            s = s * dec[:, sl] + sa * (kkh * ab[:, sl]) + _row_to_col(v[:, sl]) * k2[:, sl]
            so_ref[b, h] = s
            y = _col_to_row(jnp.sum(s * r[:, sl], axis=1, keepdims=True))
            mean = jnp.mean(y, axis=-1, keepdims=True)
            var = jnp.mean(jnp.square(y - mean), axis=-1, keepdims=True)
            ys.append((y - mean) * lax.rsqrt(var + B_GN_EPS) * lng_ref[:, sl] + lnb_ref[:, sl]
                      + jnp.sum(r[:, sl] * k2[:, sl] * rk_ref[:, sl], axis=-1, keepdims=True) * v[:, sl])
        y_ref[pl.ds(b, 1), :] = jnp.concatenate(ys, axis=1)
        return carry

    lax.fori_loop(0, z.shape[0], seq, 0)
    y_ref[...] = y_ref[...] * g


def rwkv_step(zb, prev, s0, mu, w0, w_up, a0, a_up, g_up, k_k, k_a, r_k, ln_g, ln_b):
    bd = zb.shape[0]
    vec = lambda x: x.reshape(1, -1)
    wide = lambda: pltpu.VMEM((bd, B_WIDTH), F32)
    return pl.pallas_call(
        _rwkv_step_kernel,
        out_shape=[jax.ShapeDtypeStruct((bd, B_WIDTH), F32),
                   jax.ShapeDtypeStruct((bd, B_HEADS, B_DH, B_DH), F32)],
        scratch_shapes=[wide() for _ in range(6)],
        compiler_params=pltpu.CompilerParams(vmem_limit_bytes=VMEM_LIMIT),
        name="rwkv_step",
    )(zb, prev, s0, vec(mu), vec(w0), w_up, vec(a0), a_up, g_up, vec(k_k), vec(k_a), vec(r_k), vec(ln_g), vec(ln_b))


def _decode_attention_glue(qs, k, v, lamvec, subln, lam_init, cache_k, cache_v, page_table):
    bd = qs.shape[0]
    qf = qs.astype(F32).reshape(bd, 1, A_HEADS, 2, A_DH)
    k5 = k.reshape(bd, 1, A_HEADS, 2, A_DH)
    v4 = v.reshape(bd, 1, A_HEADS, A_VD)

    def update(carry, s, vals):
        m, l, acc = carry
        m_new = jnp.maximum(m, s.max(-1))
        corr = jnp.exp(m - m_new)
        p = jnp.exp(s - m_new[..., None])
        return (m_new, l * corr + p.sum(-1),
                acc * corr[..., None] + jnp.einsum('bhcqk,bkhe->bhcqe', p, vals))

    def page_step(carry, phys):
        kp = cache_k[phys].reshape(bd, PAGE_SIZE, A_HEADS, 2, A_DH)
        s = jnp.einsum('bqhcd,bkhcd->bhcqk', qf, kp)
        return update(carry, s, cache_v[phys]), None

    init = (jnp.full((bd, A_HEADS, 2, 1), NEG_INF, F32), jnp.zeros((bd, A_HEADS, 2, 1), F32),
            jnp.zeros((bd, A_HEADS, 2, 1, A_VD), F32))
    carry, _ = lax.scan(page_step, init, page_table.T)
    m, l, acc = update(carry, jnp.einsum('bqhcd,bkhcd->bhcqk', qf, k5), v4)
    o = acc / l[..., None]
    lam = jnp.exp(jnp.sum(lamvec[0] * lamvec[1])) - jnp.exp(jnp.sum(lamvec[2] * lamvec[3])) + lam_init
    o = (o[:, :, 0] - lam * o[:, :, 1]).reshape(bd, A_HEADS, A_VD)
    o = o * lax.rsqrt(jnp.mean(o * o, axis=-1, keepdims=True) + SUBLN_EPS) * subln * (1.0 - lam_init)
    return o.reshape(bd, A_WIDTH)


def _decode_rwkv_glue(zb, prev, s0, mu, w0, w_up, a0, a_up, g_up, k_k, k_a, r_k, ln_g, ln_b):
    bd = zb.shape[0]
    zs = zb + mu * (prev - zb)
    o1, o2, o3 = B_WIDTH, 2 * B_WIDTH, 3 * B_WIDTH
    o4 = o3 + B_W_RANK
    o5 = o4 + B_A_RANK
    r, k, v = zs[:, :o1], zs[:, o1:o2], zs[:, o2:o3]
    w = -jax.nn.softplus(-(w0 + jnp.tanh(zs[:, o3:o4]) @ w_up)) - 0.5
    decay = jnp.exp(-jnp.exp(w))
    a = jax.nn.sigmoid(a0 + zs[:, o4:o5] @ a_up)
    g = jax.nn.sigmoid(zs[:, o5:]) @ g_up
    heads = lambda t: t.reshape(bd, B_HEADS, B_DH)
    kk = heads(k * k_k)
    kk = kk / jnp.maximum(jnp.sqrt(jnp.sum(kk * kk, axis=-1, keepdims=True)), 1e-12)
    k = k * (1.0 + (a - 1.0) * k_a)
    r, k, v, decay, a = heads(r), heads(k), heads(v), heads(decay), heads(a)
    sa = jnp.einsum('bhij,bhj->bhi', s0, -kk)
    s = s0 * decay[:, :, None, :] + sa[..., None] * (kk * a)[:, :, None, :] + v[..., None] * k[:, :, None, :]
    y = jnp.einsum('bhij,bhj->bhi', s, r)
    mean = jnp.mean(y, axis=-1, keepdims=True)
    var = jnp.mean(jnp.square(y - mean), axis=-1, keepdims=True)
    y = ((y - mean) * lax.rsqrt(var + B_GN_EPS)).reshape(bd, B_WIDTH) * ln_g + ln_b
    bonus = jnp.sum(r * k * r_k, axis=-1, keepdims=True) * v
    return (y + bonus.reshape(bd, B_WIDTH)) * g, s


def _decode_gdn_glue(z, ba, buf, s0, conv_w, a_log, dt_bias, gnorm):
    bd = z.shape[0]
    xp = jnp.concatenate([buf, z[:, None, :C_CONV_CH]], axis=1)
    act = jax.nn.silu(jnp.sum(xp * conv_w[None], axis=1))
    l2n = lambda t: t * lax.rsqrt(jnp.sum(t * t, axis=-1, keepdims=True) + 1e-6)
    rep = C_V_HEADS // C_K_HEADS
    q = jnp.repeat(l2n(act[:, :C_KEY].reshape(bd, C_K_HEADS, C_DK)) * (C_DK ** -0.5), rep, axis=1)
    k = jnp.repeat(l2n(act[:, C_KEY:2 * C_KEY].reshape(bd, C_K_HEADS, C_DK)), rep, axis=1)
    v = act[:, 2 * C_KEY:].reshape(bd, C_V_HEADS, C_DV)
    beta = jax.nn.sigmoid(ba[:, :C_V_HEADS])
    g = -jnp.exp(a_log) * jax.nn.softplus(ba[:, C_V_HEADS:] + dt_bias)
    s = s0 * jnp.exp(g)[..., None, None]
    kv = jnp.einsum('bhkv,bhk->bhv', s, k)
    s = s + jnp.einsum('bhk,bhv->bhkv', k, (v - kv) * beta[..., None])
    o = jnp.einsum('bhkv,bhk->bhv', s, q)
    gate = z[:, C_CONV_CH:].reshape(bd, C_V_HEADS, C_DV)
    o = o * lax.rsqrt(jnp.mean(o * o, axis=-1, keepdims=True) + NORM_EPS) * gnorm * jax.nn.silu(gate)
    return o.reshape(bd, C_VAL), s, xp[:, 1:]


def kernel(x_prompt, x_sample, cache_k, cache_v, page_table, state_wkv, state_shift, state_gdn, state_conv, meta, norm_mix, norm_ffn, norm_final, w_in_even, w_out_even, lam_q1, lam_k1, lam_q2, lam_k2, subln, rw_mu, rw_w0, rw_w_up, rw_a0, rw_a_up, rw_g_up, rw_k_k, rw_k_a, rw_r_k, rw_ln_g, rw_ln_b, w_in_odd, conv_w, a_log, dt_bias, gdn_norm, w_out_odd, peer_wq, peer_keys, peer_u, peer_v):
    x = jnp.concatenate([x_sample.reshape(DEC_BATCH, D_MODEL),
                         jnp.zeros((SEQ0 - DEC_BATCH, D_MODEL), F32),
                         meta.astype(F32), x_prompt.reshape(SEQ, D_MODEL)], axis=0)
    past_len = page_table.shape[1] * PAGE_SIZE
    pos = jnp.concatenate([jnp.full((DEC_BATCH,), past_len, jnp.int32),
                           jnp.zeros((SEQ0 - DEC_BATCH,), jnp.int32),
                           jnp.arange(T_PROMPT, dtype=jnp.int32)])
    cos, sin = _rope_tables(pos)
    nd = DEC_BATCH

    def peer(xin, layer):
        hn = rmsnorm_rows(xin, norm_ffn[layer])
        qp = matmul_cols(hn, peer_wq[layer], 0, P_HEADS * P_DQ, tn=1024)
        c1, e2, thr = peer_select(qp, peer_keys, layer)
        return peer_dense(hn, c1, e2, thr, peer_u, peer_v, layer)

    h = rmsnorm_rows(x, norm_mix[0])
    z_att = matmul_cols(h, w_in_even[0], 0, 3 * A_WIDTH, tn=1024)
    zb = matmul_cols(h, w_in_even[0], 3 * A_WIDTH, B_PROJ, tn=256)
    qb, kf, kb, vb = rope_qkv(z_att, cos, sin)
    lamvec = jnp.stack([lam_q1[0], lam_k1[0], lam_q2[0], lam_k2[0]])
    att = attn_prompt(qb, kb, vb, lamvec, subln[0], lambda_init(0))
    rw, wkv_p = rwkv_prompt(zb, rw_mu[0], rw_w0[0], rw_w_up[0], rw_a0[0], rw_a_up[0], rw_g_up[0],
                            rw_k_k[0], rw_k_a[0], rw_r_k[0], rw_ln_g[0], rw_ln_b[0])
    v_f = z_att[:, 2 * A_WIDTH:]
    n_phys = cache_k.shape[1]
    lane_map = jnp.arange(A_WIDTH, dtype=jnp.int32)[None, None, :] // A_DH
    qbd = jnp.where(lane_map == jnp.arange(2 * A_HEADS, dtype=jnp.int32)[None, :, None], qb[:nd, None, :], 0)
    att_d = attn_decode(qbd, kf[:nd].reshape(nd, 1, A_WIDTH), v_f[:nd].reshape(nd, 1, A_WIDTH),
                        cache_k.reshape(n_phys, PAGE_SIZE, A_WIDTH), cache_v.reshape(n_phys, PAGE_SIZE, A_WIDTH),
                        page_table, lamvec, subln[0], lambda_init(0)).reshape(nd, A_WIDTH)
    rw_d, wkv_s = rwkv_step(zb[:nd], state_shift[0], state_wkv[0], rw_mu[0], rw_w0[0], rw_w_up[0], rw_a0[0],
                            rw_a_up[0], rw_g_up[0], rw_k_k[0], rw_k_a[0], rw_r_k[0], rw_ln_g[0], rw_ln_b[0])
    mix = jnp.concatenate([att.at[:nd].set(att_d.astype(BF16)), rw.at[:nd].set(rw_d.astype(BF16))], axis=1)
    x = matmul_cols(mix, w_out_even[0], 0, D_MODEL, tn=1024, residual=x)

    x, h = rmsnorm_rows(x, norm_mix[1], add=peer(x, 0), with_sum=True)
    z1 = matmul_cols(h, w_in_odd[0], 0, C_CONV_CH + C_VAL, tn=1024)
    ba = matmul_cols(h, w_in_odd[0][:, C_CONV_CH + C_VAL:], 0, 2 * C_V_HEADS, tn=2 * C_V_HEADS)
    o, gdn_p = gdn_prompt(z1, ba, conv_w[0], a_log[0], dt_bias[0], gdn_norm[0])
    o_d, gdn_s, conv_s = gdn_step(z1[:nd], ba[:nd], state_conv[0], state_gdn[0], conv_w[0], a_log[0],
                                  dt_bias[0], gdn_norm[0])
    x = matmul_cols(o.at[:nd].set(o_d.reshape(nd, C_VAL)), w_out_odd[0], 0, D_MODEL, tn=512, residual=x)
    xf = rmsnorm_rows(x, norm_final, add=peer(x, 1), out_dtype=F32)

    y_prompt = xf[SEQ0 + N_META:].reshape(1, SEQ, D_MODEL)
    y_sample = xf[:nd].reshape(nd, 1, D_MODEL)
    k_p = kf[SEQ0:].reshape(1, 1, T_PROMPT, A_HEADS, 2 * A_DH)
    v_p = v_f[SEQ0:].reshape(1, 1, T_PROMPT, A_HEADS, A_VD)
    k_s = kf[:nd].reshape(1, nd, 1, A_HEADS, 2 * A_DH)
    v_s = v_f[:nd].reshape(1, nd, 1, A_HEADS, A_VD)
    shift_p = zb[R_ROWS - 1:].reshape(1, 1, B_PROJ)
    shift_s = zb[:nd].reshape(1, nd, B_PROJ)
    conv_p = z1[R_ROWS - (C_CONV - 1):, :C_CONV_CH].reshape(1, 1, C_CONV - 1, C_CONV_CH)
    return (y_prompt, y_sample, k_p, v_p, k_s, v_s,
            wkv_p.reshape(1, 1, B_HEADS, B_DH, B_DH), wkv_s.reshape(1, nd, B_HEADS, B_DH, B_DH),
            shift_p, shift_s,
            gdn_p.reshape(1, 1, C_V_HEADS, C_DK, C_DV), gdn_s.reshape(1, nd, C_V_HEADS, C_DK, C_DV),
            conv_p, conv_s.reshape(1, nd, C_CONV - 1, C_CONV_CH))
```

```python
import functools
import math

import jax
import jax.numpy as jnp
from jax import lax
from jax.experimental import pallas as pl
from jax.experimental.pallas import tpu as pltpu

D_MODEL = 2048
SEQ = 8192
DEC_BATCH = 32
N_META = 16
PAGE_SIZE = 128
NORM_EPS = 1e-6
SUBLN_EPS = 1e-5
ROPE_THETA = 10000.0
NEG_INF = -1e30

A_HEADS = 8
A_DH = 64
A_VD = 128
A_WIDTH = 1024

B_HEADS = 16
B_DH = 64
B_WIDTH = 1024
B_W_RANK = 64
B_A_RANK = 64
B_G_RANK = 128
B_PROJ = 3328
B_GN_EPS = 64e-5

C_K_HEADS = 16
C_V_HEADS = 32
C_DK = 128
C_DV = 128
C_KEY = 2048
C_VAL = 4096
C_CONV = 4
C_CONV_CH = 8192

P_HEADS = 8
P_NKEYS = 128
P_TOPK = 16
P_DQ = 256

T_PROMPT = N_META + SEQ
R_ROWS = 8320
SEQ0 = R_ROWS - T_PROMPT
CHUNK = 64

VMEM_LIMIT = 56 * 1024 * 1024

F32 = jnp.float32
BF16 = jnp.bfloat16
HIGHEST = lax.Precision.HIGHEST


def _cparams(*sem):
    return pltpu.CompilerParams(dimension_semantics=sem, vmem_limit_bytes=VMEM_LIMIT)


def _rmsnorm_kernel(*refs, has_add, with_sum):
    x = refs[0][...]
    if has_add:
        x = x + refs[1][...]
    g_ref = refs[1 + has_add]
    outs = refs[2 + has_add:]
    if with_sum:
        outs[0][...] = x
    y = x * lax.rsqrt(jnp.mean(x * x, axis=-1, keepdims=True) + NORM_EPS)
    outs[-1][...] = (y * g_ref[...]).astype(outs[-1].dtype)


def rmsnorm_rows(x, g, add=None, with_sum=False, out_dtype=BF16, tm=640):
    m, d = x.shape
    row = pl.BlockSpec((tm, d), lambda i: (i, 0))
    ins = [x] + ([add] if add is not None else [])
    out_shape = [jax.ShapeDtypeStruct((m, d), out_dtype)]
    if with_sum:
        out_shape.insert(0, jax.ShapeDtypeStruct((m, d), F32))
    res = pl.pallas_call(
        functools.partial(_rmsnorm_kernel, has_add=add is not None, with_sum=with_sum),
        grid=(m // tm,),
        in_specs=[row] * len(ins) + [pl.BlockSpec((1, d), lambda i: (0, 0))],
        out_specs=[row] * len(out_shape),
        out_shape=out_shape,
        compiler_params=_cparams("parallel"),
        name="rmsnorm_rows",
    )(*ins, g.reshape(1, d))
    return res if with_sum else res[0]


def _matmul_kernel(a_ref, w_ref, *rest, has_res):
    if has_res:
        r_ref, o_ref, wb_ref = rest
    else:
        o_ref, wb_ref = rest

    @pl.when(pl.program_id(1) == 0)
    def _():
        wb_ref[...] = w_ref[...].astype(BF16)

    acc = jnp.dot(a_ref[...], wb_ref[...], preferred_element_type=F32)
    if has_res:
        acc = acc + r_ref[...]
    o_ref[...] = acc


def matmul_cols(a, w, col0, n, tn, tm=640, residual=None):
    m, k = a.shape
    assert w.shape[0] == k and n % tn == 0 and col0 % tn == 0 and m % tm == 0
    cb0 = col0 // tn
    in_specs = [pl.BlockSpec((tm, k), lambda j, i: (i, 0)),
                pl.BlockSpec((k, tn), lambda j, i: (0, cb0 + j))]
    args = [a, w]
    if residual is not None:
        in_specs.append(pl.BlockSpec((tm, tn), lambda j, i: (i, j)))
        args.append(residual)
    return pl.pallas_call(
        functools.partial(_matmul_kernel, has_res=residual is not None),
        grid=(n // tn, m // tm),
        in_specs=in_specs,
        out_specs=pl.BlockSpec((tm, tn), lambda j, i: (i, j)),
        out_shape=jax.ShapeDtypeStruct((m, n), F32),
        scratch_shapes=[pltpu.VMEM((k, tn), BF16)],
        compiler_params=_cparams("parallel", "arbitrary"),
        name="matmul_cols",
    )(*args)


def _rope_tables(pos):
    half = A_DH // 2
    inv_freq = ROPE_THETA ** (-jnp.arange(half, dtype=F32) / half)
    ang = pos.astype(F32)[:, None] * inv_freq[None, :]
    cos = jnp.tile(jnp.cos(ang), (1, 4))
    sin = jnp.sin(ang)
    sin = jnp.tile(jnp.concatenate([-sin, sin], axis=1), (1, 2))
    return cos, sin


def _rope_kernel(z_ref, cos_ref, sin_ref, qb_ref, kf_ref, kb_ref, vb_ref):
    cos = cos_ref[...]
    sin = sin_ref[...]
    lane = lax.broadcasted_iota(jnp.int32, cos.shape, 1)
    first_half = (lane % A_DH) < (A_DH // 2)

    def rope(x):
        partner = jnp.where(first_half, pltpu.roll(x, 128 - A_DH // 2, 1), pltpu.roll(x, A_DH // 2, 1))
        return x * cos + partner * sin

    for h in range(A_HEADS):
        sl = slice(h * 128, (h + 1) * 128)
        q = rope(z_ref[:, sl])
        qb_ref[:, sl] = (q * (A_DH ** -0.5)).astype(BF16)
        k = rope(z_ref[:, A_WIDTH + h * 128:A_WIDTH + (h + 1) * 128])
        kf_ref[:, sl] = k
        kb_ref[:, sl] = k.astype(BF16)
    vb_ref[...] = z_ref[:, 2 * A_WIDTH:3 * A_WIDTH].astype(BF16)


def rope_qkv(z_att, cos, sin, tm=640):
    m = z_att.shape[0]
    row = lambda w: pl.BlockSpec((tm, w), lambda i: (i, 0))
    return pl.pallas_call(
        _rope_kernel,
        grid=(m // tm,),
        in_specs=[row(3 * A_WIDTH), row(128), row(128)],
        out_specs=[row(A_WIDTH)] * 4,
        out_shape=[jax.ShapeDtypeStruct((m, A_WIDTH), BF16), jax.ShapeDtypeStruct((m, A_WIDTH), F32),
                   jax.ShapeDtypeStruct((m, A_WIDTH), BF16), jax.ShapeDtypeStruct((m, A_WIDTH), BF16)],
        compiler_params=_cparams("parallel"),
        name="rope_qkv",
    )(z_att, cos, sin)


def _lambda(lam_ref, lam_init):
    l1 = jnp.exp(jnp.sum(lam_ref[0:1, :] * lam_ref[1:2, :], axis=-1, keepdims=True))
    l2 = jnp.exp(jnp.sum(lam_ref[2:3, :] * lam_ref[3:4, :], axis=-1, keepdims=True))
    return l1 - l2 + lam_init


def _subln(o, subln_ref, lam_init):
    y = o * lax.rsqrt(jnp.mean(o * o, axis=-1, keepdims=True) + SUBLN_EPS)
    return y * subln_ref[...] * (1.0 - lam_init)


def _attn_prompt_kernel(q_ref, k_ref, v_ref, lam_ref, subln_ref, o_ref, *, tq, lam_init):
    qb = pl.program_id(1)
    q = q_ref[...]
    qidx = qb * tq + lax.broadcasted_iota(jnp.int32, (tq, tq), 0)
    kiota = lax.broadcasted_iota(jnp.int32, (tq, tq), 1)
    dn = (((1,), (1,)), ((), ()))

    def body(kb, carry, masked):
        off = pl.multiple_of(kb * tq, tq)
        k = k_ref[pl.ds(off, tq), :]
        v = v_ref[pl.ds(off, tq), :]
        if masked:
            kidx = off + kiota
            mask = (kidx <= qidx) & (kidx >= SEQ0)
        out = []
        for c in range(2):
            m, l, acc = carry[c]
            s = lax.dot_general(q[:, c * A_DH:(c + 1) * A_DH], k[:, c * A_DH:(c + 1) * A_DH], dn,
                                preferred_element_type=F32)
            if masked:
                s = jnp.where(mask, s, NEG_INF)
            m_new = jnp.maximum(m, jnp.max(s, axis=-1, keepdims=True))
            corr = jnp.exp(m - m_new)
            p = jnp.exp(s - m_new)
            l = l * corr + jnp.sum(p, axis=-1, keepdims=True)
            acc = acc * corr + jnp.dot(p.astype(BF16), v, preferred_element_type=F32)
            out.append((m_new, l, acc))
        return tuple(out)

    init = tuple((jnp.full((tq, 1), NEG_INF, F32), jnp.zeros((tq, 1), F32), jnp.zeros((tq, A_VD), F32))
                 for _ in range(2))
    carry = body(0, init, True)
    carry = lax.fori_loop(1, qb, functools.partial(body, masked=False), carry)
    (_, l1, a1), (_, l2, a2) = lax.cond(qb > 0, lambda c: body(qb, c, True), lambda c: c, carry)
    o = a1 / l1 - _lambda(lam_ref, lam_init) * (a2 / l2)
    o_ref[...] = _subln(o, subln_ref, lam_init).astype(o_ref.dtype)


def attn_prompt(qb, kb, vb, lamvec, subln, lam_init, tq=640):
    m = qb.shape[0]
    return pl.pallas_call(
        functools.partial(_attn_prompt_kernel, tq=tq, lam_init=lam_init),
        grid=(A_HEADS, m // tq),
        in_specs=[pl.BlockSpec((tq, 128), lambda h, i: (i, h)),
                  pl.BlockSpec((m, 128), lambda h, i: (0, h)),
                  pl.BlockSpec((m, 128), lambda h, i: (0, h)),
                  pl.BlockSpec((4, A_DH), lambda h, i: (0, 0)),
                  pl.BlockSpec((1, A_VD), lambda h, i: (0, 0))],
        out_specs=pl.BlockSpec((tq, 128), lambda h, i: (i, h)),
        out_shape=jax.ShapeDtypeStruct((m, A_WIDTH), BF16),
        compiler_params=_cparams("parallel", "arbitrary"),
        name="attn_prompt",
    )(qb, kb, vb, lamvec, subln.reshape(1, A_VD))


PAGES_PER_STEP = 4


def _attn_decode_kernel(pt_ref, q_ref, kn_ref, vn_ref, *rest, lam_init):
    pages = rest[:2 * PAGES_PER_STEP]
    lam_ref, subln_ref, o_ref, m_s, l_s, acc_s = rest[2 * PAGES_PER_STEP:]
    p = pl.program_id(1)

    @pl.when(p == 0)
    def _():
        m_s[...] = jnp.full_like(m_s, NEG_INF)
        l_s[...] = jnp.zeros_like(l_s)
        acc_s[...] = jnp.zeros_like(acc_s)

    def page(ref):
        return jnp.concatenate([ref[0, :, h, :] for h in range(A_HEADS)], axis=1).astype(BF16)

    q = q_ref[0]
    m, l, acc = m_s[...], l_s[...], acc_s[...]
    ss = [lax.dot_general(q, page(pages[2 * i]), _NT_DIMS, preferred_element_type=F32)
          for i in range(PAGES_PER_STEP)]
    m_new = m
    for s in ss:
        m_new = jnp.maximum(m_new, jnp.max(s, axis=-1, keepdims=True))
    corr = jnp.exp(m - m_new)
    l = l * corr
    acc = acc * corr
    for i, s in enumerate(ss):
        pr = jnp.exp(s - m_new)
        l = l + jnp.sum(pr, axis=-1, keepdims=True)
        acc = acc + jnp.dot(pr.astype(BF16), page(pages[2 * i + 1]), preferred_element_type=F32)
    m = m_new
    m_s[...], l_s[...], acc_s[...] = m, l, acc

    @pl.when(p == pl.num_programs(1) - 1)
    def _():
        kn = kn_ref[0].astype(BF16).astype(F32)
        vn = vn_ref[0].astype(BF16).astype(F32)
        s = jnp.sum(q.astype(F32) * kn, axis=-1, keepdims=True)
        m_new = jnp.maximum(m, s)
        corr = jnp.exp(m - m_new)
        pr = jnp.exp(s - m_new)
        lf = l * corr + pr
        o = (acc * corr + pr.astype(BF16).astype(F32) * vn) / lf
        lam = _lambda(lam_ref, lam_init)
        outs = []
        for h in range(A_HEADS):
            sl = slice(h * A_VD, (h + 1) * A_VD)
            outs.append(_subln(o[2 * h:2 * h + 1, sl] - lam * o[2 * h + 1:2 * h + 2, sl], subln_ref, lam_init))
        o_ref[0] = jnp.concatenate(outs, axis=1).astype(o_ref.dtype)


def attn_decode(qbd, k_new, v_new, cache_k, cache_v, page_table, lamvec, subln, lam_init):
    bd = qbd.shape[0]
    n_pages = page_table.shape[1]
    assert n_pages % PAGES_PER_STEP == 0
    page_specs = []
    for i in range(PAGES_PER_STEP):
        idx = lambda b, p, pt, i=i: (0, pt[b, p * PAGES_PER_STEP + i], 0, 0, 0)
        spec = pl.BlockSpec((None, 1, PAGE_SIZE, A_HEADS, 128), idx)
        page_specs += [spec, spec]
    per_seq = lambda r: pl.BlockSpec((1, r, A_WIDTH), lambda b, p, pt: (b, 0, 0))
    grid_spec = pltpu.PrefetchScalarGridSpec(
        num_scalar_prefetch=1,
        grid=(bd, n_pages // PAGES_PER_STEP),
        in_specs=[per_seq(2 * A_HEADS), per_seq(1), per_seq(1)] + page_specs
                 + [pl.BlockSpec((4, A_DH), lambda b, p, pt: (0, 0)), pl.BlockSpec((1, A_VD), lambda b, p, pt: (0, 0))],
        out_specs=per_seq(1),
        scratch_shapes=[pltpu.VMEM((2 * A_HEADS, 1), F32), pltpu.VMEM((2 * A_HEADS, 1), F32),
                        pltpu.VMEM((2 * A_HEADS, A_WIDTH), F32)],
    )
    args = [qbd, k_new, v_new] + [cache_k, cache_v] * PAGES_PER_STEP + [lamvec, subln.reshape(1, A_VD)]
    return pl.pallas_call(
        functools.partial(_attn_decode_kernel, lam_init=lam_init),
        grid_spec=grid_spec,
        out_shape=jax.ShapeDtypeStruct((bd, 1, A_WIDTH), BF16),
        compiler_params=_cparams("parallel", "arbitrary"),
        name="attn_decode",
    )(page_table, *args)


def _top_values(s, k):
    n = s.shape[0]
    row = lax.broadcasted_iota(jnp.int32, s.shape, 0).astype(F32)
    vals = []
    for _ in range(k):
        m = jnp.max(s, axis=0, keepdims=True)
        first = jnp.min(jnp.where(s == m, row, float(n)), axis=0, keepdims=True)
        s = jnp.where(row == first, -jnp.inf, s)
        vals.append(m)
    return jnp.concatenate(vals, axis=0)


def _peer_select_kernel(q_ref, keys_ref, c1_ref, e2_ref, thr_ref):
    dn = (((1,), (1,)), ((), ()))

    def head(h, carry):
        qo = pl.multiple_of(h * P_DQ, P_DQ)
        s1 = lax.dot_general(keys_ref[0, h], q_ref[:, pl.ds(qo, 128)], dn,
                             precision=HIGHEST, preferred_element_type=F32)
        s2 = lax.dot_general(keys_ref[1, h], q_ref[:, pl.ds(qo + 128, 128)], dn,
                             precision=HIGHEST, preferred_element_type=F32)
        t1 = _top_values(s1, P_TOPK)
        t2 = _top_values(s2, P_TOPK)
        e1 = jnp.exp(s1 - t1[0:1])
        e2 = jnp.exp(s2 - t2[0:1])
        et1 = jnp.exp(t1 - t1[0:1])
        et2 = jnp.exp(t2 - t2[0:1])
        nb = [P_TOPK // (a + 1) for a in range(P_TOPK)]
        cand = jnp.concatenate([et1[a:a + 1] * et2[:nb[a]] for a in range(P_TOPK)], axis=0)
        top = _top_values(cand, P_TOPK)
        inv_z = 1.0 / jnp.sum(top, axis=0, keepdims=True)
        scaled = jnp.concatenate([(et1[a:a + 1] * inv_z) * et2[:nb[a]] for a in range(P_TOPK)], axis=0)
        thr = jnp.min(jnp.where(cand >= top[P_TOPK - 1:P_TOPK], scaled, jnp.inf), axis=0, keepdims=True)
        ro = pl.multiple_of(h * 128, 128)
        c1_ref[pl.ds(ro, 128), :] = e1 * inv_z
        e2_ref[pl.ds(ro, 128), :] = e2
        thr_ref[pl.ds(h, 1), :] = thr
        return carry

    lax.fori_loop(0, P_HEADS, head, 0)


def peer_select(qp, keys, layer, tm=640):
    m = qp.shape[0]
    col = lambda r: pl.BlockSpec((r, tm), lambda i: (0, i))
    return pl.pallas_call(
        _peer_select_kernel,
        grid=(m // tm,),
        in_specs=[pl.BlockSpec((tm, P_HEADS * P_DQ), lambda i: (i, 0)),
                  pl.BlockSpec((None, 2, P_HEADS, P_NKEYS, 128), lambda i: (layer, 0, 0, 0, 0))],
        out_specs=[col(P_HEADS * 128), col(P_HEADS * 128), col(P_HEADS)],
        out_shape=[jax.ShapeDtypeStruct((P_HEADS * 128, m), F32), jax.ShapeDtypeStruct((P_HEADS * 128, m), F32),
                   jax.ShapeDtypeStruct((P_HEADS, m), F32)],
        compiler_params=_cparams("parallel"),
        name="peer_select",
    )(qp, keys)


def _peer_dense_kernel(h_ref, c1_ref, e2_ref, thr_ref, u_ref, v_ref, o_ref, *, te):
    j = pl.program_id(1)

    @pl.when(j == 0)
    def _():
        o_ref[...] = jnp.zeros_like(o_ref)

    sub = 256
    total = None
    for cb in range(te // sub):
        rows = slice(cb * sub, (cb + 1) * sub)
        g = lax.dot_general(u_ref[rows, :].astype(BF16), h_ref[...], _NT_DIMS, preferred_element_type=F32)
        act = 0.5 * g * (1.0 + lax.erf(g * (2.0 ** -0.5)))
        parts = []
        for ib in range(sub // 128):
            i1 = j * (te // 128) + cb * (sub // 128) + ib
            w = jnp.zeros((128, g.shape[1]), F32)
            for h in range(P_HEADS):
                p = e2_ref[h * 128:(h + 1) * 128, :] * c1_ref[pl.ds(h * 128 + i1, 1), :]
                w = w + jnp.where(p >= thr_ref[h:h + 1, :], p, 0.0)
            parts.append(act[ib * 128:(ib + 1) * 128, :] * w)
        a = jnp.concatenate(parts, axis=0).T.astype(BF16)
        y = jnp.dot(a, v_ref[rows, :].astype(BF16), preferred_element_type=F32)
        total = y if total is None else total + y
    o_ref[...] += total


_NT_DIMS = (((1,), (1,)), ((), ()))


def peer_dense(hn, c1, e2, thr, u_tabs, v_tabs, layer, tm=640, te=512):
    m, d = hn.shape
    ne = u_tabs.shape[1]
    col = lambda r: pl.BlockSpec((r, tm), lambda i, j: (0, i))
    tab = pl.BlockSpec((None, te, d), lambda i, j: (layer, j, 0))
    return pl.pallas_call(
        functools.partial(_peer_dense_kernel, te=te),
        grid=(m // tm, ne // te),
        in_specs=[pl.BlockSpec((tm, d), lambda i, j: (i, 0)),
                  col(P_HEADS * 128), col(P_HEADS * 128), col(P_HEADS), tab, tab],
        out_specs=pl.BlockSpec((tm, d), lambda i, j: (i, 0)),
        out_shape=jax.ShapeDtypeStruct((m, d), F32),
        compiler_params=_cparams("parallel", "arbitrary"),
        name="peer_dense",
    )(hn, c1, e2, thr, u_tabs, v_tabs)


SEQ_PASSES = 1


def _dotp(a, b, dims=(((1,), (0,)), ((), ())), passes=None):
    passes = SEQ_PASSES if passes is None else passes
    dot = lambda x, y: lax.dot_general(x, y, dims, preferred_element_type=F32)
    a_hi = a.astype(BF16)
    b_hi = b.astype(BF16)
    if passes == 1:
        return dot(a_hi, b_hi)
    a_lo = (a - a_hi.astype(F32)).astype(BF16)
    b_lo = (b - b_hi.astype(F32)).astype(BF16)
    return dot(a_hi, b_hi) + dot(a_lo, b_hi) + dot(a_hi, b_lo)


_NT = (((1,), (1,)), ((), ()))
_TN = (((0,), (0,)), ((), ()))


def _tri_masks():
    i = lax.broadcasted_iota(jnp.int32, (CHUNK, CHUNK), 0)
    j = lax.broadcasted_iota(jnp.int32, (CHUNK, CHUNK), 1)
    return i >= j, i > j


def _invert_unit_lower_batch(a_ref, at_ref, xt_ref, ti_ref, nmat):
    zpad = jnp.zeros((128 - nmat, 128), F32)
    for t in range(CHUNK):
        slab = a_ref[pl.ds(t, nmat, stride=CHUNK), :]
        at_ref[t * CHUNK:(t + 1) * CHUNK, :] = jnp.concatenate([slab, zpad], axis=0).T[:CHUNK, :]
    sub = lax.broadcasted_iota(jnp.int32, (8, 128), 0)
    for tb in range(CHUNK // 8):
        def row(ti, carry, tb=tb):
            t = tb * 8 + ti
            acc = [jnp.zeros((8, 128), F32) for _ in range(tb)] + [(sub == ti).astype(F32)]
            for sb in range(tb + 1):
                for s in range(8):
                    sg = sb * 8 + s
                    a = at_ref[pl.ds(t * CHUNK + sg, 1), :]
                    for cb in range(sb + 1):
                        acc[cb] = acc[cb] - a * xt_ref[sg * CHUNK + cb * 8:sg * CHUNK + cb * 8 + 8, :]
            for cb in range(tb + 1):
                xt_ref[pl.ds(pl.multiple_of(t * CHUNK + cb * 8, 8), 8), :] = acc[cb]
            return carry
        lax.fori_loop(0, 8, row, 0)
    zrow = jnp.zeros((128 - CHUNK, 128), F32)
    for t in range(CHUNK):
        slab = xt_ref[t * CHUNK:(t + 1) * CHUNK, :]
        ti_ref[pl.ds(t, nmat, stride=CHUNK), :] = jnp.concatenate([slab, zrow], axis=0).T[:nmat, :]


def _softplus(x):
    return jnp.maximum(x, 0.0) + jnp.log(1.0 + jnp.exp(-jnp.abs(x)))


def _sigmoid(x):
    return 1.0 / (1.0 + jnp.exp(-x))


GDN_ROWS = 2 * CHUNK
GDN_NMAT = 2 * C_V_HEADS
GDN_GROUP = 4


def _gdn_kernel(z_ref, ba_ref, bat_ref, convw_ref, alog_ref, dtb_ref, alogt_ref, dtbt_ref, gnorm_ref,
                o_ref, s_ref,
                halo_ref, q_s, k_s, v_s, gcb_ref, bb_ref, gct_ref, a_ref, at_ref, xt_ref, ti_ref, att_ref):
    step = pl.program_id(0)
    nrow = GDN_ROWS

    @pl.when(step == 0)
    def _():
        s_ref[...] = jnp.zeros_like(s_ref)
        halo_ref[...] = jnp.zeros_like(halo_ref)
        xt_ref[...] = jnp.zeros_like(xt_ref)
        a_ref[...] = jnp.zeros_like(a_ref)

    valid = (step * nrow + lax.broadcasted_iota(jnp.int32, (nrow, 1), 0)) >= SEQ0
    valid_t = (step * nrow + lax.broadcasted_iota(jnp.int32, (1, nrow), 1)) >= SEQ0
    beta = jnp.where(valid, _sigmoid(ba_ref[:, :C_V_HEADS]), 0.0)
    g = jnp.where(valid, -jnp.exp(alog_ref[...]) * _softplus(ba_ref[:, C_V_HEADS:] + dtb_ref[...]), 0.0)
    g_t = jnp.where(valid_t, -jnp.exp(alogt_ref[...]) * _softplus(bat_ref[C_V_HEADS:, :] + dtbt_ref[...]), 0.0)
    causal, strict = _tri_masks()
    tri = causal.astype(F32)
    hi = lambda a, b, dims=(((1,), (0,)), ((), ())): lax.dot_general(a, b, dims, precision=HIGHEST,
                                                                      preferred_element_type=F32)
    gc = jnp.concatenate([hi(tri, g[c * CHUNK:(c + 1) * CHUNK]) for c in range(2)], axis=0)
    gct_ref[...] = jnp.concatenate([hi(g_t[:, c * CHUNK:(c + 1) * CHUNK], tri, _NT) for c in range(2)], axis=1)
    for h in range(C_V_HEADS):
        gcb_ref[h] = jnp.broadcast_to(gc[:, h:h + 1], (nrow, 128))
        bb_ref[h] = jnp.broadcast_to(beta[:, h:h + 1], (nrow, 128))

    def conv_act(col0):
        x = z_ref[:, col0:col0 + 128]
        xe = jnp.concatenate([halo_ref[:, col0:col0 + 128], x], axis=0)
        w = convw_ref[:, col0:col0 + 128]
        y = (w[3:4] * x + w[2:3] * pltpu.roll(xe, 1, 0)[8:] + w[1:2] * pltpu.roll(xe, 2, 0)[8:]
             + w[0:1] * pltpu.roll(xe, 3, 0)[8:])
        return y * _sigmoid(y)

    def l2n(x):
        return x * lax.rsqrt(jnp.sum(x * x, axis=-1, keepdims=True) + 1e-6)

    for h in range(C_K_HEADS):
        q_s[:, h * 128:(h + 1) * 128] = l2n(conv_act(h * 128)) * (C_DK ** -0.5)
        k_s[:, h * 128:(h + 1) * 128] = l2n(conv_act(C_KEY + h * 128))
    for h in range(C_V_HEADS):
        v_s[:, h * 128:(h + 1) * 128] = conv_act(2 * C_KEY + h * 128)
    halo_ref[...] = z_ref[nrow - 8:nrow, :C_CONV_CH]

    def build(kh, carry):
        ko = pl.multiple_of(kh * 128, 128)
        for c in range(2):
            rows = slice(c * CHUNK, (c + 1) * CHUNK)
            k = k_s[rows, pl.ds(ko, 128)]
            q = q_s[rows, pl.ds(ko, 128)]
            kk = _dotp(k, k, _NT)
            qk = _dotp(q, k, _NT)
            for r in range(2):
                vh = 2 * kh + r
                diff = gcb_ref[vh][rows, :CHUNK] - gct_ref[pl.ds(vh, 1), c * CHUNK:(c + 1) * CHUNK]
                dec = jnp.where(causal, jnp.exp(jnp.where(causal, diff, 0.0)), 0.0)
                m0 = pl.multiple_of((c * C_V_HEADS + vh) * CHUNK, CHUNK)
                a_ref[pl.ds(m0, CHUNK), :CHUNK] = jnp.where(strict, bb_ref[vh][rows, :CHUNK] * kk * dec, 0.0)
                att_ref[pl.ds(m0, CHUNK), :CHUNK] = jnp.where(causal, qk * dec, 0.0)
        return carry

    lax.fori_loop(0, C_K_HEADS, build, 0)
    _invert_unit_lower_batch(a_ref, at_ref, xt_ref, ti_ref, GDN_NMAT)

    def head_chain(vh, s):
        kh = vh // 2
        ko = pl.multiple_of(kh * 128, 128)
        vo = pl.multiple_of(vh * 128, 128)
        outs = []
        for c in range(2):
            rows = slice(c * CHUNK, (c + 1) * CHUNK)
            k = k_s[rows, pl.ds(ko, 128)]
            q = q_s[rows, pl.ds(ko, 128)]
            v = v_s[rows, pl.ds(vo, 128)]
            gcc = gcb_ref[vh][rows, :]
            bet = bb_ref[vh][rows, :]
            gl = gcc[CHUNK - 1:CHUNK, :]
            eg = jnp.exp(gcc)
            m0 = pl.multiple_of((c * C_V_HEADS + vh) * CHUNK, CHUNK)
            tinv = ti_ref[pl.ds(m0, CHUNK), :CHUNK]
            att = att_ref[pl.ds(m0, CHUNK), :CHUNK]
            kb = k * bet
            uw = _dotp(tinv, jnp.concatenate([v * bet, kb * eg], axis=1))
            ws = _dotp(jnp.concatenate([uw[:, C_DV:], q * eg], axis=0), s)
            v_new = uw[:, :C_DV] - ws[:CHUNK]
            o = ws[CHUNK:] + _dotp(att, v_new)
            s = s * jnp.exp(gl) + _dotp(k * jnp.exp(gl - gcc), v_new, _TN)
            gate = z_ref[rows, pl.ds(pl.multiple_of(C_CONV_CH + vh * 128, 128), 128)]
            on = o * lax.rsqrt(jnp.mean(o * o, axis=-1, keepdims=True) + NORM_EPS) * gnorm_ref[...]
            outs.append((on * (gate * _sigmoid(gate))).astype(o_ref.dtype))
        return s, outs

    def group(gi, carry):
        heads = [gi * GDN_GROUP + r for r in range(GDN_GROUP)]
        res = [head_chain(vh, s_ref[vh]) for vh in heads]
        for vh, (s, outs) in zip(heads, res):
            s_ref[vh] = s
            for c in range(2):
                o_ref[c * CHUNK:(c + 1) * CHUNK, pl.ds(pl.multiple_of(vh * 128, 128), 128)] = outs[c]
        return carry

    lax.fori_loop(0, C_V_HEADS // GDN_GROUP, group, 0)


def gdn_prompt(z, ba, conv_w, a_log, dt_bias, gdn_norm):
    m = z.shape[0]
    nrow = GDN_ROWS
    full = lambda shape: pl.BlockSpec(shape, lambda i: (0,) * len(shape))
    mat = lambda: pltpu.VMEM((GDN_NMAT * CHUNK, 128), F32)
    return pl.pallas_call(
        _gdn_kernel,
        grid=(m // nrow,),
        in_specs=[pl.BlockSpec((nrow, C_CONV_CH + C_VAL), lambda i: (i, 0)),
                  pl.BlockSpec((nrow, 2 * C_V_HEADS), lambda i: (i, 0)),
                  pl.BlockSpec((2 * C_V_HEADS, nrow), lambda i: (0, i)),
                  full((C_CONV, C_CONV_CH)), full((1, C_V_HEADS)), full((1, C_V_HEADS)),
                  full((C_V_HEADS, 1)), full((C_V_HEADS, 1)), full((1, C_DV))],
        out_specs=[pl.BlockSpec((nrow, C_VAL), lambda i: (i, 0)),
                   full((C_V_HEADS, C_DK, C_DV))],
        out_shape=[jax.ShapeDtypeStruct((m, C_VAL), BF16),
                   jax.ShapeDtypeStruct((C_V_HEADS, C_DK, C_DV), F32)],
        scratch_shapes=[pltpu.VMEM((8, C_CONV_CH), F32),
                        pltpu.VMEM((nrow, C_KEY), F32), pltpu.VMEM((nrow, C_KEY), F32), pltpu.VMEM((nrow, C_VAL), F32),
                        pltpu.VMEM((C_V_HEADS, nrow, 128), F32), pltpu.VMEM((C_V_HEADS, nrow, 128), F32),
                        pltpu.VMEM((C_V_HEADS, nrow), F32),
                        mat(), mat(), mat(), mat(), mat()],
        compiler_params=_cparams("arbitrary"),
        name="gdn_prompt",
    )(z, ba, ba.T, conv_w, a_log.reshape(1, -1), dt_bias.reshape(1, -1),
      a_log.reshape(-1, 1), dt_bias.reshape(-1, 1), gdn_norm.reshape(1, -1))


RW_CHUNKS = 5
RW_ROWS = RW_CHUNKS * CHUNK
RW_NMAT = RW_CHUNKS * B_HEADS
RW_GROUP = 2


def _rwkv_kernel(z_ref, mu_ref, w0_ref, wup_ref, a0_ref, aup_ref, gup_ref, kk_ref, ka_ref, rk_ref, lng_ref, lnb_ref,
                 y_ref, s_ref,
                 halo_ref, at_s, bt_s, kt_s, rt_s, v_s, g_s, bon_s, pc_s,
                 a_ref, att_ref, xt_ref, ti_ref, lak_ref, arbk_ref):
    step = pl.program_id(0)
    nrow = RW_ROWS

    @pl.when(step == 0)
    def _():
        s_ref[...] = jnp.zeros_like(s_ref)
        halo_ref[...] = jnp.zeros_like(halo_ref)
        xt_ref[...] = jnp.zeros_like(xt_ref)
        a_ref[...] = jnp.zeros_like(a_ref)

    valid = (step * nrow + lax.broadcasted_iota(jnp.int32, (nrow, 1), 0)) >= SEQ0
    z = z_ref[...]
    zprev = pltpu.roll(jnp.concatenate([halo_ref[...], z], axis=0), 1, 0)[8:]
    halo_ref[...] = z[nrow - 8:nrow]
    zs = z + mu_ref[...] * (zprev - z)
    o1, o2, o3 = B_WIDTH, 2 * B_WIDTH, 3 * B_WIDTH
    o4 = o3 + B_W_RANK
    o5 = o4 + B_A_RANK
    r = zs[:, :o1]
    k = zs[:, o1:o2]
    v = jnp.where(valid, zs[:, o2:o3], 0.0)
    w = -_softplus(-(w0_ref[...] + _dotp(jnp.tanh(zs[:, o3:o4]), wup_ref[...]))) - 0.5
    ld = jnp.where(valid, -jnp.exp(w), 0.0)
    a = _sigmoid(a0_ref[...] + _dotp(zs[:, o4:o5], aup_ref[...]))
    g_s[...] = _dotp(_sigmoid(zs[:, o5:]), gup_ref[...])
    k2 = jnp.where(valid, k * (1.0 + (a - 1.0) * ka_ref[...]), 0.0)
    bon_s[...] = r * k2 * rk_ref[...]
    v_s[...] = v
    kkraw = k * kk_ref[...]
    causal, strict = _tri_masks()
    tri = causal.astype(F32)
    cum = jnp.concatenate(
        [lax.dot_general(tri, ld[c * CHUNK:(c + 1) * CHUNK], (((1,), (0,)), ((), ())), precision=HIGHEST,
                         preferred_element_type=F32) for c in range(RW_CHUNKS)], axis=0)
    for c in range(RW_CHUNKS):
        pc_s[c * 8:(c + 1) * 8, :] = jnp.broadcast_to(jnp.exp(cum[(c + 1) * CHUNK - 1:(c + 1) * CHUNK]), (8, B_WIDTH))
    em = jnp.exp(-cum)
    kt_s[...] = k2 * em
    rt_s[...] = r * jnp.exp(cum)
    ea = jnp.exp(cum - ld)
    for h in range(B_HEADS):
        sl = slice(h * B_DH, (h + 1) * B_DH)
        kh = kkraw[:, sl]
        nrm = jnp.maximum(jnp.sqrt(jnp.sum(kh * kh, axis=-1, keepdims=True)), 1e-12)
        kkn = jnp.where(valid, kh / nrm, 0.0)
        at_s[:, sl] = -kkn * ea[:, sl]
        bt_s[:, sl] = kkn * a[:, sl] * em[:, sl]

    causal2 = (lax.broadcasted_iota(jnp.int32, (CHUNK, 2 * CHUNK), 0)
               >= lax.broadcasted_iota(jnp.int32, (CHUNK, 2 * CHUNK), 1) % CHUNK)

    def operands(po, c, rr):
        rows = slice(c * CHUNK, (c + 1) * CHUNK)
        sl = slice(rr * B_DH, (rr + 1) * B_DH)
        ar = jnp.concatenate([at_s[rows, pl.ds(po, 128)][:, sl], rt_s[rows, pl.ds(po, 128)][:, sl]], axis=0)
        bk = jnp.concatenate([bt_s[rows, pl.ds(po, 128)][:, sl], kt_s[rows, pl.ds(po, 128)][:, sl]], axis=0)
        return ar, bk

    def build(p, carry):
        po = pl.multiple_of(p * 128, 128)
        for c in range(RW_CHUNKS):
            for rr in range(2):
                ar, bk = operands(po, c, rr)
                gm = _dotp(ar, bk, _NT)
                m0 = pl.multiple_of((c * B_HEADS + 2 * p + rr) * CHUNK, CHUNK)
                a_ref[pl.ds(m0, CHUNK), :CHUNK] = jnp.where(strict, -gm[:CHUNK, :CHUNK], 0.0)
                lak_ref[pl.ds(m0, CHUNK), :CHUNK] = jnp.where(strict, gm[:CHUNK, CHUNK:], 0.0)
                arbk_ref[pl.ds(m0, CHUNK), :] = jnp.where(causal2, gm[CHUNK:, :], 0.0)
        return carry

    lax.fori_loop(0, B_HEADS // 2, build, 0)
    _invert_unit_lower_batch(a_ref, att_ref, xt_ref, ti_ref, RW_NMAT)

    def pair_chain(p, states):
        po = pl.multiple_of(p * 128, 128)
        ys = []
        for c in range(RW_CHUNKS):
            rows = slice(c * CHUNK, (c + 1) * CHUNK)
            vp = v_s[rows, pl.ds(po, 128)]
            gp = g_s[rows, pl.ds(po, 128)]
            bp = bon_s[rows, pl.ds(po, 128)]
            pcp = pc_s[c * 8:c * 8 + 1, pl.ds(po, 128)]
            lg = lng_ref[:, pl.ds(po, 128)]
            lb = lnb_ref[:, pl.ds(po, 128)]
            outs = []
            for rr in range(2):
                sl = slice(rr * B_DH, (rr + 1) * B_DH)
                s = states[rr]
                ar, bk = operands(po, c, rr)
                m0 = pl.multiple_of((c * B_HEADS + 2 * p + rr) * CHUNK, CHUNK)
                vh = vp[:, sl]
                ars = _dotp(ar, s, _NT)
                u = _dotp(ti_ref[pl.ds(m0, CHUNK), :CHUNK],
                          ars[:CHUNK] + _dotp(lak_ref[pl.ds(m0, CHUNK), :CHUNK], vh))
                uv = jnp.concatenate([u, vh], axis=0)
                y = ars[CHUNK:] + _dotp(arbk_ref[pl.ds(m0, CHUNK), :], uv)
                pc = pcp[:, sl]
                states[rr] = s * pc + _dotp(uv, bk * pc, _TN)
                mean = jnp.mean(y, axis=-1, keepdims=True)
                var = jnp.mean(jnp.square(y - mean), axis=-1, keepdims=True)
                yn = (y - mean) * lax.rsqrt(var + B_GN_EPS) * lg[:, sl] + lb[:, sl]
                bonus = jnp.sum(bp[:, sl], axis=-1, keepdims=True) * vh
                outs.append((yn + bonus) * gp[:, sl])
            ys.append(jnp.concatenate(outs, axis=1).astype(y_ref.dtype))
        return states, ys

    def group(gi, carry):
        pairs = [gi * RW_GROUP + r for r in range(RW_GROUP)]
        res = [pair_chain(p, [s_ref[2 * p], s_ref[2 * p + 1]]) for p in pairs]
        for p, (states, ys) in zip(pairs, res):
            s_ref[2 * p] = states[0]
            s_ref[2 * p + 1] = states[1]
            for c in range(RW_CHUNKS):
                y_ref[c * CHUNK:(c + 1) * CHUNK, pl.ds(pl.multiple_of(p * 128, 128), 128)] = ys[c]
        return carry

    lax.fori_loop(0, B_HEADS // 2 // RW_GROUP, group, 0)


def rwkv_prompt(zb, mu, w0, w_up, a0, a_up, g_up, k_k, k_a, r_k, ln_g, ln_b):
    m = zb.shape[0]
    nrow = RW_ROWS
    full = lambda shape: pl.BlockSpec(shape, lambda i: (0,) * len(shape))
    vec = lambda x: x.reshape(1, -1)
    wide = lambda: pltpu.VMEM((nrow, B_WIDTH), F32)
    mat = lambda rows: pltpu.VMEM((rows, 128), F32)
    return pl.pallas_call(
        _rwkv_kernel,
        grid=(m // nrow,),
        in_specs=[pl.BlockSpec((nrow, B_PROJ), lambda i: (i, 0)),
                  full((1, B_PROJ)), full((1, B_WIDTH)), full((B_W_RANK, B_WIDTH)), full((1, B_WIDTH)),
                  full((B_A_RANK, B_WIDTH)), full((B_G_RANK, B_WIDTH)), full((1, B_WIDTH)), full((1, B_WIDTH)),
                  full((1, B_WIDTH)), full((1, B_WIDTH)), full((1, B_WIDTH))],
        out_specs=[pl.BlockSpec((nrow, B_WIDTH), lambda i: (i, 0)), full((B_HEADS, B_DH, B_DH))],
        out_shape=[jax.ShapeDtypeStruct((m, B_WIDTH), BF16), jax.ShapeDtypeStruct((B_HEADS, B_DH, B_DH), F32)],
        scratch_shapes=[pltpu.VMEM((8, B_PROJ), F32),
                        wide(), wide(), wide(), wide(), wide(), wide(), wide(),
                        pltpu.VMEM((8 * RW_CHUNKS, B_WIDTH), F32),
                        mat(RW_NMAT * CHUNK), mat(CHUNK * CHUNK), mat(CHUNK * CHUNK), mat(RW_NMAT * CHUNK),
                        mat(RW_NMAT * CHUNK), mat(RW_NMAT * CHUNK)],
        compiler_params=_cparams("arbitrary"),
        name="rwkv_prompt",
    )(zb, vec(mu), vec(w0), w_up, vec(a0), a_up, g_up, vec(k_k), vec(k_a), vec(r_k), vec(ln_g), vec(ln_b))


def lambda_init(layer):
    return 0.8 - 0.6 * math.exp(-0.3 * layer)


def _row_to_col(row):
    n = row.shape[1]
    eye = lax.broadcasted_iota(jnp.int32, (n, n), 0) == lax.broadcasted_iota(jnp.int32, (n, n), 1)
    return jnp.sum(jnp.where(eye, row, 0.0), axis=1, keepdims=True)


def _col_to_row(col):
    n = col.shape[0]
    eye = lax.broadcasted_iota(jnp.int32, (n, n), 0) == lax.broadcasted_iota(jnp.int32, (n, n), 1)
    return jnp.sum(jnp.where(eye, col, 0.0), axis=0, keepdims=True)


def _gdn_step_kernel(z_ref, ba_ref, buf_ref, s_ref, convw_ref, alog_ref, dtb_ref, gnorm_ref,
                     o_ref, so_ref, bufo_ref):
    mixed = z_ref[0][:, :C_CONV_CH]
    buf = buf_ref[0]
    w = convw_ref[...]
    conv = jnp.sum(buf * w[:C_CONV - 1], axis=0, keepdims=True) + mixed * w[C_CONV - 1:]
    act = conv * _sigmoid(conv)
    bufo_ref[0] = jnp.concatenate([buf[1:], mixed], axis=0)
    ba = ba_ref[0]
    beta = _sigmoid(ba[:, :C_V_HEADS])
    eg = jnp.exp(-jnp.exp(alog_ref[...]) * _softplus(ba[:, C_V_HEADS:] + dtb_ref[...]))
    l2n = lambda t: t * lax.rsqrt(jnp.sum(t * t, axis=-1, keepdims=True) + 1e-6)
    outs = []
    for h in range(C_V_HEADS):
        kh = h // (C_V_HEADS // C_K_HEADS)
        q = _row_to_col(l2n(act[:, kh * C_DK:(kh + 1) * C_DK]) * (C_DK ** -0.5))
        k = _row_to_col(l2n(act[:, C_KEY + kh * C_DK:C_KEY + (kh + 1) * C_DK]))
        v = act[:, 2 * C_KEY + h * C_DV:2 * C_KEY + (h + 1) * C_DV]
        s = s_ref[0, h] * eg[:, h:h + 1]
        kv = jnp.sum(s * k, axis=0, keepdims=True)
        s = s + k * ((v - kv) * beta[:, h:h + 1])
        so_ref[0, h] = s
        o = jnp.sum(s * q, axis=0, keepdims=True)
        gate = z_ref[0][:, C_CONV_CH + h * C_DV:C_CONV_CH + (h + 1) * C_DV]
        on = o * lax.rsqrt(jnp.mean(o * o, axis=-1, keepdims=True) + NORM_EPS) * gnorm_ref[...]
        outs.append(on * (gate * _sigmoid(gate)))
    o_ref[0] = jnp.concatenate(outs, axis=1).astype(o_ref.dtype)


def gdn_step(z, ba, buf, s0, conv_w, a_log, dt_bias, gnorm):
    bd = z.shape[0]
    full = lambda shape: pl.BlockSpec(shape, lambda b: (0,) * len(shape))
    per = lambda *shape: pl.BlockSpec((1,) + shape, lambda b: (b,) + (0,) * len(shape))
    return pl.pallas_call(
        _gdn_step_kernel,
        grid=(bd,),
        in_specs=[per(1, C_CONV_CH + C_VAL), per(1, 2 * C_V_HEADS), per(C_CONV - 1, C_CONV_CH),
                  per(C_V_HEADS, C_DK, C_DV), full((C_CONV, C_CONV_CH)), full((1, C_V_HEADS)),
                  full((1, C_V_HEADS)), full((1, C_DV))],
        out_specs=[per(1, C_VAL), per(C_V_HEADS, C_DK, C_DV), per(C_CONV - 1, C_CONV_CH)],
        out_shape=[jax.ShapeDtypeStruct((bd, 1, C_VAL), BF16),
                   jax.ShapeDtypeStruct((bd, C_V_HEADS, C_DK, C_DV), F32),
                   jax.ShapeDtypeStruct((bd, C_CONV - 1, C_CONV_CH), F32)],
        compiler_params=_cparams("parallel"),
        name="gdn_step",
    )(z.reshape(bd, 1, -1), ba.reshape(bd, 1, -1), buf, s0, conv_w, a_log.reshape(1, -1),
      dt_bias.reshape(1, -1), gnorm.reshape(1, -1))


def _rwkv_step_kernel(z_ref, prev_ref, s_ref, mu_ref, w0_ref, wup_ref, a0_ref, aup_ref, gup_ref, kk_ref, ka_ref,
                      rk_ref, lng_ref, lnb_ref, y_ref, so_ref, r_s, k_s, v_s, kk_s, a_s, w_s):
    z = z_ref[...]
    zs = z + mu_ref[...] * (prev_ref[...] - z)
    o1, o2, o3 = B_WIDTH, 2 * B_WIDTH, 3 * B_WIDTH
    o4 = o3 + B_W_RANK
    o5 = o4 + B_A_RANK
    k = zs[:, o1:o2]
    w = -_softplus(-(w0_ref[...] + _dotp(jnp.tanh(zs[:, o3:o4]), wup_ref[...]))) - 0.5
    a = _sigmoid(a0_ref[...] + _dotp(zs[:, o4:o5], aup_ref[...]))
    g = _dotp(_sigmoid(zs[:, o5:]), gup_ref[...])
    r_s[...] = zs[:, :o1]
    k_s[...] = k * (1.0 + (a - 1.0) * ka_ref[...])
    v_s[...] = zs[:, o2:o3]
    kk_s[...] = k * kk_ref[...]
    a_s[...] = a
    w_s[...] = jnp.exp(-jnp.exp(w))

    def seq(b, carry):
        row = lambda ref: ref[pl.ds(b, 1), :]
        r, k2, v, kkraw, ab, dec = row(r_s), row(k_s), row(v_s), row(kk_s), row(a_s), row(w_s)
        ys = []
        for h in range(B_HEADS):
            sl = slice(h * B_DH, (h + 1) * B_DH)
            kkh = kkraw[:, sl]
            kkh = kkh / jnp.maximum(jnp.sqrt(jnp.sum(kkh * kkh, axis=-1, keepdims=True)), 1e-12)
            s = s_ref[b, h]
            sa = jnp.sum(s * (-kkh), axis=1, keepdims=True)
            s = s * dec[:, sl] + sa * (kkh * ab[:, sl]) + _row_to_col(v[:, sl]) * k2[:, sl]
            so_ref[b, h] = s
            y = _col_to_row(jnp.sum(s * r[:, sl], axis=1, keepdims=True))
            mean = jnp.mean(y, axis=-1, keepdims=True)
            var = jnp.mean(jnp.square(y - mean), axis=-1, keepdims=True)
            ys.append((y - mean) * lax.rsqrt(var + B_GN_EPS) * lng_ref[:, sl] + lnb_ref[:, sl]
                      + jnp.sum(r[:, sl] * k2[:, sl] * rk_ref[:, sl], axis=-1, keepdims=True) * v[:, sl])
        y_ref[pl.ds(b, 1), :] = jnp.concatenate(ys, axis=1)
        return carry

    lax.fori_loop(0, z.shape[0], seq, 0)
    y_ref[...] = y_ref[...] * g


def rwkv_step(zb, prev, s0, mu, w0, w_up, a0, a_up, g_up, k_k, k_a, r_k, ln_g, ln_b):
    bd = zb.shape[0]
    vec = lambda x: x.reshape(1, -1)
    wide = lambda: pltpu.VMEM((bd, B_WIDTH), F32)
    return pl.pallas_call(
        _rwkv_step_kernel,
        out_shape=[jax.ShapeDtypeStruct((bd, B_WIDTH), F32),
                   jax.ShapeDtypeStruct((bd, B_HEADS, B_DH, B_DH), F32)],
        scratch_shapes=[wide() for _ in range(6)],
        compiler_params=pltpu.CompilerParams(vmem_limit_bytes=VMEM_LIMIT),
        name="rwkv_step",
    )(zb, prev, s0, vec(mu), vec(w0), w_up, vec(a0), a_up, g_up, vec(k_k), vec(k_a), vec(r_k), vec(ln_g), vec(ln_b))


def _decode_attention_glue(qs, k, v, lamvec, subln, lam_init, cache_k, cache_v, page_table):
    bd = qs.shape[0]
    qf = qs.astype(F32).reshape(bd, 1, A_HEADS, 2, A_DH)
    k5 = k.reshape(bd, 1, A_HEADS, 2, A_DH)
    v4 = v.reshape(bd, 1, A_HEADS, A_VD)

    def update(carry, s, vals):
        m, l, acc = carry
        m_new = jnp.maximum(m, s.max(-1))
        corr = jnp.exp(m - m_new)
        p = jnp.exp(s - m_new[..., None])
        return (m_new, l * corr + p.sum(-1),
                acc * corr[..., None] + jnp.einsum('bhcqk,bkhe->bhcqe', p, vals))

    def page_step(carry, phys):
        kp = cache_k[phys].reshape(bd, PAGE_SIZE, A_HEADS, 2, A_DH)
        s = jnp.einsum('bqhcd,bkhcd->bhcqk', qf, kp)
        return update(carry, s, cache_v[phys]), None

    init = (jnp.full((bd, A_HEADS, 2, 1), NEG_INF, F32), jnp.zeros((bd, A_HEADS, 2, 1), F32),
            jnp.zeros((bd, A_HEADS, 2, 1, A_VD), F32))
    carry, _ = lax.scan(page_step, init, page_table.T)
    m, l, acc = update(carry, jnp.einsum('bqhcd,bkhcd->bhcqk', qf, k5), v4)
    o = acc / l[..., None]
    lam = jnp.exp(jnp.sum(lamvec[0] * lamvec[1])) - jnp.exp(jnp.sum(lamvec[2] * lamvec[3])) + lam_init
    o = (o[:, :, 0] - lam * o[:, :, 1]).reshape(bd, A_HEADS, A_VD)
    o = o * lax.rsqrt(jnp.mean(o * o, axis=-1, keepdims=True) + SUBLN_EPS) * subln * (1.0 - lam_init)
    return o.reshape(bd, A_WIDTH)


def _decode_rwkv_glue(zb, prev, s0, mu, w0, w_up, a0, a_up, g_up, k_k, k_a, r_k, ln_g, ln_b):
    bd = zb.shape[0]
    zs = zb + mu * (prev - zb)
    o1, o2, o3 = B_WIDTH, 2 * B_WIDTH, 3 * B_WIDTH
    o4 = o3 + B_W_RANK
    o5 = o4 + B_A_RANK
    r, k, v = zs[:, :o1], zs[:, o1:o2], zs[:, o2:o3]
    w = -jax.nn.softplus(-(w0 + jnp.tanh(zs[:, o3:o4]) @ w_up)) - 0.5
    decay = jnp.exp(-jnp.exp(w))
    a = jax.nn.sigmoid(a0 + zs[:, o4:o5] @ a_up)
    g = jax.nn.sigmoid(zs[:, o5:]) @ g_up
    heads = lambda t: t.reshape(bd, B_HEADS, B_DH)
    kk = heads(k * k_k)
    kk = kk / jnp.maximum(jnp.sqrt(jnp.sum(kk * kk, axis=-1, keepdims=True)), 1e-12)
    k = k * (1.0 + (a - 1.0) * k_a)
    r, k, v, decay, a = heads(r), heads(k), heads(v), heads(decay), heads(a)
    sa = jnp.einsum('bhij,bhj->bhi', s0, -kk)
    s = s0 * decay[:, :, None, :] + sa[..., None] * (kk * a)[:, :, None, :] + v[..., None] * k[:, :, None, :]
    y = jnp.einsum('bhij,bhj->bhi', s, r)
    mean = jnp.mean(y, axis=-1, keepdims=True)
    var = jnp.mean(jnp.square(y - mean), axis=-1, keepdims=True)
    y = ((y - mean) * lax.rsqrt(var + B_GN_EPS)).reshape(bd, B_WIDTH) * ln_g + ln_b
    bonus = jnp.sum(r * k * r_k, axis=-1, keepdims=True) * v
    return (y + bonus.reshape(bd, B_WIDTH)) * g, s


def _decode_gdn_glue(z, ba, buf, s0, conv_w, a_log, dt_bias, gnorm):
    bd = z.shape[0]
    xp = jnp.concatenate([buf, z[:, None, :C_CONV_CH]], axis=1)
    act = jax.nn.silu(jnp.sum(xp * conv_w[None], axis=1))
    l2n = lambda t: t * lax.rsqrt(jnp.sum(t * t, axis=-1, keepdims=True) + 1e-6)
    rep = C_V_HEADS // C_K_HEADS
    q = jnp.repeat(l2n(act[:, :C_KEY].reshape(bd, C_K_HEADS, C_DK)) * (C_DK ** -0.5), rep, axis=1)
    k = jnp.repeat(l2n(act[:, C_KEY:2 * C_KEY].reshape(bd, C_K_HEADS, C_DK)), rep, axis=1)
    v = act[:, 2 * C_KEY:].reshape(bd, C_V_HEADS, C_DV)
    beta = jax.nn.sigmoid(ba[:, :C_V_HEADS])
    g = -jnp.exp(a_log) * jax.nn.softplus(ba[:, C_V_HEADS:] + dt_bias)
    s = s0 * jnp.exp(g)[..., None, None]
    kv = jnp.einsum('bhkv,bhk->bhv', s, k)
    s = s + jnp.einsum('bhk,bhv->bhkv', k, (v - kv) * beta[..., None])
    o = jnp.einsum('bhkv,bhk->bhv', s, q)
    gate = z[:, C_CONV_CH:].reshape(bd, C_V_HEADS, C_DV)
    o = o * lax.rsqrt(jnp.mean(o * o, axis=-1, keepdims=True) + NORM_EPS) * gnorm * jax.nn.silu(gate)
    return o.reshape(bd, C_VAL), s, xp[:, 1:]


def kernel(x_prompt, x_sample, cache_k, cache_v, page_table, state_wkv, state_shift, state_gdn, state_conv, meta, norm_mix, norm_ffn, norm_final, w_in_even, w_out_even, lam_q1, lam_k1, lam_q2, lam_k2, subln, rw_mu, rw_w0, rw_w_up, rw_a0, rw_a_up, rw_g_up, rw_k_k, rw_k_a, rw_r_k, rw_ln_g, rw_ln_b, w_in_odd, conv_w, a_log, dt_bias, gdn_norm, w_out_odd, peer_wq, peer_keys, peer_u, peer_v):
    x = jnp.concatenate([x_sample.reshape(DEC_BATCH, D_MODEL),
                         jnp.zeros((SEQ0 - DEC_BATCH, D_MODEL), F32),
                         meta.astype(F32), x_prompt.reshape(SEQ, D_MODEL)], axis=0)
    past_len = page_table.shape[1] * PAGE_SIZE
    pos = jnp.concatenate([jnp.full((DEC_BATCH,), past_len, jnp.int32),
                           jnp.zeros((SEQ0 - DEC_BATCH,), jnp.int32),
                           jnp.arange(T_PROMPT, dtype=jnp.int32)])
    cos, sin = _rope_tables(pos)
    nd = DEC_BATCH

    def peer(xin, layer):
        hn = rmsnorm_rows(xin, norm_ffn[layer])
        qp = matmul_cols(hn, peer_wq[layer], 0, P_HEADS * P_DQ, tn=1024)
        c1, e2, thr = peer_select(qp, peer_keys, layer)
        return peer_dense(hn, c1, e2, thr, peer_u, peer_v, layer)

    h = rmsnorm_rows(x, norm_mix[0])
    z_att = matmul_cols(h, w_in_even[0], 0, 3 * A_WIDTH, tn=1024)
    zb = matmul_cols(h, w_in_even[0], 3 * A_WIDTH, B_PROJ, tn=256)
    qb, kf, kb, vb = rope_qkv(z_att, cos, sin)
    lamvec = jnp.stack([lam_q1[0], lam_k1[0], lam_q2[0], lam_k2[0]])
    att = attn_prompt(qb, kb, vb, lamvec, subln[0], lambda_init(0))
    rw, wkv_p = rwkv_prompt(zb, rw_mu[0], rw_w0[0], rw_w_up[0], rw_a0[0], rw_a_up[0], rw_g_up[0],
                            rw_k_k[0], rw_k_a[0], rw_r_k[0], rw_ln_g[0], rw_ln_b[0])
    v_f = z_att[:, 2 * A_WIDTH:]
    lane_map = jnp.arange(A_WIDTH, dtype=jnp.int32)[None, None, :] // A_DH
    qbd = jnp.where(lane_map == jnp.arange(2 * A_HEADS, dtype=jnp.int32)[None, :, None], qb[:nd, None, :], 0)
    att_d = attn_decode(qbd, kf[:nd].reshape(nd, 1, A_WIDTH), v_f[:nd].reshape(nd, 1, A_WIDTH),
                        cache_k, cache_v, page_table, lamvec, subln[0], lambda_init(0)).reshape(nd, A_WIDTH)
    rw_d, wkv_s = rwkv_step(zb[:nd], state_shift[0], state_wkv[0], rw_mu[0], rw_w0[0], rw_w_up[0], rw_a0[0],
                            rw_a_up[0], rw_g_up[0], rw_k_k[0], rw_k_a[0], rw_r_k[0], rw_ln_g[0], rw_ln_b[0])
    mix = jnp.concatenate([att.at[:nd].set(att_d.astype(BF16)), rw.at[:nd].set(rw_d.astype(BF16))], axis=1)
    x = matmul_cols(mix, w_out_even[0], 0, D_MODEL, tn=1024, residual=x)

    x, h = rmsnorm_rows(x, norm_mix[1], add=peer(x, 0), with_sum=True)
    z1 = matmul_cols(h, w_in_odd[0], 0, C_CONV_CH + C_VAL, tn=1024)
    ba = matmul_cols(h, w_in_odd[0][:, C_CONV_CH + C_VAL:], 0, 2 * C_V_HEADS, tn=2 * C_V_HEADS)
    o, gdn_p = gdn_prompt(z1, ba, conv_w[0], a_log[0], dt_bias[0], gdn_norm[0])
    o_d, gdn_s, conv_s = gdn_step(z1[:nd], ba[:nd], state_conv[0], state_gdn[0], conv_w[0], a_log[0],
                                  dt_bias[0], gdn_norm[0])
    x = matmul_cols(o.at[:nd].set(o_d.reshape(nd, C_VAL)), w_out_odd[0], 0, D_MODEL, tn=512, residual=x)
    xf = rmsnorm_rows(x, norm_final, add=peer(x, 1), out_dtype=F32)

    y_prompt = xf[SEQ0 + N_META:].reshape(1, SEQ, D_MODEL)
    y_sample = xf[:nd].reshape(nd, 1, D_MODEL)
    k_p = kf[SEQ0:].reshape(1, 1, T_PROMPT, A_HEADS, 2 * A_DH)
    v_p = v_f[SEQ0:].reshape(1, 1, T_PROMPT, A_HEADS, A_VD)
    k_s = kf[:nd].reshape(1, nd, 1, A_HEADS, 2 * A_DH)
    v_s = v_f[:nd].reshape(1, nd, 1, A_HEADS, A_VD)
    shift_p = zb[R_ROWS - 1:].reshape(1, 1, B_PROJ)
    shift_s = zb[:nd].reshape(1, nd, B_PROJ)
    conv_p = z1[R_ROWS - (C_CONV - 1):, :C_CONV_CH].reshape(1, 1, C_CONV - 1, C_CONV_CH)
    return (y_prompt, y_sample, k_p, v_p, k_s, v_s,
            wkv_p.reshape(1, 1, B_HEADS, B_DH, B_DH), wkv_s.reshape(1, nd, B_HEADS, B_DH, B_DH),
            shift_p, shift_s,
            gdn_p.reshape(1, 1, C_V_HEADS, C_DK, C_DV), gdn_s.reshape(1, nd, C_V_HEADS, C_DK, C_DV),
            conv_p, conv_s.reshape(1, nd, C_CONV - 1, C_CONV_CH))
```

```python
import functools
import math

import jax
import jax.numpy as jnp
from jax import lax
from jax.experimental import pallas as pl
from jax.experimental.pallas import tpu as pltpu

D_MODEL = 2048
SEQ = 8192
DEC_BATCH = 32
N_META = 16
PAGE_SIZE = 128
NORM_EPS = 1e-6
SUBLN_EPS = 1e-5
ROPE_THETA = 10000.0
NEG_INF = -1e30

A_HEADS = 8
A_DH = 64
A_VD = 128
A_WIDTH = 1024

B_HEADS = 16
B_DH = 64
B_WIDTH = 1024
B_W_RANK = 64
B_A_RANK = 64
B_G_RANK = 128
B_PROJ = 3328
B_GN_EPS = 64e-5

C_K_HEADS = 16
C_V_HEADS = 32
C_DK = 128
C_DV = 128
C_KEY = 2048
C_VAL = 4096
C_CONV = 4
C_CONV_CH = 8192

P_HEADS = 8
P_NKEYS = 128
P_TOPK = 16
P_DQ = 256

T_PROMPT = N_META + SEQ
R_ROWS = 8320
SEQ0 = R_ROWS - T_PROMPT
CHUNK = 64

VMEM_LIMIT = 56 * 1024 * 1024

F32 = jnp.float32
BF16 = jnp.bfloat16
HIGHEST = lax.Precision.HIGHEST


def _cparams(*sem):
    return pltpu.CompilerParams(dimension_semantics=sem, vmem_limit_bytes=VMEM_LIMIT)


def _rmsnorm_kernel(*refs, has_add, with_sum):
    x = refs[0][...]
    if has_add:
        x = x + refs[1][...]
    g_ref = refs[1 + has_add]
    outs = refs[2 + has_add:]
    if with_sum:
        outs[0][...] = x
    y = x * lax.rsqrt(jnp.mean(x * x, axis=-1, keepdims=True) + NORM_EPS)
    outs[-1][...] = (y * g_ref[...]).astype(outs[-1].dtype)


def rmsnorm_rows(x, g, add=None, with_sum=False, out_dtype=BF16, tm=640):
    m, d = x.shape
    row = pl.BlockSpec((tm, d), lambda i: (i, 0))
    ins = [x] + ([add] if add is not None else [])
    out_shape = [jax.ShapeDtypeStruct((m, d), out_dtype)]
    if with_sum:
        out_shape.insert(0, jax.ShapeDtypeStruct((m, d), F32))
    res = pl.pallas_call(
        functools.partial(_rmsnorm_kernel, has_add=add is not None, with_sum=with_sum),
        grid=(m // tm,),
        in_specs=[row] * len(ins) + [pl.BlockSpec((1, d), lambda i: (0, 0))],
        out_specs=[row] * len(out_shape),
        out_shape=out_shape,
        compiler_params=_cparams("parallel"),
        name="rmsnorm_rows",
    )(*ins, g.reshape(1, d))
    return res if with_sum else res[0]


def _matmul_kernel(a_ref, w_ref, *rest, has_res):
    if has_res:
        r_ref, o_ref, wb_ref = rest
    else:
        o_ref, wb_ref = rest

    @pl.when(pl.program_id(1) == 0)
    def _():
        wb_ref[...] = w_ref[...].astype(BF16)

    acc = jnp.dot(a_ref[...], wb_ref[...], preferred_element_type=F32)
    if has_res:
        acc = acc + r_ref[...]
    o_ref[...] = acc


def matmul_cols(a, w, col0, n, tn, tm=640, residual=None):
    m, k = a.shape
    assert w.shape[0] == k and n % tn == 0 and col0 % tn == 0 and m % tm == 0
    cb0 = col0 // tn
    in_specs = [pl.BlockSpec((tm, k), lambda j, i: (i, 0)),
                pl.BlockSpec((k, tn), lambda j, i: (0, cb0 + j))]
    args = [a, w]
    if residual is not None:
        in_specs.append(pl.BlockSpec((tm, tn), lambda j, i: (i, j)))
        args.append(residual)
    return pl.pallas_call(
        functools.partial(_matmul_kernel, has_res=residual is not None),
        grid=(n // tn, m // tm),
        in_specs=in_specs,
        out_specs=pl.BlockSpec((tm, tn), lambda j, i: (i, j)),
        out_shape=jax.ShapeDtypeStruct((m, n), F32),
        scratch_shapes=[pltpu.VMEM((k, tn), BF16)],
        compiler_params=_cparams("parallel", "arbitrary"),
        name="matmul_cols",
    )(*args)


def _rope_tables(pos):
    half = A_DH // 2
    inv_freq = ROPE_THETA ** (-jnp.arange(half, dtype=F32) / half)
    ang = pos.astype(F32)[:, None] * inv_freq[None, :]
    cos = jnp.tile(jnp.cos(ang), (1, 4))
    sin = jnp.sin(ang)
    sin = jnp.tile(jnp.concatenate([-sin, sin], axis=1), (1, 2))
    return cos, sin


def _rope_kernel(z_ref, cos_ref, sin_ref, qb_ref, kf_ref, kb_ref, vb_ref):
    cos = cos_ref[...]
    sin = sin_ref[...]
    lane = lax.broadcasted_iota(jnp.int32, cos.shape, 1)
    first_half = (lane % A_DH) < (A_DH // 2)

    def rope(x):
        partner = jnp.where(first_half, pltpu.roll(x, 128 - A_DH // 2, 1), pltpu.roll(x, A_DH // 2, 1))
        return x * cos + partner * sin

    for h in range(A_HEADS):
        sl = slice(h * 128, (h + 1) * 128)
        q = rope(z_ref[:, sl])
        qb_ref[:, sl] = (q * (A_DH ** -0.5)).astype(BF16)
        k = rope(z_ref[:, A_WIDTH + h * 128:A_WIDTH + (h + 1) * 128])
        kf_ref[:, sl] = k
        kb_ref[:, sl] = k.astype(BF16)
    vb_ref[...] = z_ref[:, 2 * A_WIDTH:3 * A_WIDTH].astype(BF16)


def rope_qkv(z_att, cos, sin, tm=640):
    m = z_att.shape[0]
    row = lambda w: pl.BlockSpec((tm, w), lambda i: (i, 0))
    return pl.pallas_call(
        _rope_kernel,
        grid=(m // tm,),
        in_specs=[row(3 * A_WIDTH), row(128), row(128)],
        out_specs=[row(A_WIDTH)] * 4,
        out_shape=[jax.ShapeDtypeStruct((m, A_WIDTH), BF16), jax.ShapeDtypeStruct((m, A_WIDTH), F32),
                   jax.ShapeDtypeStruct((m, A_WIDTH), BF16), jax.ShapeDtypeStruct((m, A_WIDTH), BF16)],
        compiler_params=_cparams("parallel"),
        name="rope_qkv",
    )(z_att, cos, sin)


def _lambda(lam_ref, lam_init):
    l1 = jnp.exp(jnp.sum(lam_ref[0:1, :] * lam_ref[1:2, :], axis=-1, keepdims=True))
    l2 = jnp.exp(jnp.sum(lam_ref[2:3, :] * lam_ref[3:4, :], axis=-1, keepdims=True))
    return l1 - l2 + lam_init


def _subln(o, subln_ref, lam_init):
    y = o * lax.rsqrt(jnp.mean(o * o, axis=-1, keepdims=True) + SUBLN_EPS)
    return y * subln_ref[...] * (1.0 - lam_init)


def _attn_prompt_kernel(q_ref, k_ref, v_ref, lam_ref, subln_ref, o_ref, *, tq, lam_init):
    qb = pl.program_id(1)
    q = q_ref[...]
    qidx = qb * tq + lax.broadcasted_iota(jnp.int32, (tq, tq), 0)
    kiota = lax.broadcasted_iota(jnp.int32, (tq, tq), 1)
    dn = (((1,), (1,)), ((), ()))

    def body(kb, carry, masked):
        off = pl.multiple_of(kb * tq, tq)
        k = k_ref[pl.ds(off, tq), :]
        v = v_ref[pl.ds(off, tq), :]
        if masked:
            kidx = off + kiota
            mask = (kidx <= qidx) & (kidx >= SEQ0)
        out = []
        for c in range(2):
            m, l, acc = carry[c]
            s = lax.dot_general(q[:, c * A_DH:(c + 1) * A_DH], k[:, c * A_DH:(c + 1) * A_DH], dn,
                                preferred_element_type=F32)
            if masked:
                s = jnp.where(mask, s, NEG_INF)
            m_new = jnp.maximum(m, jnp.max(s, axis=-1, keepdims=True))
            corr = jnp.exp(m - m_new)
            p = jnp.exp(s - m_new)
            l = l * corr + jnp.sum(p, axis=-1, keepdims=True)
            acc = acc * corr + jnp.dot(p.astype(BF16), v, preferred_element_type=F32)
            out.append((m_new, l, acc))
        return tuple(out)

    init = tuple((jnp.full((tq, 1), NEG_INF, F32), jnp.zeros((tq, 1), F32), jnp.zeros((tq, A_VD), F32))
                 for _ in range(2))
    carry = body(0, init, True)
    carry = lax.fori_loop(1, qb, functools.partial(body, masked=False), carry)
    (_, l1, a1), (_, l2, a2) = lax.cond(qb > 0, lambda c: body(qb, c, True), lambda c: c, carry)
    o = a1 / l1 - _lambda(lam_ref, lam_init) * (a2 / l2)
    o_ref[...] = _subln(o, subln_ref, lam_init).astype(o_ref.dtype)


def attn_prompt(qb, kb, vb, lamvec, subln, lam_init, tq=640):
    m = qb.shape[0]
    return pl.pallas_call(
        functools.partial(_attn_prompt_kernel, tq=tq, lam_init=lam_init),
        grid=(A_HEADS, m // tq),
        in_specs=[pl.BlockSpec((tq, 128), lambda h, i: (i, h)),
                  pl.BlockSpec((m, 128), lambda h, i: (0, h)),
                  pl.BlockSpec((m, 128), lambda h, i: (0, h)),
                  pl.BlockSpec((4, A_DH), lambda h, i: (0, 0)),
                  pl.BlockSpec((1, A_VD), lambda h, i: (0, 0))],
        out_specs=pl.BlockSpec((tq, 128), lambda h, i: (i, h)),
        out_shape=jax.ShapeDtypeStruct((m, A_WIDTH), BF16),
        compiler_params=_cparams("parallel", "arbitrary"),
        name="attn_prompt",
    )(qb, kb, vb, lamvec, subln.reshape(1, A_VD))


PAGES_PER_STEP = 4


def _attn_decode_kernel(pt_ref, q_ref, kn_ref, vn_ref, *rest, lam_init):
    pages = rest[:2 * PAGES_PER_STEP]
    lam_ref, subln_ref, o_ref, m_s, l_s, acc_s = rest[2 * PAGES_PER_STEP:]
    p = pl.program_id(1)

    @pl.when(p == 0)
    def _():
        m_s[...] = jnp.full_like(m_s, NEG_INF)
        l_s[...] = jnp.zeros_like(l_s)
        acc_s[...] = jnp.zeros_like(acc_s)

    nk = PAGE_SIZE * A_HEADS

    def page(ref):
        return ref[0].reshape(nk, A_VD).astype(BF16)

    own_head = (lax.broadcasted_iota(jnp.int32, (2 * A_HEADS, nk), 0) // 2
                == lax.broadcasted_iota(jnp.int32, (2 * A_HEADS, nk), 1) % A_HEADS)
    q = q_ref[0]
    m, l, acc = m_s[...], l_s[...], acc_s[...]
    ss = [jnp.where(own_head, lax.dot_general(q, page(pages[2 * i]), _NT_DIMS, preferred_element_type=F32), NEG_INF)
          for i in range(PAGES_PER_STEP)]
    m_new = m
    for s in ss:
        m_new = jnp.maximum(m_new, jnp.max(s, axis=-1, keepdims=True))
    corr = jnp.exp(m - m_new)
    l = l * corr
    acc = acc * corr
    for i, s in enumerate(ss):
        pr = jnp.exp(s - m_new)
        l = l + jnp.sum(pr, axis=-1, keepdims=True)
        acc = acc + jnp.dot(pr.astype(BF16), page(pages[2 * i + 1]), preferred_element_type=F32)
    m = m_new
    m_s[...], l_s[...], acc_s[...] = m, l, acc

    @pl.when(p == pl.num_programs(1) - 1)
    def _():
        kn = kn_ref[0].astype(BF16).astype(F32)
        vn = vn_ref[0].astype(BF16).astype(F32)
        s = jnp.sum(q.astype(F32) * kn, axis=-1, keepdims=True)
        m_new = jnp.maximum(m, s)
        corr = jnp.exp(m - m_new)
        pr = jnp.exp(s - m_new)
        lf = l * corr + pr
        o = (acc * corr + pr.astype(BF16).astype(F32) * vn) / lf
        lam = _lambda(lam_ref, lam_init)
        outs = [_subln(o[2 * h:2 * h + 1] - lam * o[2 * h + 1:2 * h + 2], subln_ref, lam_init)
                for h in range(A_HEADS)]
        o_ref[0] = jnp.concatenate(outs, axis=1).astype(o_ref.dtype)


def attn_decode(qm, k_new, v_new, cache_k, cache_v, page_table, lamvec, subln, lam_init):
    bd = qm.shape[0]
    n_pages = page_table.shape[1]
    assert n_pages % PAGES_PER_STEP == 0
    page_specs = []
    for i in range(PAGES_PER_STEP):
        idx = lambda b, p, pt, i=i: (0, pt[b, p * PAGES_PER_STEP + i], 0, 0, 0)
        spec = pl.BlockSpec((None, 1, PAGE_SIZE, A_HEADS, 128), idx)
        page_specs += [spec, spec]
    per_seq = lambda r, w: pl.BlockSpec((1, r, w), lambda b, p, pt: (b, 0, 0))
    grid_spec = pltpu.PrefetchScalarGridSpec(
        num_scalar_prefetch=1,
        grid=(bd, n_pages // PAGES_PER_STEP),
        in_specs=[per_seq(2 * A_HEADS, A_VD)] * 3 + page_specs
                 + [pl.BlockSpec((4, A_DH), lambda b, p, pt: (0, 0)), pl.BlockSpec((1, A_VD), lambda b, p, pt: (0, 0))],
        out_specs=per_seq(1, A_WIDTH),
        scratch_shapes=[pltpu.VMEM((2 * A_HEADS, 1), F32), pltpu.VMEM((2 * A_HEADS, 1), F32),
                        pltpu.VMEM((2 * A_HEADS, A_VD), F32)],
    )
    args = [qm, k_new, v_new] + [cache_k, cache_v] * PAGES_PER_STEP + [lamvec, subln.reshape(1, A_VD)]
    return pl.pallas_call(
        functools.partial(_attn_decode_kernel, lam_init=lam_init),
        grid_spec=grid_spec,
        out_shape=jax.ShapeDtypeStruct((bd, 1, A_WIDTH), BF16),
        compiler_params=_cparams("parallel", "arbitrary"),
        name="attn_decode",
    )(page_table, *args)


def _top_values(s, k):
    n = s.shape[0]
    row = lax.broadcasted_iota(jnp.int32, s.shape, 0).astype(F32)
    vals = []
    for _ in range(k):
        m = jnp.max(s, axis=0, keepdims=True)
        first = jnp.min(jnp.where(s == m, row, float(n)), axis=0, keepdims=True)
        s = jnp.where(row == first, -jnp.inf, s)
        vals.append(m)
    return jnp.concatenate(vals, axis=0)


def _peer_select_kernel(q_ref, keys_ref, c1_ref, e2_ref, thr_ref):
    dn = (((1,), (1,)), ((), ()))

    def head(h, carry):
        qo = pl.multiple_of(h * P_DQ, P_DQ)
        s1 = lax.dot_general(keys_ref[0, h], q_ref[:, pl.ds(qo, 128)], dn,
                             precision=HIGHEST, preferred_element_type=F32)
        s2 = lax.dot_general(keys_ref[1, h], q_ref[:, pl.ds(qo + 128, 128)], dn,
                             precision=HIGHEST, preferred_element_type=F32)
        t1 = _top_values(s1, P_TOPK)
        t2 = _top_values(s2, P_TOPK)
        e1 = jnp.exp(s1 - t1[0:1])
        e2 = jnp.exp(s2 - t2[0:1])
        et1 = jnp.exp(t1 - t1[0:1])
        et2 = jnp.exp(t2 - t2[0:1])
        nb = [P_TOPK // (a + 1) for a in range(P_TOPK)]
        cand = jnp.concatenate([et1[a:a + 1] * et2[:nb[a]] for a in range(P_TOPK)], axis=0)
        top = _top_values(cand, P_TOPK)
        inv_z = 1.0 / jnp.sum(top, axis=0, keepdims=True)
        scaled = jnp.concatenate([(et1[a:a + 1] * inv_z) * et2[:nb[a]] for a in range(P_TOPK)], axis=0)
        thr = jnp.min(jnp.where(cand >= top[P_TOPK - 1:P_TOPK], scaled, jnp.inf), axis=0, keepdims=True)
        ro = pl.multiple_of(h * 128, 128)
        c1_ref[pl.ds(ro, 128), :] = e1 * inv_z
        e2_ref[pl.ds(ro, 128), :] = e2
        thr_ref[pl.ds(h, 1), :] = thr
        return carry

    lax.fori_loop(0, P_HEADS, head, 0)


def peer_select(qp, keys, layer, tm=640):
    m = qp.shape[0]
    col = lambda r: pl.BlockSpec((r, tm), lambda i: (0, i))
    return pl.pallas_call(
        _peer_select_kernel,
        grid=(m // tm,),
        in_specs=[pl.BlockSpec((tm, P_HEADS * P_DQ), lambda i: (i, 0)),
                  pl.BlockSpec((None, 2, P_HEADS, P_NKEYS, 128), lambda i: (layer, 0, 0, 0, 0))],
        out_specs=[col(P_HEADS * 128), col(P_HEADS * 128), col(P_HEADS)],
        out_shape=[jax.ShapeDtypeStruct((P_HEADS * 128, m), F32), jax.ShapeDtypeStruct((P_HEADS * 128, m), F32),
                   jax.ShapeDtypeStruct((P_HEADS, m), F32)],
        compiler_params=_cparams("parallel"),
        name="peer_select",
    )(qp, keys)


def _peer_dense_kernel(h_ref, c1_ref, e2_ref, thr_ref, u_ref, v_ref, o_ref, *, te):
    j = pl.program_id(1)

    @pl.when(j == 0)
    def _():
        o_ref[...] = jnp.zeros_like(o_ref)

    sub = 256
    total = None
    for cb in range(te // sub):
        rows = slice(cb * sub, (cb + 1) * sub)
        g = lax.dot_general(u_ref[rows, :].astype(BF16), h_ref[...], _NT_DIMS, preferred_element_type=F32)
        act = 0.5 * g * (1.0 + lax.erf(g * (2.0 ** -0.5)))
        parts = []
        for ib in range(sub // 128):
            i1 = j * (te // 128) + cb * (sub // 128) + ib
            w = jnp.zeros((128, g.shape[1]), F32)
            for h in range(P_HEADS):
                p = e2_ref[h * 128:(h + 1) * 128, :] * c1_ref[pl.ds(h * 128 + i1, 1), :]
                w = w + jnp.where(p >= thr_ref[h:h + 1, :], p, 0.0)
            parts.append(act[ib * 128:(ib + 1) * 128, :] * w)
        a = jnp.concatenate(parts, axis=0).T.astype(BF16)
        y = jnp.dot(a, v_ref[rows, :].astype(BF16), preferred_element_type=F32)
        total = y if total is None else total + y
    o_ref[...] += total


_NT_DIMS = (((1,), (1,)), ((), ()))


def peer_dense(hn, c1, e2, thr, u_tabs, v_tabs, layer, tm=640, te=512):
    m, d = hn.shape
    ne = u_tabs.shape[1]
    col = lambda r: pl.BlockSpec((r, tm), lambda i, j: (0, i))
    tab = pl.BlockSpec((None, te, d), lambda i, j: (layer, j, 0))
    return pl.pallas_call(
        functools.partial(_peer_dense_kernel, te=te),
        grid=(m // tm, ne // te),
        in_specs=[pl.BlockSpec((tm, d), lambda i, j: (i, 0)),
                  col(P_HEADS * 128), col(P_HEADS * 128), col(P_HEADS), tab, tab],
        out_specs=pl.BlockSpec((tm, d), lambda i, j: (i, 0)),
        out_shape=jax.ShapeDtypeStruct((m, d), F32),
        compiler_params=_cparams("parallel", "arbitrary"),
        name="peer_dense",
    )(hn, c1, e2, thr, u_tabs, v_tabs)


SEQ_PASSES = 1


def _dotp(a, b, dims=(((1,), (0,)), ((), ())), passes=None):
    passes = SEQ_PASSES if passes is None else passes
    dot = lambda x, y: lax.dot_general(x, y, dims, preferred_element_type=F32)
    a_hi = a.astype(BF16)
    b_hi = b.astype(BF16)
    if passes == 1:
        return dot(a_hi, b_hi)
    a_lo = (a - a_hi.astype(F32)).astype(BF16)
    b_lo = (b - b_hi.astype(F32)).astype(BF16)
    return dot(a_hi, b_hi) + dot(a_lo, b_hi) + dot(a_hi, b_lo)


_NT = (((1,), (1,)), ((), ()))
_TN = (((0,), (0,)), ((), ()))


def _tri_masks():
    i = lax.broadcasted_iota(jnp.int32, (CHUNK, CHUNK), 0)
    j = lax.broadcasted_iota(jnp.int32, (CHUNK, CHUNK), 1)
    return i >= j, i > j


def _invert_unit_lower_batch(a_ref, at_ref, xt_ref, ti_ref, nmat, pair_blockdiag=False):
    zpad = jnp.zeros((128 - nmat, 128), F32)
    for t in range(CHUNK):
        slab = a_ref[pl.ds(t, nmat, stride=CHUNK), :]
        at_ref[t * CHUNK:(t + 1) * CHUNK, :] = jnp.concatenate([slab, zpad], axis=0).T[:CHUNK, :]
    sub = lax.broadcasted_iota(jnp.int32, (8, 128), 0)
    for tb in range(CHUNK // 8):
        def row(ti, carry, tb=tb):
            t = tb * 8 + ti
            acc = [jnp.zeros((8, 128), F32) for _ in range(tb)] + [(sub == ti).astype(F32)]
            for sb in range(tb + 1):
                for s in range(8):
                    sg = sb * 8 + s
                    a = at_ref[pl.ds(t * CHUNK + sg, 1), :]
                    for cb in range(sb + 1):
                        acc[cb] = acc[cb] - a * xt_ref[sg * CHUNK + cb * 8:sg * CHUNK + cb * 8 + 8, :]
            for cb in range(tb + 1):
                xt_ref[pl.ds(pl.multiple_of(t * CHUNK + cb * 8, 8), 8), :] = acc[cb]
            return carry
        lax.fori_loop(0, 8, row, 0)
    zrow = jnp.zeros((128 - CHUNK, 128), F32)
    odd = lax.broadcasted_iota(jnp.int32, (nmat, 128), 0) % 2 == 1
    for t in range(CHUNK):
        slab = xt_ref[t * CHUNK:(t + 1) * CHUNK, :]
        out = jnp.concatenate([slab, zrow], axis=0).T[:nmat, :]
        if pair_blockdiag:
            out = jnp.where(odd, pltpu.roll(out, CHUNK, 1), out)
        ti_ref[pl.ds(t, nmat, stride=CHUNK), :] = out


def _softplus(x):
    return jnp.maximum(x, 0.0) + jnp.log(1.0 + jnp.exp(-jnp.abs(x)))


def _sigmoid(x):
    return 1.0 / (1.0 + jnp.exp(-x))


GDN_ROWS = 2 * CHUNK
GDN_NMAT = 2 * C_V_HEADS
GDN_KH = 2


def _gdn_kernel(z_ref, ba_ref, bat_ref, convw_ref, alog_ref, dtb_ref, alogt_ref, dtbt_ref, gnorm_ref,
                o_ref, s_ref,
                halo_ref, q_s, k_s, kt_s, v_s, gcb_ref, bb_ref, gct_ref, a_ref, at_ref, xt_ref, ti_ref, att_ref):
    step = pl.program_id(0)
    nrow = GDN_ROWS

    @pl.when(step == 0)
    def _():
        s_ref[...] = jnp.zeros_like(s_ref)
        halo_ref[...] = jnp.zeros_like(halo_ref)
        xt_ref[...] = jnp.zeros_like(xt_ref)
        a_ref[...] = jnp.zeros_like(a_ref)
        att_ref[...] = jnp.zeros_like(att_ref)

    valid = (step * nrow + lax.broadcasted_iota(jnp.int32, (nrow, 1), 0)) >= SEQ0
    valid_t = (step * nrow + lax.broadcasted_iota(jnp.int32, (1, nrow), 1)) >= SEQ0
    beta = jnp.where(valid, _sigmoid(ba_ref[:, :C_V_HEADS]), 0.0)
    g = jnp.where(valid, -jnp.exp(alog_ref[...]) * _softplus(ba_ref[:, C_V_HEADS:] + dtb_ref[...]), 0.0)
    g_t = jnp.where(valid_t, -jnp.exp(alogt_ref[...]) * _softplus(bat_ref[C_V_HEADS:, :] + dtbt_ref[...]), 0.0)
    causal, strict = _tri_masks()
    tri = causal.astype(F32)
    hi = lambda a, b, dims=(((1,), (0,)), ((), ())): lax.dot_general(a, b, dims, precision=HIGHEST,
                                                                      preferred_element_type=F32)
    gc = jnp.concatenate([hi(tri, g[c * CHUNK:(c + 1) * CHUNK]) for c in range(2)], axis=0)
    gct_ref[...] = jnp.concatenate([hi(g_t[:, c * CHUNK:(c + 1) * CHUNK], tri, _NT) for c in range(2)], axis=1)
    for h in range(C_V_HEADS):
        gcb_ref[h] = jnp.broadcast_to(gc[:, h:h + 1], (nrow, 128))
        bb_ref[h] = jnp.broadcast_to(beta[:, h:h + 1], (nrow, 128))

    def conv_act(col0):
        x = z_ref[:, col0:col0 + 128]
        xe = jnp.concatenate([halo_ref[:, col0:col0 + 128], x], axis=0)
        w = convw_ref[:, col0:col0 + 128]
        y = (w[3:4] * x + w[2:3] * pltpu.roll(xe, 1, 0)[8:] + w[1:2] * pltpu.roll(xe, 2, 0)[8:]
             + w[0:1] * pltpu.roll(xe, 3, 0)[8:])
        return y * _sigmoid(y)

    def l2n(x):
        return x * lax.rsqrt(jnp.sum(x * x, axis=-1, keepdims=True) + 1e-6)

    for h in range(C_K_HEADS):
        q_s[:, h * 128:(h + 1) * 128] = l2n(conv_act(h * 128)) * (C_DK ** -0.5)
        kh_tile = l2n(conv_act(C_KEY + h * 128))
        k_s[:, h * 128:(h + 1) * 128] = kh_tile
        kt_s[h * 128:(h + 1) * 128, :] = kh_tile.T
    for h in range(C_V_HEADS):
        v_s[:, h * 128:(h + 1) * 128] = conv_act(2 * C_KEY + h * 128)
    halo_ref[...] = z_ref[nrow - 8:nrow, :C_CONV_CH]

    def build(kh, carry):
        ko = pl.multiple_of(kh * 128, 128)
        kt = kt_s[pl.ds(ko, 128), :]
        for c in range(2):
            rows = slice(c * CHUNK, (c + 1) * CHUNK)
            kq = jnp.concatenate([k_s[rows, pl.ds(ko, 128)], q_s[rows, pl.ds(ko, 128)]], axis=0)
            g2 = _dotp(kq, kt[:, c * CHUNK:(c + 1) * CHUNK])
            p0 = pl.multiple_of((c * C_K_HEADS + kh) * 128, 128)
            for r in range(2):
                vh = 2 * kh + r
                diff = gcb_ref[vh][rows, :CHUNK] - gct_ref[pl.ds(vh, 1), c * CHUNK:(c + 1) * CHUNK]
                dec = jnp.where(causal, jnp.exp(jnp.where(causal, diff, 0.0)), 0.0)
                a_ref[pl.ds(p0 + r * CHUNK, CHUNK), :CHUNK] = jnp.where(
                    strict, bb_ref[vh][rows, :CHUNK] * g2[:CHUNK] * dec, 0.0)
                att_ref[pl.ds(p0 + r * CHUNK, CHUNK), r * CHUNK:(r + 1) * CHUNK] = jnp.where(
                    causal, g2[CHUNK:] * dec, 0.0)
        return carry

    lax.fori_loop(0, C_K_HEADS, build, 0)
    _invert_unit_lower_batch(a_ref, at_ref, xt_ref, ti_ref, GDN_NMAT, pair_blockdiag=True)

    def bdiag(blocks):
        r, w = blocks[0].shape
        return jnp.concatenate(
            [jnp.concatenate([b if j == i else jnp.zeros((r, w), F32) for j in range(len(blocks))], axis=1)
             for i, b in enumerate(blocks)], axis=0)

    def pair(gi, carry):
        khs = [gi * GDN_KH + x for x in range(GDN_KH)]
        kos = [pl.multiple_of(kh * 128, 128) for kh in khs]
        kts = [kt_s[pl.ds(ko, 128), :] for ko in kos]
        s4 = jnp.concatenate([jnp.concatenate([s_ref[2 * kh], s_ref[2 * kh + 1]], axis=1) for kh in khs], axis=0)
        outs = []
        for c in range(2):
            rows = slice(c * CHUNK, (c + 1) * CHUNK)
            kq = bdiag([jnp.concatenate([k_s[rows, pl.ds(ko, 128)], q_s[rows, pl.ds(ko, 128)]], axis=0) for ko in kos])
            kqs = _dotp(kq, s4)
            rhs, eg, gl, gcc = [], [], [], []
            for x, kh in enumerate(khs):
                v2 = v_s[rows, pl.ds(pl.multiple_of(kh * 256, 256), 256)]
                for r in range(2):
                    g = gcb_ref[2 * kh + r][rows, :]
                    e = jnp.exp(g)
                    ks = kqs[x * 128:x * 128 + CHUNK, r * C_DV:(r + 1) * C_DV]
                    rhs.append(bb_ref[2 * kh + r][rows, :] * (v2[:, r * C_DV:(r + 1) * C_DV] - e * ks))
                    gcc.append(g)
                    eg.append(e)
                    gl.append(g[CHUNK - 1:CHUNK, :])
            p0s = [pl.multiple_of((c * C_K_HEADS + kh) * 128, 128) for kh in khs]
            vn = _dotp(bdiag([ti_ref[pl.ds(p0, 128), :] for p0 in p0s]), jnp.concatenate(rhs, axis=0))
            av = _dotp(bdiag([att_ref[pl.ds(p0, 128), :] for p0 in p0s]), vn)
            oc, dvn, glast = [], [], []
            for x, kh in enumerate(khs):
                o2 = []
                for r in range(2):
                    i = 2 * x + r
                    o = eg[i] * kqs[x * 128 + CHUNK:(x + 1) * 128, r * C_DV:(r + 1) * C_DV] + av[i * CHUNK:(i + 1) * CHUNK]
                    o2.append(o * lax.rsqrt(jnp.mean(o * o, axis=-1, keepdims=True) + NORM_EPS) * gnorm_ref[...])
                gate2 = z_ref[rows, pl.ds(pl.multiple_of(C_CONV_CH + kh * 256, 256), 256)]
                oc.append((jnp.concatenate(o2, axis=1) * (gate2 * _sigmoid(gate2))).astype(o_ref.dtype))
                dvn.append(jnp.concatenate([jnp.exp(gl[2 * x + r] - gcc[2 * x + r]) * vn[(2 * x + r) * CHUNK:(2 * x + r + 1) * CHUNK]
                                            for r in range(2)], axis=1))
                glast.append(jnp.broadcast_to(jnp.concatenate([jnp.exp(gl[2 * x]), jnp.exp(gl[2 * x + 1])], axis=1),
                                              (C_DK, 2 * C_DV)))
            outs.append(oc)
            s4 = s4 * jnp.concatenate(glast, axis=0) + _dotp(
                bdiag([kt[:, c * CHUNK:(c + 1) * CHUNK] for kt in kts]), jnp.concatenate(dvn, axis=0))
        for x, kh in enumerate(khs):
            s_ref[2 * kh] = s4[x * C_DK:(x + 1) * C_DK, :C_DV]
            s_ref[2 * kh + 1] = s4[x * C_DK:(x + 1) * C_DK, C_DV:]
            for c in range(2):
                o_ref[c * CHUNK:(c + 1) * CHUNK, pl.ds(pl.multiple_of(kh * 256, 256), 256)] = outs[c][x]
        return carry

    lax.fori_loop(0, C_K_HEADS // GDN_KH, pair, 0)


def gdn_prompt(z, ba, conv_w, a_log, dt_bias, gdn_norm):
    m = z.shape[0]
    nrow = GDN_ROWS
    full = lambda shape: pl.BlockSpec(shape, lambda i: (0,) * len(shape))
    mat = lambda: pltpu.VMEM((GDN_NMAT * CHUNK, 128), F32)
    return pl.pallas_call(
        _gdn_kernel,
        grid=(m // nrow,),
        in_specs=[pl.BlockSpec((nrow, C_CONV_CH + C_VAL), lambda i: (i, 0)),
                  pl.BlockSpec((nrow, 2 * C_V_HEADS), lambda i: (i, 0)),
                  pl.BlockSpec((2 * C_V_HEADS, nrow), lambda i: (0, i)),
                  full((C_CONV, C_CONV_CH)), full((1, C_V_HEADS)), full((1, C_V_HEADS)),
                  full((C_V_HEADS, 1)), full((C_V_HEADS, 1)), full((1, C_DV))],
        out_specs=[pl.BlockSpec((nrow, C_VAL), lambda i: (i, 0)),
                   full((C_V_HEADS, C_DK, C_DV))],
        out_shape=[jax.ShapeDtypeStruct((m, C_VAL), BF16),
                   jax.ShapeDtypeStruct((C_V_HEADS, C_DK, C_DV), F32)],
        scratch_shapes=[pltpu.VMEM((8, C_CONV_CH), F32),
                        pltpu.VMEM((nrow, C_KEY), F32), pltpu.VMEM((nrow, C_KEY), F32), pltpu.VMEM((C_KEY, nrow), F32),
                        pltpu.VMEM((nrow, C_VAL), F32),
                        pltpu.VMEM((C_V_HEADS, nrow, 128), F32), pltpu.VMEM((C_V_HEADS, nrow, 128), F32),
                        pltpu.VMEM((C_V_HEADS, nrow), F32),
                        mat(), mat(), mat(), mat(), mat()],
        compiler_params=_cparams("arbitrary"),
        name="gdn_prompt",
    )(z, ba, ba.T, conv_w, a_log.reshape(1, -1), dt_bias.reshape(1, -1),
      a_log.reshape(-1, 1), dt_bias.reshape(-1, 1), gdn_norm.reshape(1, -1))


RW_CHUNKS = 5
RW_ROWS = RW_CHUNKS * CHUNK
RW_NMAT = RW_CHUNKS * B_HEADS
RW_GROUP = 2


def _rwkv_kernel(z_ref, mu_ref, w0_ref, wup_ref, a0_ref, aup_ref, gup_ref, kk_ref, ka_ref, rk_ref, lng_ref, lnb_ref,
                 y_ref, s_ref,
                 halo_ref, at_s, bt_s, kt_s, rt_s, v_s, g_s, bon_s, pc_s,
                 a_ref, att_ref, xt_ref, ti_ref, lak_ref, arbk_ref):
    step = pl.program_id(0)
    nrow = RW_ROWS

    @pl.when(step == 0)
    def _():
        s_ref[...] = jnp.zeros_like(s_ref)
        halo_ref[...] = jnp.zeros_like(halo_ref)
        xt_ref[...] = jnp.zeros_like(xt_ref)
        a_ref[...] = jnp.zeros_like(a_ref)

    valid = (step * nrow + lax.broadcasted_iota(jnp.int32, (nrow, 1), 0)) >= SEQ0
    z = z_ref[...]
    zprev = pltpu.roll(jnp.concatenate([halo_ref[...], z], axis=0), 1, 0)[8:]
    halo_ref[...] = z[nrow - 8:nrow]
    zs = z + mu_ref[...] * (zprev - z)
    o1, o2, o3 = B_WIDTH, 2 * B_WIDTH, 3 * B_WIDTH
    o4 = o3 + B_W_RANK
    o5 = o4 + B_A_RANK
    r = zs[:, :o1]
    k = zs[:, o1:o2]
    v = jnp.where(valid, zs[:, o2:o3], 0.0)
    w = -_softplus(-(w0_ref[...] + _dotp(jnp.tanh(zs[:, o3:o4]), wup_ref[...]))) - 0.5
    ld = jnp.where(valid, -jnp.exp(w), 0.0)
    a = _sigmoid(a0_ref[...] + _dotp(zs[:, o4:o5], aup_ref[...]))
    g_s[...] = _dotp(_sigmoid(zs[:, o5:]), gup_ref[...])
    k2 = jnp.where(valid, k * (1.0 + (a - 1.0) * ka_ref[...]), 0.0)
    bon_s[...] = r * k2 * rk_ref[...]
    v_s[...] = v
    kkraw = k * kk_ref[...]
    causal, strict = _tri_masks()
    tri = causal.astype(F32)
    cum = jnp.concatenate(
        [lax.dot_general(tri, ld[c * CHUNK:(c + 1) * CHUNK], (((1,), (0,)), ((), ())), precision=HIGHEST,
                         preferred_element_type=F32) for c in range(RW_CHUNKS)], axis=0)
    for c in range(RW_CHUNKS):
        pc_s[c * 8:(c + 1) * 8, :] = jnp.broadcast_to(jnp.exp(cum[(c + 1) * CHUNK - 1:(c + 1) * CHUNK]), (8, B_WIDTH))
    em = jnp.exp(-cum)
    kt_s[...] = k2 * em
    rt_s[...] = r * jnp.exp(cum)
    ea = jnp.exp(cum - ld)
    for h in range(B_HEADS):
        sl = slice(h * B_DH, (h + 1) * B_DH)
        kh = kkraw[:, sl]
        nrm = jnp.maximum(jnp.sqrt(jnp.sum(kh * kh, axis=-1, keepdims=True)), 1e-12)
        kkn = jnp.where(valid, kh / nrm, 0.0)
        at_s[:, sl] = -kkn * ea[:, sl]
        bt_s[:, sl] = kkn * a[:, sl] * em[:, sl]

    causal2 = (lax.broadcasted_iota(jnp.int32, (CHUNK, 2 * CHUNK), 0)
               >= lax.broadcasted_iota(jnp.int32, (CHUNK, 2 * CHUNK), 1) % CHUNK)

    def operands(po, c, rr):
        rows = slice(c * CHUNK, (c + 1) * CHUNK)
        sl = slice(rr * B_DH, (rr + 1) * B_DH)
        ar = jnp.concatenate([at_s[rows, pl.ds(po, 128)][:, sl], rt_s[rows, pl.ds(po, 128)][:, sl]], axis=0)
        bk = jnp.concatenate([bt_s[rows, pl.ds(po, 128)][:, sl], kt_s[rows, pl.ds(po, 128)][:, sl]], axis=0)
        return ar, bk

    def build(p, carry):
        po = pl.multiple_of(p * 128, 128)
        for c in range(RW_CHUNKS):
            for rr in range(2):
                ar, bk = operands(po, c, rr)
                gm = _dotp(ar, bk, _NT)
                m0 = pl.multiple_of((c * B_HEADS + 2 * p + rr) * CHUNK, CHUNK)
                a_ref[pl.ds(m0, CHUNK), :CHUNK] = jnp.where(strict, -gm[:CHUNK, :CHUNK], 0.0)
                lak_ref[pl.ds(m0, CHUNK), :CHUNK] = jnp.where(strict, gm[:CHUNK, CHUNK:], 0.0)
                arbk_ref[pl.ds(m0, CHUNK), :] = jnp.where(causal2, gm[CHUNK:, :], 0.0)
        return carry

    lax.fori_loop(0, B_HEADS // 2, build, 0)
    _invert_unit_lower_batch(a_ref, att_ref, xt_ref, ti_ref, RW_NMAT)

    def pair_chain(p, states):
        po = pl.multiple_of(p * 128, 128)
        ys = []
        for c in range(RW_CHUNKS):
            rows = slice(c * CHUNK, (c + 1) * CHUNK)
            vp = v_s[rows, pl.ds(po, 128)]
            gp = g_s[rows, pl.ds(po, 128)]
            bp = bon_s[rows, pl.ds(po, 128)]
            pcp = pc_s[c * 8:c * 8 + 1, pl.ds(po, 128)]
            lg = lng_ref[:, pl.ds(po, 128)]
            lb = lnb_ref[:, pl.ds(po, 128)]
            outs = []
            for rr in range(2):
                sl = slice(rr * B_DH, (rr + 1) * B_DH)
                s = states[rr]
                ar, bk = operands(po, c, rr)
                m0 = pl.multiple_of((c * B_HEADS + 2 * p + rr) * CHUNK, CHUNK)
                vh = vp[:, sl]
                ars = _dotp(ar, s, _NT)
                u = _dotp(ti_ref[pl.ds(m0, CHUNK), :CHUNK],
                          ars[:CHUNK] + _dotp(lak_ref[pl.ds(m0, CHUNK), :CHUNK], vh))
                uv = jnp.concatenate([u, vh], axis=0)
                y = ars[CHUNK:] + _dotp(arbk_ref[pl.ds(m0, CHUNK), :], uv)
                pc = pcp[:, sl]
                states[rr] = s * pc + _dotp(uv, bk * pc, _TN)
                mean = jnp.mean(y, axis=-1, keepdims=True)
                var = jnp.mean(jnp.square(y - mean), axis=-1, keepdims=True)
                yn = (y - mean) * lax.rsqrt(var + B_GN_EPS) * lg[:, sl] + lb[:, sl]
                bonus = jnp.sum(bp[:, sl], axis=-1, keepdims=True) * vh
                outs.append((yn + bonus) * gp[:, sl])
            ys.append(jnp.concatenate(outs, axis=1).astype(y_ref.dtype))
        return states, ys

    def group(gi, carry):
        pairs = [gi * RW_GROUP + r for r in range(RW_GROUP)]
        res = [pair_chain(p, [s_ref[2 * p], s_ref[2 * p + 1]]) for p in pairs]
        for p, (states, ys) in zip(pairs, res):
            s_ref[2 * p] = states[0]
            s_ref[2 * p + 1] = states[1]
            for c in range(RW_CHUNKS):
                y_ref[c * CHUNK:(c + 1) * CHUNK, pl.ds(pl.multiple_of(p * 128, 128), 128)] = ys[c]
        return carry

    lax.fori_loop(0, B_HEADS // 2 // RW_GROUP, group, 0)


def rwkv_prompt(zb, mu, w0, w_up, a0, a_up, g_up, k_k, k_a, r_k, ln_g, ln_b):
    m = zb.shape[0]
    nrow = RW_ROWS
    full = lambda shape: pl.BlockSpec(shape, lambda i: (0,) * len(shape))
    vec = lambda x: x.reshape(1, -1)
    wide = lambda: pltpu.VMEM((nrow, B_WIDTH), F32)
    mat = lambda rows: pltpu.VMEM((rows, 128), F32)
    return pl.pallas_call(
        _rwkv_kernel,
        grid=(m // nrow,),
        in_specs=[pl.BlockSpec((nrow, B_PROJ), lambda i: (i, 0)),
                  full((1, B_PROJ)), full((1, B_WIDTH)), full((B_W_RANK, B_WIDTH)), full((1, B_WIDTH)),
                  full((B_A_RANK, B_WIDTH)), full((B_G_RANK, B_WIDTH)), full((1, B_WIDTH)), full((1, B_WIDTH)),
                  full((1, B_WIDTH)), full((1, B_WIDTH)), full((1, B_WIDTH))],
        out_specs=[pl.BlockSpec((nrow, B_WIDTH), lambda i: (i, 0)), full((B_HEADS, B_DH, B_DH))],
        out_shape=[jax.ShapeDtypeStruct((m, B_WIDTH), BF16), jax.ShapeDtypeStruct((B_HEADS, B_DH, B_DH), F32)],
        scratch_shapes=[pltpu.VMEM((8, B_PROJ), F32),
                        wide(), wide(), wide(), wide(), wide(), wide(), wide(),
                        pltpu.VMEM((8 * RW_CHUNKS, B_WIDTH), F32),
                        mat(RW_NMAT * CHUNK), mat(CHUNK * CHUNK), mat(CHUNK * CHUNK), mat(RW_NMAT * CHUNK),
                        mat(RW_NMAT * CHUNK), mat(RW_NMAT * CHUNK)],
        compiler_params=_cparams("arbitrary"),
        name="rwkv_prompt",
    )(zb, vec(mu), vec(w0), w_up, vec(a0), a_up, g_up, vec(k_k), vec(k_a), vec(r_k), vec(ln_g), vec(ln_b))


def lambda_init(layer):
    return 0.8 - 0.6 * math.exp(-0.3 * layer)


def _row_to_col(row):
    n = row.shape[1]
    eye = lax.broadcasted_iota(jnp.int32, (n, n), 0) == lax.broadcasted_iota(jnp.int32, (n, n), 1)
    return jnp.sum(jnp.where(eye, row, 0.0), axis=1, keepdims=True)


def _col_to_row(col):
    n = col.shape[0]
    eye = lax.broadcasted_iota(jnp.int32, (n, n), 0) == lax.broadcasted_iota(jnp.int32, (n, n), 1)
    return jnp.sum(jnp.where(eye, col, 0.0), axis=0, keepdims=True)


def _gdn_step_kernel(z_ref, ba_ref, buf_ref, s_ref, convw_ref, alog_ref, dtb_ref, gnorm_ref,
                     o_ref, so_ref, bufo_ref):
    mixed = z_ref[0][:, :C_CONV_CH]
    buf = buf_ref[0]
    w = convw_ref[...]
    conv = jnp.sum(buf * w[:C_CONV - 1], axis=0, keepdims=True) + mixed * w[C_CONV - 1:]
    act = conv * _sigmoid(conv)
    bufo_ref[0] = jnp.concatenate([buf[1:], mixed], axis=0)
    ba = ba_ref[0]
    beta = _sigmoid(ba[:, :C_V_HEADS])
    eg = jnp.exp(-jnp.exp(alog_ref[...]) * _softplus(ba[:, C_V_HEADS:] + dtb_ref[...]))
    l2n = lambda t: t * lax.rsqrt(jnp.sum(t * t, axis=-1, keepdims=True) + 1e-6)
    outs = []
    for h in range(C_V_HEADS):
        kh = h // (C_V_HEADS // C_K_HEADS)
        q = _row_to_col(l2n(act[:, kh * C_DK:(kh + 1) * C_DK]) * (C_DK ** -0.5))
        k = _row_to_col(l2n(act[:, C_KEY + kh * C_DK:C_KEY + (kh + 1) * C_DK]))
        v = act[:, 2 * C_KEY + h * C_DV:2 * C_KEY + (h + 1) * C_DV]
        s = s_ref[0, h] * eg[:, h:h + 1]
        kv = jnp.sum(s * k, axis=0, keepdims=True)
        s = s + k * ((v - kv) * beta[:, h:h + 1])
        so_ref[0, h] = s
        o = jnp.sum(s * q, axis=0, keepdims=True)
        gate = z_ref[0][:, C_CONV_CH + h * C_DV:C_CONV_CH + (h + 1) * C_DV]
        on = o * lax.rsqrt(jnp.mean(o * o, axis=-1, keepdims=True) + NORM_EPS) * gnorm_ref[...]
        outs.append(on * (gate * _sigmoid(gate)))
    o_ref[0] = jnp.concatenate(outs, axis=1).astype(o_ref.dtype)


def gdn_step(z, ba, buf, s0, conv_w, a_log, dt_bias, gnorm):
    bd = z.shape[0]
    full = lambda shape: pl.BlockSpec(shape, lambda b: (0,) * len(shape))
    per = lambda *shape: pl.BlockSpec((1,) + shape, lambda b: (b,) + (0,) * len(shape))
    return pl.pallas_call(
        _gdn_step_kernel,
        grid=(bd,),
        in_specs=[per(1, C_CONV_CH + C_VAL), per(1, 2 * C_V_HEADS), per(C_CONV - 1, C_CONV_CH),
                  per(C_V_HEADS, C_DK, C_DV), full((C_CONV, C_CONV_CH)), full((1, C_V_HEADS)),
                  full((1, C_V_HEADS)), full((1, C_DV))],
        out_specs=[per(1, C_VAL), per(C_V_HEADS, C_DK, C_DV), per(C_CONV - 1, C_CONV_CH)],
        out_shape=[jax.ShapeDtypeStruct((bd, 1, C_VAL), BF16),
                   jax.ShapeDtypeStruct((bd, C_V_HEADS, C_DK, C_DV), F32),
                   jax.ShapeDtypeStruct((bd, C_CONV - 1, C_CONV_CH), F32)],
        compiler_params=_cparams("parallel"),
        name="gdn_step",
    )(z.reshape(bd, 1, -1), ba.reshape(bd, 1, -1), buf, s0, conv_w, a_log.reshape(1, -1),
      dt_bias.reshape(1, -1), gnorm.reshape(1, -1))


def _rwkv_step_kernel(z_ref, prev_ref, s_ref, mu_ref, w0_ref, wup_ref, a0_ref, aup_ref, gup_ref, kk_ref, ka_ref,
                      rk_ref, lng_ref, lnb_ref, y_ref, so_ref, r_s, k_s, v_s, kk_s, a_s, w_s):
    z = z_ref[...]
    zs = z + mu_ref[...] * (prev_ref[...] - z)
    o1, o2, o3 = B_WIDTH, 2 * B_WIDTH, 3 * B_WIDTH
    o4 = o3 + B_W_RANK
    o5 = o4 + B_A_RANK
    k = zs[:, o1:o2]
    w = -_softplus(-(w0_ref[...] + _dotp(jnp.tanh(zs[:, o3:o4]), wup_ref[...]))) - 0.5
    a = _sigmoid(a0_ref[...] + _dotp(zs[:, o4:o5], aup_ref[...]))
    g = _dotp(_sigmoid(zs[:, o5:]), gup_ref[...])
    r_s[...] = zs[:, :o1]
    k_s[...] = k * (1.0 + (a - 1.0) * ka_ref[...])
    v_s[...] = zs[:, o2:o3]
    kk_s[...] = k * kk_ref[...]
    a_s[...] = a
    w_s[...] = jnp.exp(-jnp.exp(w))

    def seq(b, carry):
        row = lambda ref: ref[pl.ds(b, 1), :]
        r, k2, v, kkraw, ab, dec = row(r_s), row(k_s), row(v_s), row(kk_s), row(a_s), row(w_s)
        ys = []
        for h in range(B_HEADS):
            sl = slice(h * B_DH, (h + 1) * B_DH)
            kkh = kkraw[:, sl]
            kkh = kkh / jnp.maximum(jnp.sqrt(jnp.sum(kkh * kkh, axis=-1, keepdims=True)), 1e-12)
            s = s_ref[b, h]
            sa = jnp.sum(s * (-kkh), axis=1, keepdims=True)
            s = s * dec[:, sl] + sa * (kkh * ab[:, sl]) + _row_to_col(v[:, sl]) * k2[:, sl]
            so_ref[b, h] = s
            y = _col_to_row(jnp.sum(s * r[:, sl], axis=1, keepdims=True))
            mean = jnp.mean(y, axis=-1, keepdims=True)
            var = jnp.mean(jnp.square(y - mean), axis=-1, keepdims=True)
            ys.append((y - mean) * lax.rsqrt(var + B_GN_EPS) * lng_ref[:, sl] + lnb_ref[:, sl]
                      + jnp.sum(r[:, sl] * k2[:, sl] * rk_ref[:, sl], axis=-1, keepdims=True) * v[:, sl])
        y_ref[pl.ds(b, 1), :] = jnp.concatenate(ys, axis=1)
        return carry

    lax.fori_loop(0, z.shape[0], seq, 0)
    y_ref[...] = y_ref[...] * g


def rwkv_step(zb, prev, s0, mu, w0, w_up, a0, a_up, g_up, k_k, k_a, r_k, ln_g, ln_b):
    bd = zb.shape[0]
    vec = lambda x: x.reshape(1, -1)
    wide = lambda: pltpu.VMEM((bd, B_WIDTH), F32)
    return pl.pallas_call(
        _rwkv_step_kernel,
        out_shape=[jax.ShapeDtypeStruct((bd, B_WIDTH), F32),
                   jax.ShapeDtypeStruct((bd, B_HEADS, B_DH, B_DH), F32)],
        scratch_shapes=[wide() for _ in range(6)],
        compiler_params=pltpu.CompilerParams(vmem_limit_bytes=VMEM_LIMIT),
        name="rwkv_step",
    )(zb, prev, s0, vec(mu), vec(w0), w_up, vec(a0), a_up, g_up, vec(k_k), vec(k_a), vec(r_k), vec(ln_g), vec(ln_b))


def _decode_attention_glue(qs, k, v, lamvec, subln, lam_init, cache_k, cache_v, page_table):
    bd = qs.shape[0]
    qf = qs.astype(F32).reshape(bd, 1, A_HEADS, 2, A_DH)
    k5 = k.reshape(bd, 1, A_HEADS, 2, A_DH)
    v4 = v.reshape(bd, 1, A_HEADS, A_VD)

    def update(carry, s, vals):
        m, l, acc = carry
        m_new = jnp.maximum(m, s.max(-1))
        corr = jnp.exp(m - m_new)
        p = jnp.exp(s - m_new[..., None])
        return (m_new, l * corr + p.sum(-1),
                acc * corr[..., None] + jnp.einsum('bhcqk,bkhe->bhcqe', p, vals))

    def page_step(carry, phys):
        kp = cache_k[phys].reshape(bd, PAGE_SIZE, A_HEADS, 2, A_DH)
        s = jnp.einsum('bqhcd,bkhcd->bhcqk', qf, kp)
        return update(carry, s, cache_v[phys]), None

    init = (jnp.full((bd, A_HEADS, 2, 1), NEG_INF, F32), jnp.zeros((bd, A_HEADS, 2, 1), F32),
            jnp.zeros((bd, A_HEADS, 2, 1, A_VD), F32))
    carry, _ = lax.scan(page_step, init, page_table.T)
    m, l, acc = update(carry, jnp.einsum('bqhcd,bkhcd->bhcqk', qf, k5), v4)
    o = acc / l[..., None]
    lam = jnp.exp(jnp.sum(lamvec[0] * lamvec[1])) - jnp.exp(jnp.sum(lamvec[2] * lamvec[3])) + lam_init
    o = (o[:, :, 0] - lam * o[:, :, 1]).reshape(bd, A_HEADS, A_VD)
    o = o * lax.rsqrt(jnp.mean(o * o, axis=-1, keepdims=True) + SUBLN_EPS) * subln * (1.0 - lam_init)
    return o.reshape(bd, A_WIDTH)


def _decode_rwkv_glue(zb, prev, s0, mu, w0, w_up, a0, a_up, g_up, k_k, k_a, r_k, ln_g, ln_b):
    bd = zb.shape[0]
    zs = zb + mu * (prev - zb)
    o1, o2, o3 = B_WIDTH, 2 * B_WIDTH, 3 * B_WIDTH
    o4 = o3 + B_W_RANK
    o5 = o4 + B_A_RANK
    r, k, v = zs[:, :o1], zs[:, o1:o2], zs[:, o2:o3]
    w = -jax.nn.softplus(-(w0 + jnp.tanh(zs[:, o3:o4]) @ w_up)) - 0.5
    decay = jnp.exp(-jnp.exp(w))
    a = jax.nn.sigmoid(a0 + zs[:, o4:o5] @ a_up)
    g = jax.nn.sigmoid(zs[:, o5:]) @ g_up
    heads = lambda t: t.reshape(bd, B_HEADS, B_DH)
    kk = heads(k * k_k)
    kk = kk / jnp.maximum(jnp.sqrt(jnp.sum(kk * kk, axis=-1, keepdims=True)), 1e-12)
    k = k * (1.0 + (a - 1.0) * k_a)
    r, k, v, decay, a = heads(r), heads(k), heads(v), heads(decay), heads(a)
    sa = jnp.einsum('bhij,bhj->bhi', s0, -kk)
    s = s0 * decay[:, :, None, :] + sa[..., None] * (kk * a)[:, :, None, :] + v[..., None] * k[:, :, None, :]
    y = jnp.einsum('bhij,bhj->bhi', s, r)
    mean = jnp.mean(y, axis=-1, keepdims=True)
    var = jnp.mean(jnp.square(y - mean), axis=-1, keepdims=True)
    y = ((y - mean) * lax.rsqrt(var + B_GN_EPS)).reshape(bd, B_WIDTH) * ln_g + ln_b
    bonus = jnp.sum(r * k * r_k, axis=-1, keepdims=True) * v
    return (y + bonus.reshape(bd, B_WIDTH)) * g, s


def _decode_gdn_glue(z, ba, buf, s0, conv_w, a_log, dt_bias, gnorm):
    bd = z.shape[0]
    xp = jnp.concatenate([buf, z[:, None, :C_CONV_CH]], axis=1)
    act = jax.nn.silu(jnp.sum(xp * conv_w[None], axis=1))
    l2n = lambda t: t * lax.rsqrt(jnp.sum(t * t, axis=-1, keepdims=True) + 1e-6)
    rep = C_V_HEADS // C_K_HEADS
    q = jnp.repeat(l2n(act[:, :C_KEY].reshape(bd, C_K_HEADS, C_DK)) * (C_DK ** -0.5), rep, axis=1)
    k = jnp.repeat(l2n(act[:, C_KEY:2 * C_KEY].reshape(bd, C_K_HEADS, C_DK)), rep, axis=1)
    v = act[:, 2 * C_KEY:].reshape(bd, C_V_HEADS, C_DV)
    beta = jax.nn.sigmoid(ba[:, :C_V_HEADS])
    g = -jnp.exp(a_log) * jax.nn.softplus(ba[:, C_V_HEADS:] + dt_bias)
    s = s0 * jnp.exp(g)[..., None, None]
    kv = jnp.einsum('bhkv,bhk->bhv', s, k)
    s = s + jnp.einsum('bhk,bhv->bhkv', k, (v - kv) * beta[..., None])
    o = jnp.einsum('bhkv,bhk->bhv', s, q)
    gate = z[:, C_CONV_CH:].reshape(bd, C_V_HEADS, C_DV)
    o = o * lax.rsqrt(jnp.mean(o * o, axis=-1, keepdims=True) + NORM_EPS) * gnorm * jax.nn.silu(gate)
    return o.reshape(bd, C_VAL), s, xp[:, 1:]


def kernel(x_prompt, x_sample, cache_k, cache_v, page_table, state_wkv, state_shift, state_gdn, state_conv, meta, norm_mix, norm_ffn, norm_final, w_in_even, w_out_even, lam_q1, lam_k1, lam_q2, lam_k2, subln, rw_mu, rw_w0, rw_w_up, rw_a0, rw_a_up, rw_g_up, rw_k_k, rw_k_a, rw_r_k, rw_ln_g, rw_ln_b, w_in_odd, conv_w, a_log, dt_bias, gdn_norm, w_out_odd, peer_wq, peer_keys, peer_u, peer_v):
    x = jnp.concatenate([x_sample.reshape(DEC_BATCH, D_MODEL),
                         jnp.zeros((SEQ0 - DEC_BATCH, D_MODEL), F32),
                         meta.astype(F32), x_prompt.reshape(SEQ, D_MODEL)], axis=0)
    past_len = page_table.shape[1] * PAGE_SIZE
    pos = jnp.concatenate([jnp.full((DEC_BATCH,), past_len, jnp.int32),
                           jnp.zeros((SEQ0 - DEC_BATCH,), jnp.int32),
                           jnp.arange(T_PROMPT, dtype=jnp.int32)])
    cos, sin = _rope_tables(pos)
    nd = DEC_BATCH

    def peer(xin, layer):
        hn = rmsnorm_rows(xin, norm_ffn[layer])
        qp = matmul_cols(hn, peer_wq[layer], 0, P_HEADS * P_DQ, tn=1024)
        c1, e2, thr = peer_select(qp, peer_keys, layer)
        return peer_dense(hn, c1, e2, thr, peer_u, peer_v, layer)

    h = rmsnorm_rows(x, norm_mix[0])
    z_att = matmul_cols(h, w_in_even[0], 0, 3 * A_WIDTH, tn=1024)
    zb = matmul_cols(h, w_in_even[0], 3 * A_WIDTH, B_PROJ, tn=256)
    qb, kf, kb, vb = rope_qkv(z_att, cos, sin)
    lamvec = jnp.stack([lam_q1[0], lam_k1[0], lam_q2[0], lam_k2[0]])
    att = attn_prompt(qb, kb, vb, lamvec, subln[0], lambda_init(0))
    rw, wkv_p = rwkv_prompt(zb, rw_mu[0], rw_w0[0], rw_w_up[0], rw_a0[0], rw_a_up[0], rw_g_up[0],
                            rw_k_k[0], rw_k_a[0], rw_r_k[0], rw_ln_g[0], rw_ln_b[0])
    v_f = z_att[:, 2 * A_WIDTH:]
    per_map = lambda t: jnp.repeat(t[:nd].reshape(nd, A_HEADS, A_VD), 2, axis=1)
    own_map = (jnp.arange(A_VD, dtype=jnp.int32)[None, :] // A_DH) == (jnp.arange(2 * A_HEADS, dtype=jnp.int32)[:, None] % 2)
    att_d = attn_decode(jnp.where(own_map[None], per_map(qb), 0), per_map(kf), per_map(v_f),
                        cache_k, cache_v, page_table, lamvec, subln[0], lambda_init(0)).reshape(nd, A_WIDTH)
    rw_d, wkv_s = rwkv_step(zb[:nd], state_shift[0], state_wkv[0], rw_mu[0], rw_w0[0], rw_w_up[0], rw_a0[0],
                            rw_a_up[0], rw_g_up[0], rw_k_k[0], rw_k_a[0], rw_r_k[0], rw_ln_g[0], rw_ln_b[0])
    mix = jnp.concatenate([att.at[:nd].set(att_d.astype(BF16)), rw.at[:nd].set(rw_d.astype(BF16))], axis=1)
    x = matmul_cols(mix, w_out_even[0], 0, D_MODEL, tn=1024, residual=x)

    x, h = rmsnorm_rows(x, norm_mix[1], add=peer(x, 0), with_sum=True)
    z1 = matmul_cols(h, w_in_odd[0], 0, C_CONV_CH + C_VAL, tn=1024)
    ba = matmul_cols(h, w_in_odd[0][:, C_CONV_CH + C_VAL:], 0, 2 * C_V_HEADS, tn=2 * C_V_HEADS)
    o, gdn_p = gdn_prompt(z1, ba, conv_w[0], a_log[0], dt_bias[0], gdn_norm[0])
    o_d, gdn_s, conv_s = gdn_step(z1[:nd], ba[:nd], state_conv[0], state_gdn[0], conv_w[0], a_log[0],
                                  dt_bias[0], gdn_norm[0])
    x = matmul_cols(o.at[:nd].set(o_d.reshape(nd, C_VAL)), w_out_odd[0], 0, D_MODEL, tn=512, residual=x)
    xf = rmsnorm_rows(x, norm_final, add=peer(x, 1), out_dtype=F32)

    y_prompt = xf[SEQ0 + N_META:].reshape(1, SEQ, D_MODEL)
    y_sample = xf[:nd].reshape(nd, 1, D_MODEL)
    k_p = kf[SEQ0:].reshape(1, 1, T_PROMPT, A_HEADS, 2 * A_DH)
    v_p = v_f[SEQ0:].reshape(1, 1, T_PROMPT, A_HEADS, A_VD)
    k_s = kf[:nd].reshape(1, nd, 1, A_HEADS, 2 * A_DH)
    v_s = v_f[:nd].reshape(1, nd, 1, A_HEADS, A_VD)
    shift_p = zb[R_ROWS - 1:].reshape(1, 1, B_PROJ)
    shift_s = zb[:nd].reshape(1, nd, B_PROJ)
    conv_p = z1[R_ROWS - (C_CONV - 1):, :C_CONV_CH].reshape(1, 1, C_CONV - 1, C_CONV_CH)
    return (y_prompt, y_sample, k_p, v_p, k_s, v_s,
            wkv_p.reshape(1, 1, B_HEADS, B_DH, B_DH), wkv_s.reshape(1, nd, B_HEADS, B_DH, B_DH),
            shift_p, shift_s,
            gdn_p.reshape(1, 1, C_V_HEADS, C_DK, C_DV), gdn_s.reshape(1, nd, C_V_HEADS, C_DK, C_DV),
            conv_p, conv_s.reshape(1, nd, C_CONV - 1, C_CONV_CH))
```

```python
import functools
import math

import jax
import jax.numpy as jnp
from jax import lax
from jax.experimental import pallas as pl
from jax.experimental.pallas import tpu as pltpu

D_MODEL = 2048
SEQ = 8192
DEC_BATCH = 32
N_META = 16
PAGE_SIZE = 128
NORM_EPS = 1e-6
SUBLN_EPS = 1e-5
ROPE_THETA = 10000.0
NEG_INF = -1e30

A_HEADS = 8
A_DH = 64
A_VD = 128
A_WIDTH = 1024

B_HEADS = 16
B_DH = 64
B_WIDTH = 1024
B_W_RANK = 64
B_A_RANK = 64
B_G_RANK = 128
B_PROJ = 3328
B_GN_EPS = 64e-5

C_K_HEADS = 16
C_V_HEADS = 32
C_DK = 128
C_DV = 128
C_KEY = 2048
C_VAL = 4096
C_CONV = 4
C_CONV_CH = 8192

P_HEADS = 8
P_NKEYS = 128
P_TOPK = 16
P_DQ = 256

T_PROMPT = N_META + SEQ
R_ROWS = 8320
SEQ0 = R_ROWS - T_PROMPT
CHUNK = 64

VMEM_LIMIT = 56 * 1024 * 1024

F32 = jnp.float32
BF16 = jnp.bfloat16
HIGHEST = lax.Precision.HIGHEST


def _cparams(*sem):
    return pltpu.CompilerParams(dimension_semantics=sem, vmem_limit_bytes=VMEM_LIMIT)


def _rmsnorm_kernel(*refs, has_add, with_sum):
    x = refs[0][...]
    if has_add:
        x = x + refs[1][...]
    g_ref = refs[1 + has_add]
    outs = refs[2 + has_add:]
    if with_sum:
        outs[0][...] = x
    y = x * lax.rsqrt(jnp.mean(x * x, axis=-1, keepdims=True) + NORM_EPS)
    outs[-1][...] = (y * g_ref[...]).astype(outs[-1].dtype)


def rmsnorm_rows(x, g, add=None, with_sum=False, out_dtype=BF16, tm=640):
    m, d = x.shape
    row = pl.BlockSpec((tm, d), lambda i: (i, 0))
    ins = [x] + ([add] if add is not None else [])
    out_shape = [jax.ShapeDtypeStruct((m, d), out_dtype)]
    if with_sum:
        out_shape.insert(0, jax.ShapeDtypeStruct((m, d), F32))
    res = pl.pallas_call(
        functools.partial(_rmsnorm_kernel, has_add=add is not None, with_sum=with_sum),
        grid=(m // tm,),
        in_specs=[row] * len(ins) + [pl.BlockSpec((1, d), lambda i: (0, 0))],
        out_specs=[row] * len(out_shape),
        out_shape=out_shape,
        compiler_params=_cparams("parallel"),
        name="rmsnorm_rows",
    )(*ins, g.reshape(1, d))
    return res if with_sum else res[0]


def _matmul_kernel(a_ref, w_ref, *rest, has_res):
    if has_res:
        r_ref, o_ref, wb_ref = rest
    else:
        o_ref, wb_ref = rest

    @pl.when(pl.program_id(1) == 0)
    def _():
        wb_ref[...] = w_ref[...].astype(BF16)

    acc = jnp.dot(a_ref[...], wb_ref[...], preferred_element_type=F32)
    if has_res:
        acc = acc + r_ref[...]
    o_ref[...] = acc


def matmul_cols(a, w, col0, n, tn, tm=640, residual=None):
    m, k = a.shape
    assert w.shape[0] == k and n % tn == 0 and col0 % tn == 0 and m % tm == 0
    cb0 = col0 // tn
    in_specs = [pl.BlockSpec((tm, k), lambda j, i: (i, 0)),
                pl.BlockSpec((k, tn), lambda j, i: (0, cb0 + j))]
    args = [a, w]
    if residual is not None:
        in_specs.append(pl.BlockSpec((tm, tn), lambda j, i: (i, j)))
        args.append(residual)
    return pl.pallas_call(
        functools.partial(_matmul_kernel, has_res=residual is not None),
        grid=(n // tn, m // tm),
        in_specs=in_specs,
        out_specs=pl.BlockSpec((tm, tn), lambda j, i: (i, j)),
        out_shape=jax.ShapeDtypeStruct((m, n), F32),
        scratch_shapes=[pltpu.VMEM((k, tn), BF16)],
        compiler_params=_cparams("parallel", "arbitrary"),
        name="matmul_cols",
    )(*args)


def _rope_tables(pos):
    half = A_DH // 2
    inv_freq = ROPE_THETA ** (-jnp.arange(half, dtype=F32) / half)
    ang = pos.astype(F32)[:, None] * inv_freq[None, :]
    cos = jnp.tile(jnp.cos(ang), (1, 4))
    sin = jnp.sin(ang)
    sin = jnp.tile(jnp.concatenate([-sin, sin], axis=1), (1, 2))
    return cos, sin


def _rope_kernel(z_ref, cos_ref, sin_ref, qb_ref, kf_ref, kb_ref, vb_ref):
    cos = cos_ref[...]
    sin = sin_ref[...]
    lane = lax.broadcasted_iota(jnp.int32, cos.shape, 1)
    first_half = (lane % A_DH) < (A_DH // 2)

    def rope(x):
        partner = jnp.where(first_half, pltpu.roll(x, 128 - A_DH // 2, 1), pltpu.roll(x, A_DH // 2, 1))
        return x * cos + partner * sin

    for h in range(A_HEADS):
        sl = slice(h * 128, (h + 1) * 128)
        q = rope(z_ref[:, sl])
        qb_ref[:, sl] = (q * (A_DH ** -0.5)).astype(BF16)
        k = rope(z_ref[:, A_WIDTH + h * 128:A_WIDTH + (h + 1) * 128])
        kf_ref[:, sl] = k
        kb_ref[:, sl] = k.astype(BF16)
    vb_ref[...] = z_ref[:, 2 * A_WIDTH:3 * A_WIDTH].astype(BF16)


def rope_qkv(z_att, cos, sin, tm=640):
    m = z_att.shape[0]
    row = lambda w: pl.BlockSpec((tm, w), lambda i: (i, 0))
    return pl.pallas_call(
        _rope_kernel,
        grid=(m // tm,),
        in_specs=[row(3 * A_WIDTH), row(128), row(128)],
        out_specs=[row(A_WIDTH)] * 4,
        out_shape=[jax.ShapeDtypeStruct((m, A_WIDTH), BF16), jax.ShapeDtypeStruct((m, A_WIDTH), F32),
                   jax.ShapeDtypeStruct((m, A_WIDTH), BF16), jax.ShapeDtypeStruct((m, A_WIDTH), BF16)],
        compiler_params=_cparams("parallel"),
        name="rope_qkv",
    )(z_att, cos, sin)


def _lambda(lam_ref, lam_init):
    l1 = jnp.exp(jnp.sum(lam_ref[0:1, :] * lam_ref[1:2, :], axis=-1, keepdims=True))
    l2 = jnp.exp(jnp.sum(lam_ref[2:3, :] * lam_ref[3:4, :], axis=-1, keepdims=True))
    return l1 - l2 + lam_init


def _subln(o, subln_ref, lam_init):
    y = o * lax.rsqrt(jnp.mean(o * o, axis=-1, keepdims=True) + SUBLN_EPS)
    return y * subln_ref[...] * (1.0 - lam_init)


def _attn_prompt_kernel(q_ref, k_ref, v_ref, lam_ref, subln_ref, o_ref, *, tq, lam_init):
    qb = pl.program_id(1)
    q = q_ref[...]
    qidx = qb * tq + lax.broadcasted_iota(jnp.int32, (tq, tq), 0)
    kiota = lax.broadcasted_iota(jnp.int32, (tq, tq), 1)
    dn = (((1,), (1,)), ((), ()))

    def body(kb, carry, masked):
        off = pl.multiple_of(kb * tq, tq)
        k = k_ref[pl.ds(off, tq), :]
        v = v_ref[pl.ds(off, tq), :]
        if masked:
            kidx = off + kiota
            mask = (kidx <= qidx) & (kidx >= SEQ0)
        out = []
        for c in range(2):
            m, l, acc = carry[c]
            s = lax.dot_general(q[:, c * A_DH:(c + 1) * A_DH], k[:, c * A_DH:(c + 1) * A_DH], dn,
                                preferred_element_type=F32)
            if masked:
                s = jnp.where(mask, s, NEG_INF)
            m_new = jnp.maximum(m, jnp.max(s, axis=-1, keepdims=True))
            corr = jnp.exp(m - m_new)
            p = jnp.exp(s - m_new)
            l = l * corr + jnp.sum(p, axis=-1, keepdims=True)
            acc = acc * corr + jnp.dot(p.astype(BF16), v, preferred_element_type=F32)
            out.append((m_new, l, acc))
        return tuple(out)

    init = tuple((jnp.full((tq, 1), NEG_INF, F32), jnp.zeros((tq, 1), F32), jnp.zeros((tq, A_VD), F32))
                 for _ in range(2))
    carry = body(0, init, True)
    carry = lax.fori_loop(1, qb, functools.partial(body, masked=False), carry)
    (_, l1, a1), (_, l2, a2) = lax.cond(qb > 0, lambda c: body(qb, c, True), lambda c: c, carry)
    o = a1 / l1 - _lambda(lam_ref, lam_init) * (a2 / l2)
    o_ref[...] = _subln(o, subln_ref, lam_init).astype(o_ref.dtype)


def attn_prompt(qb, kb, vb, lamvec, subln, lam_init, tq=640):
    m = qb.shape[0]
    return pl.pallas_call(
        functools.partial(_attn_prompt_kernel, tq=tq, lam_init=lam_init),
        grid=(A_HEADS, m // tq),
        in_specs=[pl.BlockSpec((tq, 128), lambda h, i: (i, h)),
                  pl.BlockSpec((m, 128), lambda h, i: (0, h)),
                  pl.BlockSpec((m, 128), lambda h, i: (0, h)),
                  pl.BlockSpec((4, A_DH), lambda h, i: (0, 0)),
                  pl.BlockSpec((1, A_VD), lambda h, i: (0, 0))],
        out_specs=pl.BlockSpec((tq, 128), lambda h, i: (i, h)),
        out_shape=jax.ShapeDtypeStruct((m, A_WIDTH), BF16),
        compiler_params=_cparams("parallel", "arbitrary"),
        name="attn_prompt",
    )(qb, kb, vb, lamvec, subln.reshape(1, A_VD))


PAGES_PER_STEP = 4


def _attn_decode_kernel(pt_ref, q_ref, kn_ref, vn_ref, *rest, lam_init):
    pages = rest[:2 * PAGES_PER_STEP]
    lam_ref, subln_ref, o_ref, m_s, l_s, acc_s = rest[2 * PAGES_PER_STEP:]
    p = pl.program_id(1)

    @pl.when(p == 0)
    def _():
        m_s[...] = jnp.full_like(m_s, NEG_INF)
        l_s[...] = jnp.zeros_like(l_s)
        acc_s[...] = jnp.zeros_like(acc_s)

    nk = PAGE_SIZE * A_HEADS

    def page(ref):
        return ref[0].reshape(nk, A_VD).astype(BF16)

    own_head = (lax.broadcasted_iota(jnp.int32, (2 * A_HEADS, nk), 0) // 2
                == lax.broadcasted_iota(jnp.int32, (2 * A_HEADS, nk), 1) % A_HEADS)
    q = q_ref[0]
    m, l, acc = m_s[...], l_s[...], acc_s[...]
    ss = [jnp.where(own_head, lax.dot_general(q, page(pages[2 * i]), _NT_DIMS, preferred_element_type=F32), NEG_INF)
          for i in range(PAGES_PER_STEP)]
    m_new = m
    for s in ss:
        m_new = jnp.maximum(m_new, jnp.max(s, axis=-1, keepdims=True))
    corr = jnp.exp(m - m_new)
    l = l * corr
    acc = acc * corr
    for i, s in enumerate(ss):
        pr = jnp.exp(s - m_new)
        l = l + jnp.sum(pr, axis=-1, keepdims=True)
        acc = acc + jnp.dot(pr.astype(BF16), page(pages[2 * i + 1]), preferred_element_type=F32)
    m = m_new
    m_s[...], l_s[...], acc_s[...] = m, l, acc

    @pl.when(p == pl.num_programs(1) - 1)
    def _():
        kn = kn_ref[0].astype(BF16).astype(F32)
        vn = vn_ref[0].astype(BF16).astype(F32)
        s = jnp.sum(q.astype(F32) * kn, axis=-1, keepdims=True)
        m_new = jnp.maximum(m, s)
        corr = jnp.exp(m - m_new)
        pr = jnp.exp(s - m_new)
        lf = l * corr + pr
        o = (acc * corr + pr.astype(BF16).astype(F32) * vn) / lf
        lam = _lambda(lam_ref, lam_init)
        outs = [_subln(o[2 * h:2 * h + 1] - lam * o[2 * h + 1:2 * h + 2], subln_ref, lam_init)
                for h in range(A_HEADS)]
        o_ref[0] = jnp.concatenate(outs, axis=1).astype(o_ref.dtype)


def attn_decode(qm, k_new, v_new, cache_k, cache_v, page_table, lamvec, subln, lam_init):
    bd = qm.shape[0]
    n_pages = page_table.shape[1]
    assert n_pages % PAGES_PER_STEP == 0
    page_specs = []
    for i in range(PAGES_PER_STEP):
        idx = lambda b, p, pt, i=i: (0, pt[b, p * PAGES_PER_STEP + i], 0, 0, 0)
        spec = pl.BlockSpec((None, 1, PAGE_SIZE, A_HEADS, 128), idx)
        page_specs += [spec, spec]
    per_seq = lambda r, w: pl.BlockSpec((1, r, w), lambda b, p, pt: (b, 0, 0))
    grid_spec = pltpu.PrefetchScalarGridSpec(
        num_scalar_prefetch=1,
        grid=(bd, n_pages // PAGES_PER_STEP),
        in_specs=[per_seq(2 * A_HEADS, A_VD)] * 3 + page_specs
                 + [pl.BlockSpec((4, A_DH), lambda b, p, pt: (0, 0)), pl.BlockSpec((1, A_VD), lambda b, p, pt: (0, 0))],
        out_specs=per_seq(1, A_WIDTH),
        scratch_shapes=[pltpu.VMEM((2 * A_HEADS, 1), F32), pltpu.VMEM((2 * A_HEADS, 1), F32),
                        pltpu.VMEM((2 * A_HEADS, A_VD), F32)],
    )
    args = [qm, k_new, v_new] + [cache_k, cache_v] * PAGES_PER_STEP + [lamvec, subln.reshape(1, A_VD)]
    return pl.pallas_call(
        functools.partial(_attn_decode_kernel, lam_init=lam_init),
        grid_spec=grid_spec,
        out_shape=jax.ShapeDtypeStruct((bd, 1, A_WIDTH), BF16),
        compiler_params=_cparams("parallel", "arbitrary"),
        name="attn_decode",
    )(page_table, *args)


def _top_values(s, k):
    n = s.shape[0]
    row = lax.broadcasted_iota(jnp.int32, s.shape, 0).astype(F32)
    vals = []
    for _ in range(k):
        m = jnp.max(s, axis=0, keepdims=True)
        first = jnp.min(jnp.where(s == m, row, float(n)), axis=0, keepdims=True)
        s = jnp.where(row == first, -jnp.inf, s)
        vals.append(m)
    return jnp.concatenate(vals, axis=0)


def _peer_select_kernel(q_ref, keys_ref, c1_ref, e2_ref, thr_ref):
    dn = (((1,), (1,)), ((), ()))

    def head(h, carry):
        qo = pl.multiple_of(h * P_DQ, P_DQ)
        s1 = lax.dot_general(keys_ref[0, h], q_ref[:, pl.ds(qo, 128)], dn,
                             precision=HIGHEST, preferred_element_type=F32)
        s2 = lax.dot_general(keys_ref[1, h], q_ref[:, pl.ds(qo + 128, 128)], dn,
                             precision=HIGHEST, preferred_element_type=F32)
        t1 = _top_values(s1, P_TOPK)
        t2 = _top_values(s2, P_TOPK)
        e1 = jnp.exp(s1 - t1[0:1])
        e2 = jnp.exp(s2 - t2[0:1])
        et1 = jnp.exp(t1 - t1[0:1])
        et2 = jnp.exp(t2 - t2[0:1])
        nb = [P_TOPK // (a + 1) for a in range(P_TOPK)]
        cand = jnp.concatenate([et1[a:a + 1] * et2[:nb[a]] for a in range(P_TOPK)], axis=0)
        top = _top_values(cand, P_TOPK)
        inv_z = 1.0 / jnp.sum(top, axis=0, keepdims=True)
        scaled = jnp.concatenate([(et1[a:a + 1] * inv_z) * et2[:nb[a]] for a in range(P_TOPK)], axis=0)
        thr = jnp.min(jnp.where(cand >= top[P_TOPK - 1:P_TOPK], scaled, jnp.inf), axis=0, keepdims=True)
        ro = pl.multiple_of(h * 128, 128)
        c1_ref[pl.ds(ro, 128), :] = e1 * inv_z
        e2_ref[pl.ds(ro, 128), :] = e2
        thr_ref[pl.ds(h, 1), :] = thr
        return carry

    lax.fori_loop(0, P_HEADS, head, 0)


def peer_select(qp, keys, layer, tm=640):
    m = qp.shape[0]
    col = lambda r: pl.BlockSpec((r, tm), lambda i: (0, i))
    return pl.pallas_call(
        _peer_select_kernel,
        grid=(m // tm,),
        in_specs=[pl.BlockSpec((tm, P_HEADS * P_DQ), lambda i: (i, 0)),
                  pl.BlockSpec((None, 2, P_HEADS, P_NKEYS, 128), lambda i: (layer, 0, 0, 0, 0))],
        out_specs=[col(P_HEADS * 128), col(P_HEADS * 128), col(P_HEADS)],
        out_shape=[jax.ShapeDtypeStruct((P_HEADS * 128, m), F32), jax.ShapeDtypeStruct((P_HEADS * 128, m), F32),
                   jax.ShapeDtypeStruct((P_HEADS, m), F32)],
        compiler_params=_cparams("parallel"),
        name="peer_select",
    )(qp, keys)


def _peer_dense_kernel(h_ref, c1_ref, e2_ref, thr_ref, u_ref, v_ref, o_ref, *, te):
    j = pl.program_id(1)

    @pl.when(j == 0)
    def _():
        o_ref[...] = jnp.zeros_like(o_ref)

    sub = 256
    total = None
    for cb in range(te // sub):
        rows = slice(cb * sub, (cb + 1) * sub)
        g = lax.dot_general(u_ref[rows, :].astype(BF16), h_ref[...], _NT_DIMS, preferred_element_type=F32)
        act = 0.5 * g * (1.0 + lax.erf(g * (2.0 ** -0.5)))
        parts = []
        for ib in range(sub // 128):
            i1 = j * (te // 128) + cb * (sub // 128) + ib
            w = jnp.zeros((128, g.shape[1]), F32)
            for h in range(P_HEADS):
                p = e2_ref[h * 128:(h + 1) * 128, :] * c1_ref[pl.ds(h * 128 + i1, 1), :]
                w = w + jnp.where(p >= thr_ref[h:h + 1, :], p, 0.0)
            parts.append(act[ib * 128:(ib + 1) * 128, :] * w)
        a = jnp.concatenate(parts, axis=0).T.astype(BF16)
        y = jnp.dot(a, v_ref[rows, :].astype(BF16), preferred_element_type=F32)
        total = y if total is None else total + y
    o_ref[...] += total


_NT_DIMS = (((1,), (1,)), ((), ()))


def peer_dense(hn, c1, e2, thr, u_tabs, v_tabs, layer, tm=640, te=512):
    m, d = hn.shape
    ne = u_tabs.shape[1]
    col = lambda r: pl.BlockSpec((r, tm), lambda i, j: (0, i))
    tab = pl.BlockSpec((None, te, d), lambda i, j: (layer, j, 0))
    return pl.pallas_call(
        functools.partial(_peer_dense_kernel, te=te),
        grid=(m // tm, ne // te),
        in_specs=[pl.BlockSpec((tm, d), lambda i, j: (i, 0)),
                  col(P_HEADS * 128), col(P_HEADS * 128), col(P_HEADS), tab, tab],
        out_specs=pl.BlockSpec((tm, d), lambda i, j: (i, 0)),
        out_shape=jax.ShapeDtypeStruct((m, d), F32),
        compiler_params=_cparams("parallel", "arbitrary"),
        name="peer_dense",
    )(hn, c1, e2, thr, u_tabs, v_tabs)


SEQ_PASSES = 1


def _dotp(a, b, dims=(((1,), (0,)), ((), ())), passes=None):
    passes = SEQ_PASSES if passes is None else passes
    dot = lambda x, y: lax.dot_general(x, y, dims, preferred_element_type=F32)
    a_hi = a.astype(BF16)
    b_hi = b.astype(BF16)
    if passes == 1:
        return dot(a_hi, b_hi)
    a_lo = (a - a_hi.astype(F32)).astype(BF16)
    b_lo = (b - b_hi.astype(F32)).astype(BF16)
    return dot(a_hi, b_hi) + dot(a_lo, b_hi) + dot(a_hi, b_lo)


_NT = (((1,), (1,)), ((), ()))
_TN = (((0,), (0,)), ((), ()))


def _tri_masks():
    i = lax.broadcasted_iota(jnp.int32, (CHUNK, CHUNK), 0)
    j = lax.broadcasted_iota(jnp.int32, (CHUNK, CHUNK), 1)
    return i >= j, i > j


def _invert_unit_lower_batch(a_ref, at_ref, xt_ref, ti_ref, nmat, pair_blockdiag=False):
    zpad = jnp.zeros((128 - nmat, 128), F32)
    for t in range(CHUNK):
        slab = a_ref[pl.ds(t, nmat, stride=CHUNK), :]
        at_ref[t * CHUNK:(t + 1) * CHUNK, :] = jnp.concatenate([slab, zpad], axis=0).T[:CHUNK, :]
    sub = lax.broadcasted_iota(jnp.int32, (8, 128), 0)
    for tb in range(CHUNK // 8):
        def row(ti, carry, tb=tb):
            t = tb * 8 + ti
            acc = [jnp.zeros((8, 128), F32) for _ in range(tb)] + [(sub == ti).astype(F32)]
            for sb in range(tb + 1):
                for s in range(8):
                    sg = sb * 8 + s
                    a = at_ref[pl.ds(t * CHUNK + sg, 1), :]
                    for cb in range(sb + 1):
                        acc[cb] = acc[cb] - a * xt_ref[sg * CHUNK + cb * 8:sg * CHUNK + cb * 8 + 8, :]
            for cb in range(tb + 1):
                xt_ref[pl.ds(pl.multiple_of(t * CHUNK + cb * 8, 8), 8), :] = acc[cb]
            return carry
        lax.fori_loop(0, 8, row, 0)
    zrow = jnp.zeros((128 - CHUNK, 128), F32)
    odd = lax.broadcasted_iota(jnp.int32, (nmat, 128), 0) % 2 == 1
    for t in range(CHUNK):
        slab = xt_ref[t * CHUNK:(t + 1) * CHUNK, :]
        out = jnp.concatenate([slab, zrow], axis=0).T[:nmat, :]
        if pair_blockdiag:
            out = jnp.where(odd, pltpu.roll(out, CHUNK, 1), out)
        ti_ref[pl.ds(t, nmat, stride=CHUNK), :] = out


def _softplus(x):
    return jnp.maximum(x, 0.0) + jnp.log(1.0 + jnp.exp(-jnp.abs(x)))


def _sigmoid(x):
    return 1.0 / (1.0 + jnp.exp(-x))


GDN_ROWS = 2 * CHUNK
GDN_NMAT = 2 * C_V_HEADS
GDN_KH = 2


def _gdn_kernel(z_ref, ba_ref, bat_ref, convw_ref, alog_ref, dtb_ref, alogt_ref, dtbt_ref, gnorm_ref,
                o_ref, s_ref,
                halo_ref, q_s, k_s, kt_s, v_s, gcb_ref, bb_ref, gct_ref, a_ref, at_ref, xt_ref, ti_ref, att_ref):
    step = pl.program_id(0)
    nrow = GDN_ROWS

    @pl.when(step == 0)
    def _():
        s_ref[...] = jnp.zeros_like(s_ref)
        halo_ref[...] = jnp.zeros_like(halo_ref)
        xt_ref[...] = jnp.zeros_like(xt_ref)
        a_ref[...] = jnp.zeros_like(a_ref)
        att_ref[...] = jnp.zeros_like(att_ref)

    valid = (step * nrow + lax.broadcasted_iota(jnp.int32, (nrow, 1), 0)) >= SEQ0
    valid_t = (step * nrow + lax.broadcasted_iota(jnp.int32, (1, nrow), 1)) >= SEQ0
    beta = jnp.where(valid, _sigmoid(ba_ref[:, :C_V_HEADS]), 0.0)
    g = jnp.where(valid, -jnp.exp(alog_ref[...]) * _softplus(ba_ref[:, C_V_HEADS:] + dtb_ref[...]), 0.0)
    g_t = jnp.where(valid_t, -jnp.exp(alogt_ref[...]) * _softplus(bat_ref[C_V_HEADS:, :] + dtbt_ref[...]), 0.0)
    causal, strict = _tri_masks()
    tri = causal.astype(F32)
    hi = lambda a, b, dims=(((1,), (0,)), ((), ())): lax.dot_general(a, b, dims, precision=HIGHEST,
                                                                      preferred_element_type=F32)
    gc = jnp.concatenate([hi(tri, g[c * CHUNK:(c + 1) * CHUNK]) for c in range(2)], axis=0)
    gct_ref[...] = jnp.concatenate([hi(g_t[:, c * CHUNK:(c + 1) * CHUNK], tri, _NT) for c in range(2)], axis=1)
    for h in range(C_V_HEADS):
        gcb_ref[h] = jnp.broadcast_to(gc[:, h:h + 1], (nrow, 128))
        bb_ref[h] = jnp.broadcast_to(beta[:, h:h + 1], (nrow, 128))

    def conv_act(col0):
        x = z_ref[:, col0:col0 + 128]
        xe = jnp.concatenate([halo_ref[:, col0:col0 + 128], x], axis=0)
        w = convw_ref[:, col0:col0 + 128]
        y = (w[3:4] * x + w[2:3] * pltpu.roll(xe, 1, 0)[8:] + w[1:2] * pltpu.roll(xe, 2, 0)[8:]
             + w[0:1] * pltpu.roll(xe, 3, 0)[8:])
        return y * _sigmoid(y)

    def l2n(x):
        return x * lax.rsqrt(jnp.sum(x * x, axis=-1, keepdims=True) + 1e-6)

    for h in range(C_K_HEADS):
        q_s[:, h * 128:(h + 1) * 128] = l2n(conv_act(h * 128)) * (C_DK ** -0.5)
        kh_tile = l2n(conv_act(C_KEY + h * 128))
        k_s[:, h * 128:(h + 1) * 128] = kh_tile
        kt_s[h * 128:(h + 1) * 128, :] = kh_tile.T
    for h in range(C_V_HEADS):
        v_s[:, h * 128:(h + 1) * 128] = conv_act(2 * C_KEY + h * 128)
    halo_ref[...] = z_ref[nrow - 8:nrow, :C_CONV_CH]

    def build(kh, carry):
        ko = pl.multiple_of(kh * 128, 128)
        kt = kt_s[pl.ds(ko, 128), :]
        for c in range(2):
            rows = slice(c * CHUNK, (c + 1) * CHUNK)
            kq = jnp.concatenate([k_s[rows, pl.ds(ko, 128)], q_s[rows, pl.ds(ko, 128)]], axis=0)
            g2 = _dotp(kq, kt[:, c * CHUNK:(c + 1) * CHUNK])
            p0 = pl.multiple_of((c * C_K_HEADS + kh) * 128, 128)
            for r in range(2):
                vh = 2 * kh + r
                diff = gcb_ref[vh][rows, :CHUNK] - gct_ref[pl.ds(vh, 1), c * CHUNK:(c + 1) * CHUNK]
                dec = jnp.where(causal, jnp.exp(jnp.where(causal, diff, 0.0)), 0.0)
                a_ref[pl.ds(p0 + r * CHUNK, CHUNK), :CHUNK] = jnp.where(
                    strict, bb_ref[vh][rows, :CHUNK] * g2[:CHUNK] * dec, 0.0)
                att_ref[pl.ds(p0 + r * CHUNK, CHUNK), r * CHUNK:(r + 1) * CHUNK] = jnp.where(
                    causal, g2[CHUNK:] * dec, 0.0)
        return carry

    lax.fori_loop(0, C_K_HEADS, build, 0)
    _invert_unit_lower_batch(a_ref, at_ref, xt_ref, ti_ref, GDN_NMAT, pair_blockdiag=True)

    def bdiag(blocks):
        r, w = blocks[0].shape
        return jnp.concatenate(
            [jnp.concatenate([b if j == i else jnp.zeros((r, w), F32) for j in range(len(blocks))], axis=1)
             for i, b in enumerate(blocks)], axis=0)

    def pair(gi, carry):
        khs = [gi * GDN_KH + x for x in range(GDN_KH)]
        kos = [pl.multiple_of(kh * 128, 128) for kh in khs]
        kts = [kt_s[pl.ds(ko, 128), :] for ko in kos]
        s4 = jnp.concatenate([jnp.concatenate([s_ref[2 * kh], s_ref[2 * kh + 1]], axis=1) for kh in khs], axis=0)
        outs = []
        for c in range(2):
            rows = slice(c * CHUNK, (c + 1) * CHUNK)
            kq = bdiag([jnp.concatenate([k_s[rows, pl.ds(ko, 128)], q_s[rows, pl.ds(ko, 128)]], axis=0) for ko in kos])
            kqs = _dotp(kq, s4)
            rhs, eg, gl, gcc = [], [], [], []
            for x, kh in enumerate(khs):
                v2 = v_s[rows, pl.ds(pl.multiple_of(kh * 256, 256), 256)]
                for r in range(2):
                    g = gcb_ref[2 * kh + r][rows, :]
                    e = jnp.exp(g)
                    ks = kqs[x * 128:x * 128 + CHUNK, r * C_DV:(r + 1) * C_DV]
                    rhs.append(bb_ref[2 * kh + r][rows, :] * (v2[:, r * C_DV:(r + 1) * C_DV] - e * ks))
                    gcc.append(g)
                    eg.append(e)
                    gl.append(g[CHUNK - 1:CHUNK, :])
            p0s = [pl.multiple_of((c * C_K_HEADS + kh) * 128, 128) for kh in khs]
            vn = _dotp(bdiag([ti_ref[pl.ds(p0, 128), :] for p0 in p0s]), jnp.concatenate(rhs, axis=0))
            av = _dotp(bdiag([att_ref[pl.ds(p0, 128), :] for p0 in p0s]), vn)
            oc, dvn, glast = [], [], []
            for x, kh in enumerate(khs):
                o2 = []
                for r in range(2):
                    i = 2 * x + r
                    o = eg[i] * kqs[x * 128 + CHUNK:(x + 1) * 128, r * C_DV:(r + 1) * C_DV] + av[i * CHUNK:(i + 1) * CHUNK]
                    o2.append(o * lax.rsqrt(jnp.mean(o * o, axis=-1, keepdims=True) + NORM_EPS) * gnorm_ref[...])
                gate2 = z_ref[rows, pl.ds(pl.multiple_of(C_CONV_CH + kh * 256, 256), 256)]
                oc.append((jnp.concatenate(o2, axis=1) * (gate2 * _sigmoid(gate2))).astype(o_ref.dtype))
                dvn.append(jnp.concatenate([jnp.exp(gl[2 * x + r] - gcc[2 * x + r]) * vn[(2 * x + r) * CHUNK:(2 * x + r + 1) * CHUNK]
                                            for r in range(2)], axis=1))
                glast.append(jnp.broadcast_to(jnp.concatenate([jnp.exp(gl[2 * x]), jnp.exp(gl[2 * x + 1])], axis=1),
                                              (C_DK, 2 * C_DV)))
            outs.append(oc)
            s4 = s4 * jnp.concatenate(glast, axis=0) + _dotp(
                bdiag([kt[:, c * CHUNK:(c + 1) * CHUNK] for kt in kts]), jnp.concatenate(dvn, axis=0))
        for x, kh in enumerate(khs):
            s_ref[2 * kh] = s4[x * C_DK:(x + 1) * C_DK, :C_DV]
            s_ref[2 * kh + 1] = s4[x * C_DK:(x + 1) * C_DK, C_DV:]
            for c in range(2):
                o_ref[c * CHUNK:(c + 1) * CHUNK, pl.ds(pl.multiple_of(kh * 256, 256), 256)] = outs[c][x]
        return carry

    lax.fori_loop(0, C_K_HEADS // GDN_KH, pair, 0)


def gdn_prompt(z, ba, conv_w, a_log, dt_bias, gdn_norm):
    m = z.shape[0]
    nrow = GDN_ROWS
    full = lambda shape: pl.BlockSpec(shape, lambda i: (0,) * len(shape))
    mat = lambda: pltpu.VMEM((GDN_NMAT * CHUNK, 128), F32)
    return pl.pallas_call(
        _gdn_kernel,
        grid=(m // nrow,),
        in_specs=[pl.BlockSpec((nrow, C_CONV_CH + C_VAL), lambda i: (i, 0)),
                  pl.BlockSpec((nrow, 2 * C_V_HEADS), lambda i: (i, 0)),
                  pl.BlockSpec((2 * C_V_HEADS, nrow), lambda i: (0, i)),
                  full((C_CONV, C_CONV_CH)), full((1, C_V_HEADS)), full((1, C_V_HEADS)),
                  full((C_V_HEADS, 1)), full((C_V_HEADS, 1)), full((1, C_DV))],
        out_specs=[pl.BlockSpec((nrow, C_VAL), lambda i: (i, 0)),
                   full((C_V_HEADS, C_DK, C_DV))],
        out_shape=[jax.ShapeDtypeStruct((m, C_VAL), BF16),
                   jax.ShapeDtypeStruct((C_V_HEADS, C_DK, C_DV), F32)],
        scratch_shapes=[pltpu.VMEM((8, C_CONV_CH), F32),
                        pltpu.VMEM((nrow, C_KEY), F32), pltpu.VMEM((nrow, C_KEY), F32), pltpu.VMEM((C_KEY, nrow), F32),
                        pltpu.VMEM((nrow, C_VAL), F32),
                        pltpu.VMEM((C_V_HEADS, nrow, 128), F32), pltpu.VMEM((C_V_HEADS, nrow, 128), F32),
                        pltpu.VMEM((C_V_HEADS, nrow), F32),
                        mat(), mat(), mat(), mat(), mat()],
        compiler_params=_cparams("arbitrary"),
        name="gdn_prompt",
    )(z, ba, ba.T, conv_w, a_log.reshape(1, -1), dt_bias.reshape(1, -1),
      a_log.reshape(-1, 1), dt_bias.reshape(-1, 1), gdn_norm.reshape(1, -1))


RW_CHUNKS = 5
RW_ROWS = RW_CHUNKS * CHUNK
RW_NMAT = RW_CHUNKS * B_HEADS


def _rwkv_kernel(z_ref, mu_ref, w0_ref, wup_ref, a0_ref, aup_ref, gup_ref, kk_ref, ka_ref, rk_ref, lng_ref, lnb_ref,
                 y_ref, s_ref,
                 halo_ref, at_s, bt_s, kt_s, rt_s, v_s, g_s, bon_s, pc_s,
                 a_ref, att_ref, xt_ref, ti_ref, lak_ref, arbk_ref):
    step = pl.program_id(0)
    nrow = RW_ROWS

    @pl.when(step == 0)
    def _():
        s_ref[...] = jnp.zeros_like(s_ref)
        halo_ref[...] = jnp.zeros_like(halo_ref)
        xt_ref[...] = jnp.zeros_like(xt_ref)
        a_ref[...] = jnp.zeros_like(a_ref)

    valid = (step * nrow + lax.broadcasted_iota(jnp.int32, (nrow, 1), 0)) >= SEQ0
    z = z_ref[...]
    zprev = pltpu.roll(jnp.concatenate([halo_ref[...], z], axis=0), 1, 0)[8:]
    halo_ref[...] = z[nrow - 8:nrow]
    zs = z + mu_ref[...] * (zprev - z)
    o1, o2, o3 = B_WIDTH, 2 * B_WIDTH, 3 * B_WIDTH
    o4 = o3 + B_W_RANK
    o5 = o4 + B_A_RANK
    r = zs[:, :o1]
    k = zs[:, o1:o2]
    v = jnp.where(valid, zs[:, o2:o3], 0.0)
    w = -_softplus(-(w0_ref[...] + _dotp(jnp.tanh(zs[:, o3:o4]), wup_ref[...]))) - 0.5
    ld = jnp.where(valid, -jnp.exp(w), 0.0)
    a = _sigmoid(a0_ref[...] + _dotp(zs[:, o4:o5], aup_ref[...]))
    g_s[...] = _dotp(_sigmoid(zs[:, o5:]), gup_ref[...])
    k2 = jnp.where(valid, k * (1.0 + (a - 1.0) * ka_ref[...]), 0.0)
    bon_s[...] = r * k2 * rk_ref[...]
    v_s[...] = v
    kkraw = k * kk_ref[...]
    causal, strict = _tri_masks()
    tri = causal.astype(F32)
    cum = jnp.concatenate(
        [lax.dot_general(tri, ld[c * CHUNK:(c + 1) * CHUNK], (((1,), (0,)), ((), ())), precision=HIGHEST,
                         preferred_element_type=F32) for c in range(RW_CHUNKS)], axis=0)
    for c in range(RW_CHUNKS):
        pc_s[c * 8:(c + 1) * 8, :] = jnp.broadcast_to(jnp.exp(cum[(c + 1) * CHUNK - 1:(c + 1) * CHUNK]), (8, B_WIDTH))
    em = jnp.exp(-cum)
    kt_s[...] = k2 * em
    rt_s[...] = r * jnp.exp(cum)
    ea = jnp.exp(cum - ld)
    for h in range(B_HEADS):
        sl = slice(h * B_DH, (h + 1) * B_DH)
        kh = kkraw[:, sl]
        nrm = jnp.maximum(jnp.sqrt(jnp.sum(kh * kh, axis=-1, keepdims=True)), 1e-12)
        kkn = jnp.where(valid, kh / nrm, 0.0)
        at_s[:, sl] = -kkn * ea[:, sl]
        bt_s[:, sl] = kkn * a[:, sl] * em[:, sl]

    def operands(po, c, rr):
        rows = slice(c * CHUNK, (c + 1) * CHUNK)
        sl = slice(rr * B_DH, (rr + 1) * B_DH)
        ar = jnp.concatenate([at_s[rows, pl.ds(po, 128)][:, sl], rt_s[rows, pl.ds(po, 128)][:, sl]], axis=0)
        bk = jnp.concatenate([bt_s[rows, pl.ds(po, 128)][:, sl], kt_s[rows, pl.ds(po, 128)][:, sl]], axis=0)
        return ar, bk

    def build(p, carry):
        po = pl.multiple_of(p * 128, 128)
        for c in range(RW_CHUNKS):
            for rr in range(2):
                ar, bk = operands(po, c, rr)
                gm = _dotp(ar, bk, _NT)
                m0 = pl.multiple_of((c * B_HEADS + 2 * p + rr) * CHUNK, CHUNK)
                pr0 = pl.multiple_of((c * (B_HEADS // 2) + p) * CHUNK, CHUNK)
                a_ref[pl.ds(m0, CHUNK), :CHUNK] = jnp.where(strict, -gm[:CHUNK, :CHUNK], 0.0)
                lak_ref[pl.ds(pr0, CHUNK), rr * CHUNK:(rr + 1) * CHUNK] = jnp.where(strict, gm[:CHUNK, CHUNK:], 0.0)
                arbk_ref[pl.ds(pr0, CHUNK), rr * CHUNK:(rr + 1) * CHUNK] = jnp.where(causal, gm[CHUNK:, :CHUNK], 0.0)
                arbk_ref[pl.ds(pr0, CHUNK), (2 + rr) * CHUNK:(3 + rr) * CHUNK] = jnp.where(causal, gm[CHUNK:, CHUNK:], 0.0)
        return carry

    lax.fori_loop(0, B_HEADS // 2, build, 0)
    _invert_unit_lower_batch(a_ref, att_ref, xt_ref, ti_ref, RW_NMAT, pair_blockdiag=True)

    same_head = (lax.broadcasted_iota(jnp.int32, (128, 128), 0) // B_DH
                 == lax.broadcasted_iota(jnp.int32, (128, 128), 1) // B_DH)

    def bd2(x):
        return jnp.where(same_head, jnp.concatenate([x, x], axis=0), 0.0)

    def pair(p, carry):
        po = pl.multiple_of(p * 128, 128)
        zero = jnp.zeros((B_DH, B_DH), F32)
        sbd = jnp.concatenate([jnp.concatenate([s_ref[2 * p], zero], axis=1),
                               jnp.concatenate([zero, s_ref[2 * p + 1]], axis=1)], axis=0)
        ys = []
        for c in range(RW_CHUNKS):
            rows = slice(c * CHUNK, (c + 1) * CHUNK)
            win = pl.ds(po, 128)
            vp = v_s[rows, win]
            ar = jnp.concatenate([at_s[rows, win], rt_s[rows, win]], axis=0)
            bk = jnp.concatenate([bt_s[rows, win], kt_s[rows, win]], axis=0)
            pc = pc_s[c * 8:c * 8 + 1, win]
            m0 = pl.multiple_of((c * B_HEADS + 2 * p) * CHUNK, CHUNK)
            pr0 = pl.multiple_of((c * (B_HEADS // 2) + p) * CHUNK, CHUNK)
            ti2 = ti_ref[pl.ds(m0, CHUNK), :] + ti_ref[pl.ds(m0 + CHUNK, CHUNK), :]
            ars = _dotp(ar, sbd, _NT)
            bdv = bd2(vp)
            u = _dotp(ti2, bd2(ars[:CHUNK] + _dotp(lak_ref[pl.ds(pr0, CHUNK), :], bdv)))
            y = ars[CHUNK:] + _dotp(arbk_ref[pl.ds(pr0, CHUNK), :], jnp.concatenate([bd2(u), bdv], axis=0))
            sbd = sbd * pc + jnp.where(same_head, _dotp(jnp.concatenate([u, vp], axis=0), bk * pc, _TN), 0.0)
            gp = g_s[rows, win]
            bp = bon_s[rows, win]
            lg = lng_ref[:, win]
            lb = lnb_ref[:, win]
            outs = []
            for rr in range(2):
                sl = slice(rr * B_DH, (rr + 1) * B_DH)
                yh = y[:, sl]
                mean = jnp.mean(yh, axis=-1, keepdims=True)
                var = jnp.mean(jnp.square(yh - mean), axis=-1, keepdims=True)
                yn = (yh - mean) * lax.rsqrt(var + B_GN_EPS) * lg[:, sl] + lb[:, sl]
                bonus = jnp.sum(bp[:, sl], axis=-1, keepdims=True) * vp[:, sl]
                outs.append((yn + bonus) * gp[:, sl])
            ys.append(jnp.concatenate(outs, axis=1).astype(y_ref.dtype))
        s_ref[2 * p] = sbd[:B_DH, :B_DH]
        s_ref[2 * p + 1] = sbd[B_DH:, B_DH:]
        for c in range(RW_CHUNKS):
            y_ref[c * CHUNK:(c + 1) * CHUNK, pl.ds(po, 128)] = ys[c]
        return carry

    lax.fori_loop(0, B_HEADS // 2, pair, 0)


def rwkv_prompt(zb, mu, w0, w_up, a0, a_up, g_up, k_k, k_a, r_k, ln_g, ln_b):
    m = zb.shape[0]
    nrow = RW_ROWS
    full = lambda shape: pl.BlockSpec(shape, lambda i: (0,) * len(shape))
    vec = lambda x: x.reshape(1, -1)
    wide = lambda: pltpu.VMEM((nrow, B_WIDTH), F32)
    mat = lambda rows: pltpu.VMEM((rows, 128), F32)
    return pl.pallas_call(
        _rwkv_kernel,
        grid=(m // nrow,),
        in_specs=[pl.BlockSpec((nrow, B_PROJ), lambda i: (i, 0)),
                  full((1, B_PROJ)), full((1, B_WIDTH)), full((B_W_RANK, B_WIDTH)), full((1, B_WIDTH)),
                  full((B_A_RANK, B_WIDTH)), full((B_G_RANK, B_WIDTH)), full((1, B_WIDTH)), full((1, B_WIDTH)),
                  full((1, B_WIDTH)), full((1, B_WIDTH)), full((1, B_WIDTH))],
        out_specs=[pl.BlockSpec((nrow, B_WIDTH), lambda i: (i, 0)), full((B_HEADS, B_DH, B_DH))],
        out_shape=[jax.ShapeDtypeStruct((m, B_WIDTH), BF16), jax.ShapeDtypeStruct((B_HEADS, B_DH, B_DH), F32)],
        scratch_shapes=[pltpu.VMEM((8, B_PROJ), F32),
                        wide(), wide(), wide(), wide(), wide(), wide(), wide(),
                        pltpu.VMEM((8 * RW_CHUNKS, B_WIDTH), F32),
                        mat(RW_NMAT * CHUNK), mat(CHUNK * CHUNK), mat(CHUNK * CHUNK), mat(RW_NMAT * CHUNK),
                        mat(RW_NMAT // 2 * CHUNK), pltpu.VMEM((RW_NMAT // 2 * CHUNK, 256), F32)],
        compiler_params=_cparams("arbitrary"),
        name="rwkv_prompt",
    )(zb, vec(mu), vec(w0), w_up, vec(a0), a_up, g_up, vec(k_k), vec(k_a), vec(r_k), vec(ln_g), vec(ln_b))


def lambda_init(layer):
    return 0.8 - 0.6 * math.exp(-0.3 * layer)


def _row_to_col(row):
    n = row.shape[1]
    eye = lax.broadcasted_iota(jnp.int32, (n, n), 0) == lax.broadcasted_iota(jnp.int32, (n, n), 1)
    return jnp.sum(jnp.where(eye, row, 0.0), axis=1, keepdims=True)


def _col_to_row(col):
    n = col.shape[0]
    eye = lax.broadcasted_iota(jnp.int32, (n, n), 0) == lax.broadcasted_iota(jnp.int32, (n, n), 1)
    return jnp.sum(jnp.where(eye, col, 0.0), axis=0, keepdims=True)


def _gdn_step_kernel(z_ref, ba_ref, buf_ref, s_ref, convw_ref, alog_ref, dtb_ref, gnorm_ref,
                     o_ref, so_ref, bufo_ref):
    mixed = z_ref[0][:, :C_CONV_CH]
    buf = buf_ref[0]
    w = convw_ref[...]
    conv = jnp.sum(buf * w[:C_CONV - 1], axis=0, keepdims=True) + mixed * w[C_CONV - 1:]
    act = conv * _sigmoid(conv)
    bufo_ref[0] = jnp.concatenate([buf[1:], mixed], axis=0)
    ba = ba_ref[0]
    beta = _sigmoid(ba[:, :C_V_HEADS])
    eg = jnp.exp(-jnp.exp(alog_ref[...]) * _softplus(ba[:, C_V_HEADS:] + dtb_ref[...]))
    l2n = lambda t: t * lax.rsqrt(jnp.sum(t * t, axis=-1, keepdims=True) + 1e-6)
    outs = []
    for h in range(C_V_HEADS):
        kh = h // (C_V_HEADS // C_K_HEADS)
        q = _row_to_col(l2n(act[:, kh * C_DK:(kh + 1) * C_DK]) * (C_DK ** -0.5))
        k = _row_to_col(l2n(act[:, C_KEY + kh * C_DK:C_KEY + (kh + 1) * C_DK]))
        v = act[:, 2 * C_KEY + h * C_DV:2 * C_KEY + (h + 1) * C_DV]
        s = s_ref[0, h] * eg[:, h:h + 1]
        kv = jnp.sum(s * k, axis=0, keepdims=True)
        s = s + k * ((v - kv) * beta[:, h:h + 1])
        so_ref[0, h] = s
        o = jnp.sum(s * q, axis=0, keepdims=True)
        gate = z_ref[0][:, C_CONV_CH + h * C_DV:C_CONV_CH + (h + 1) * C_DV]
        on = o * lax.rsqrt(jnp.mean(o * o, axis=-1, keepdims=True) + NORM_EPS) * gnorm_ref[...]
        outs.append(on * (gate * _sigmoid(gate)))
    o_ref[0] = jnp.concatenate(outs, axis=1).astype(o_ref.dtype)


def gdn_step(z, ba, buf, s0, conv_w, a_log, dt_bias, gnorm):
    bd = z.shape[0]
    full = lambda shape: pl.BlockSpec(shape, lambda b: (0,) * len(shape))
    per = lambda *shape: pl.BlockSpec((1,) + shape, lambda b: (b,) + (0,) * len(shape))
    return pl.pallas_call(
        _gdn_step_kernel,
        grid=(bd,),
        in_specs=[per(1, C_CONV_CH + C_VAL), per(1, 2 * C_V_HEADS), per(C_CONV - 1, C_CONV_CH),
                  per(C_V_HEADS, C_DK, C_DV), full((C_CONV, C_CONV_CH)), full((1, C_V_HEADS)),
                  full((1, C_V_HEADS)), full((1, C_DV))],
        out_specs=[per(1, C_VAL), per(C_V_HEADS, C_DK, C_DV), per(C_CONV - 1, C_CONV_CH)],
        out_shape=[jax.ShapeDtypeStruct((bd, 1, C_VAL), BF16),
                   jax.ShapeDtypeStruct((bd, C_V_HEADS, C_DK, C_DV), F32),
                   jax.ShapeDtypeStruct((bd, C_CONV - 1, C_CONV_CH), F32)],
        compiler_params=_cparams("parallel"),
        name="gdn_step",
    )(z.reshape(bd, 1, -1), ba.reshape(bd, 1, -1), buf, s0, conv_w, a_log.reshape(1, -1),
      dt_bias.reshape(1, -1), gnorm.reshape(1, -1))


def _rwkv_step_kernel(z_ref, prev_ref, s_ref, mu_ref, w0_ref, wup_ref, a0_ref, aup_ref, gup_ref, kk_ref, ka_ref,
                      rk_ref, lng_ref, lnb_ref, y_ref, so_ref, r_s, k_s, v_s, kk_s, a_s, w_s):
    z = z_ref[...]
    zs = z + mu_ref[...] * (prev_ref[...] - z)
    o1, o2, o3 = B_WIDTH, 2 * B_WIDTH, 3 * B_WIDTH
    o4 = o3 + B_W_RANK
    o5 = o4 + B_A_RANK
    k = zs[:, o1:o2]
    w = -_softplus(-(w0_ref[...] + _dotp(jnp.tanh(zs[:, o3:o4]), wup_ref[...]))) - 0.5
    a = _sigmoid(a0_ref[...] + _dotp(zs[:, o4:o5], aup_ref[...]))
    g = _dotp(_sigmoid(zs[:, o5:]), gup_ref[...])
    r_s[...] = zs[:, :o1]
    k_s[...] = k * (1.0 + (a - 1.0) * ka_ref[...])
    v_s[...] = zs[:, o2:o3]
    kk_s[...] = k * kk_ref[...]
    a_s[...] = a
    w_s[...] = jnp.exp(-jnp.exp(w))

    def seq(b, carry):
        row = lambda ref: ref[pl.ds(b, 1), :]
        r, k2, v, kkraw, ab, dec = row(r_s), row(k_s), row(v_s), row(kk_s), row(a_s), row(w_s)
        ys = []
        for h in range(B_HEADS):
            sl = slice(h * B_DH, (h + 1) * B_DH)
            kkh = kkraw[:, sl]
            kkh = kkh / jnp.maximum(jnp.sqrt(jnp.sum(kkh * kkh, axis=-1, keepdims=True)), 1e-12)
            s = s_ref[b, h]
            sa = jnp.sum(s * (-kkh), axis=1, keepdims=True)
            s = s * dec[:, sl] + sa * (kkh * ab[:, sl]) + _row_to_col(v[:, sl]) * k2[:, sl]
            so_ref[b, h] = s
            y = _col_to_row(jnp.sum(s * r[:, sl], axis=1, keepdims=True))
            mean = jnp.mean(y, axis=-1, keepdims=True)
            var = jnp.mean(jnp.square(y - mean), axis=-1, keepdims=True)
            ys.append((y - mean) * lax.rsqrt(var + B_GN_EPS) * lng_ref[:, sl] + lnb_ref[:, sl]
                      + jnp.sum(r[:, sl] * k2[:, sl] * rk_ref[:, sl], axis=-1, keepdims=True) * v[:, sl])
        y_ref[pl.ds(b, 1), :] = jnp.concatenate(ys, axis=1)
        return carry

    lax.fori_loop(0, z.shape[0], seq, 0)
    y_ref[...] = y_ref[...] * g


def rwkv_step(zb, prev, s0, mu, w0, w_up, a0, a_up, g_up, k_k, k_a, r_k, ln_g, ln_b):
    bd = zb.shape[0]
    vec = lambda x: x.reshape(1, -1)
    wide = lambda: pltpu.VMEM((bd, B_WIDTH), F32)
    return pl.pallas_call(
        _rwkv_step_kernel,
        out_shape=[jax.ShapeDtypeStruct((bd, B_WIDTH), F32),
                   jax.ShapeDtypeStruct((bd, B_HEADS, B_DH, B_DH), F32)],
        scratch_shapes=[wide() for _ in range(6)],
        compiler_params=pltpu.CompilerParams(vmem_limit_bytes=VMEM_LIMIT),
        name="rwkv_step",
    )(zb, prev, s0, vec(mu), vec(w0), w_up, vec(a0), a_up, g_up, vec(k_k), vec(k_a), vec(r_k), vec(ln_g), vec(ln_b))


def kernel(x_prompt, x_sample, cache_k, cache_v, page_table, state_wkv, state_shift, state_gdn, state_conv, meta, norm_mix, norm_ffn, norm_final, w_in_even, w_out_even, lam_q1, lam_k1, lam_q2, lam_k2, subln, rw_mu, rw_w0, rw_w_up, rw_a0, rw_a_up, rw_g_up, rw_k_k, rw_k_a, rw_r_k, rw_ln_g, rw_ln_b, w_in_odd, conv_w, a_log, dt_bias, gdn_norm, w_out_odd, peer_wq, peer_keys, peer_u, peer_v):
    x = jnp.concatenate([x_sample.reshape(DEC_BATCH, D_MODEL),
                         jnp.zeros((SEQ0 - DEC_BATCH, D_MODEL), F32),
                         meta.astype(F32), x_prompt.reshape(SEQ, D_MODEL)], axis=0)
    past_len = page_table.shape[1] * PAGE_SIZE
    pos = jnp.concatenate([jnp.full((DEC_BATCH,), past_len, jnp.int32),
                           jnp.zeros((SEQ0 - DEC_BATCH,), jnp.int32),
                           jnp.arange(T_PROMPT, dtype=jnp.int32)])
    cos, sin = _rope_tables(pos)
    nd = DEC_BATCH

    def peer(xin, layer):
        hn = rmsnorm_rows(xin, norm_ffn[layer])
        qp = matmul_cols(hn, peer_wq[layer], 0, P_HEADS * P_DQ, tn=1024)
        c1, e2, thr = peer_select(qp, peer_keys, layer)
        return peer_dense(hn, c1, e2, thr, peer_u, peer_v, layer)

    h = rmsnorm_rows(x, norm_mix[0])
    z_att = matmul_cols(h, w_in_even[0], 0, 3 * A_WIDTH, tn=1024)
    zb = matmul_cols(h, w_in_even[0], 3 * A_WIDTH, B_PROJ, tn=256)
    qb, kf, kb, vb = rope_qkv(z_att, cos, sin)
    lamvec = jnp.stack([lam_q1[0], lam_k1[0], lam_q2[0], lam_k2[0]])
    att = attn_prompt(qb, kb, vb, lamvec, subln[0], lambda_init(0))
    rw, wkv_p = rwkv_prompt(zb, rw_mu[0], rw_w0[0], rw_w_up[0], rw_a0[0], rw_a_up[0], rw_g_up[0],
                            rw_k_k[0], rw_k_a[0], rw_r_k[0], rw_ln_g[0], rw_ln_b[0])
    v_f = z_att[:, 2 * A_WIDTH:]
    per_map = lambda t: jnp.repeat(t[:nd].reshape(nd, A_HEADS, A_VD), 2, axis=1)
    own_map = (jnp.arange(A_VD, dtype=jnp.int32)[None, :] // A_DH) == (jnp.arange(2 * A_HEADS, dtype=jnp.int32)[:, None] % 2)
    att_d = attn_decode(jnp.where(own_map[None], per_map(qb), 0), per_map(kf), per_map(v_f),
                        cache_k, cache_v, page_table, lamvec, subln[0], lambda_init(0)).reshape(nd, A_WIDTH)
    rw_d, wkv_s = rwkv_step(zb[:nd], state_shift[0], state_wkv[0], rw_mu[0], rw_w0[0], rw_w_up[0], rw_a0[0],
                            rw_a_up[0], rw_g_up[0], rw_k_k[0], rw_k_a[0], rw_r_k[0], rw_ln_g[0], rw_ln_b[0])
    mix = jnp.concatenate([att.at[:nd].set(att_d.astype(BF16)), rw.at[:nd].set(rw_d.astype(BF16))], axis=1)
    x = matmul_cols(mix, w_out_even[0], 0, D_MODEL, tn=1024, residual=x)

    x, h = rmsnorm_rows(x, norm_mix[1], add=peer(x, 0), with_sum=True)
    z1 = matmul_cols(h, w_in_odd[0], 0, C_CONV_CH + C_VAL, tn=1024)
    ba = matmul_cols(h, w_in_odd[0][:, C_CONV_CH + C_VAL:], 0, 2 * C_V_HEADS, tn=2 * C_V_HEADS)
    o, gdn_p = gdn_prompt(z1, ba, conv_w[0], a_log[0], dt_bias[0], gdn_norm[0])
    o_d, gdn_s, conv_s = gdn_step(z1[:nd], ba[:nd], state_conv[0], state_gdn[0], conv_w[0], a_log[0],
                                  dt_bias[0], gdn_norm[0])
    x = matmul_cols(o.at[:nd].set(o_d.reshape(nd, C_VAL)), w_out_odd[0], 0, D_MODEL, tn=512, residual=x)
    xf = rmsnorm_rows(x, norm_final, add=peer(x, 1), out_dtype=F32)

    y_prompt = xf[SEQ0 + N_META:].reshape(1, SEQ, D_MODEL)
    y_sample = xf[:nd].reshape(nd, 1, D_MODEL)
    k_p = kf[SEQ0:].reshape(1, 1, T_PROMPT, A_HEADS, 2 * A_DH)
    v_p = v_f[SEQ0:].reshape(1, 1, T_PROMPT, A_HEADS, A_VD)
    k_s = kf[:nd].reshape(1, nd, 1, A_HEADS, 2 * A_DH)
    v_s = v_f[:nd].reshape(1, nd, 1, A_HEADS, A_VD)
    shift_p = zb[R_ROWS - 1:].reshape(1, 1, B_PROJ)
    shift_s = zb[:nd].reshape(1, nd, B_PROJ)
    conv_p = z1[R_ROWS - (C_CONV - 1):, :C_CONV_CH].reshape(1, 1, C_CONV - 1, C_CONV_CH)
    return (y_prompt, y_sample, k_p, v_p, k_s, v_s,
            wkv_p.reshape(1, 1, B_HEADS, B_DH, B_DH), wkv_s.reshape(1, nd, B_HEADS, B_DH, B_DH),
            shift_p, shift_s,
            gdn_p.reshape(1, 1, C_V_HEADS, C_DK, C_DV), gdn_s.reshape(1, nd, C_V_HEADS, C_DK, C_DV),
            conv_p, conv_s.reshape(1, nd, C_CONV - 1, C_CONV_CH))
```

```python
import functools
import math

import jax
import jax.numpy as jnp
from jax import lax
from jax.experimental import pallas as pl
from jax.experimental.pallas import tpu as pltpu

D_MODEL = 2048
SEQ = 8192
DEC_BATCH = 32
N_META = 16
PAGE_SIZE = 128
NORM_EPS = 1e-6
SUBLN_EPS = 1e-5
ROPE_THETA = 10000.0
NEG_INF = -1e30

A_HEADS = 8
A_DH = 64
A_VD = 128
A_WIDTH = 1024

B_HEADS = 16
B_DH = 64
B_WIDTH = 1024
B_W_RANK = 64
B_A_RANK = 64
B_G_RANK = 128
B_PROJ = 3328
B_GN_EPS = 64e-5

C_K_HEADS = 16
C_V_HEADS = 32
C_DK = 128
C_DV = 128
C_KEY = 2048
C_VAL = 4096
C_CONV = 4
C_CONV_CH = 8192

P_HEADS = 8
P_NKEYS = 128
P_TOPK = 16
P_DQ = 256

T_PROMPT = N_META + SEQ
R_ROWS = 8320
SEQ0 = R_ROWS - T_PROMPT
CHUNK = 64

VMEM_LIMIT = 56 * 1024 * 1024

F32 = jnp.float32
BF16 = jnp.bfloat16
HIGHEST = lax.Precision.HIGHEST


def _cparams(*sem):
    return pltpu.CompilerParams(dimension_semantics=sem, vmem_limit_bytes=VMEM_LIMIT)


def _rmsnorm_kernel(*refs, has_add, with_sum):
    x = refs[0][...]
    if has_add:
        x = x + refs[1][...]
    g_ref = refs[1 + has_add]
    outs = refs[2 + has_add:]
    if with_sum:
        outs[0][...] = x
    y = x * lax.rsqrt(jnp.mean(x * x, axis=-1, keepdims=True) + NORM_EPS)
    outs[-1][...] = (y * g_ref[...]).astype(outs[-1].dtype)


def rmsnorm_rows(x, g, add=None, with_sum=False, out_dtype=BF16, tm=640):
    m, d = x.shape
    row = pl.BlockSpec((tm, d), lambda i: (i, 0))
    ins = [x] + ([add] if add is not None else [])
    out_shape = [jax.ShapeDtypeStruct((m, d), out_dtype)]
    if with_sum:
        out_shape.insert(0, jax.ShapeDtypeStruct((m, d), F32))
    res = pl.pallas_call(
        functools.partial(_rmsnorm_kernel, has_add=add is not None, with_sum=with_sum),
        grid=(m // tm,),
        in_specs=[row] * len(ins) + [pl.BlockSpec((1, d), lambda i: (0, 0))],
        out_specs=[row] * len(out_shape),
        out_shape=out_shape,
        compiler_params=_cparams("parallel"),
        name="rmsnorm_rows",
    )(*ins, g.reshape(1, d))
    return res if with_sum else res[0]


def _matmul_kernel(a_ref, w_ref, *rest, has_res):
    if has_res:
        r_ref, o_ref, wb_ref = rest
    else:
        o_ref, wb_ref = rest

    @pl.when(pl.program_id(1) == 0)
    def _():
        wb_ref[...] = w_ref[...].astype(BF16)

    acc = jnp.dot(a_ref[...], wb_ref[...], preferred_element_type=F32)
    if has_res:
        acc = acc + r_ref[...]
    o_ref[...] = acc


def matmul_cols(a, w, col0, n, tn, tm=640, residual=None):
    m, k = a.shape
    assert w.shape[0] == k and n % tn == 0 and col0 % tn == 0 and m % tm == 0
    cb0 = col0 // tn
    in_specs = [pl.BlockSpec((tm, k), lambda j, i: (i, 0)),
                pl.BlockSpec((k, tn), lambda j, i: (0, cb0 + j))]
    args = [a, w]
    if residual is not None:
        in_specs.append(pl.BlockSpec((tm, tn), lambda j, i: (i, j)))
        args.append(residual)
    return pl.pallas_call(
        functools.partial(_matmul_kernel, has_res=residual is not None),
        grid=(n // tn, m // tm),
        in_specs=in_specs,
        out_specs=pl.BlockSpec((tm, tn), lambda j, i: (i, j)),
        out_shape=jax.ShapeDtypeStruct((m, n), F32),
        scratch_shapes=[pltpu.VMEM((k, tn), BF16)],
        compiler_params=_cparams("parallel", "arbitrary"),
        name="matmul_cols",
    )(*args)


def _rope_tables(pos):
    half = A_DH // 2
    inv_freq = ROPE_THETA ** (-jnp.arange(half, dtype=F32) / half)
    ang = pos.astype(F32)[:, None] * inv_freq[None, :]
    cos = jnp.tile(jnp.cos(ang), (1, 4))
    sin = jnp.sin(ang)
    sin = jnp.tile(jnp.concatenate([-sin, sin], axis=1), (1, 2))
    return cos, sin


def _rope_kernel(z_ref, cos_ref, sin_ref, qb_ref, kf_ref, kb_ref, vb_ref):
    cos = cos_ref[...]
    sin = sin_ref[...]
    lane = lax.broadcasted_iota(jnp.int32, cos.shape, 1)
    first_half = (lane % A_DH) < (A_DH // 2)

    def rope(x):
        partner = jnp.where(first_half, pltpu.roll(x, 128 - A_DH // 2, 1), pltpu.roll(x, A_DH // 2, 1))
        return x * cos + partner * sin

    for h in range(A_HEADS):
        sl = slice(h * 128, (h + 1) * 128)
        q = rope(z_ref[:, sl])
        qb_ref[:, sl] = (q * (A_DH ** -0.5)).astype(BF16)
        k = rope(z_ref[:, A_WIDTH + h * 128:A_WIDTH + (h + 1) * 128])
        kf_ref[:, sl] = k
        kb_ref[:, sl] = k.astype(BF16)
    vb_ref[...] = z_ref[:, 2 * A_WIDTH:3 * A_WIDTH].astype(BF16)


def rope_qkv(z_att, cos, sin, tm=640):
    m = z_att.shape[0]
    row = lambda w: pl.BlockSpec((tm, w), lambda i: (i, 0))
    return pl.pallas_call(
        _rope_kernel,
        grid=(m // tm,),
        in_specs=[row(3 * A_WIDTH), row(128), row(128)],
        out_specs=[row(A_WIDTH)] * 4,
        out_shape=[jax.ShapeDtypeStruct((m, A_WIDTH), BF16), jax.ShapeDtypeStruct((m, A_WIDTH), F32),
                   jax.ShapeDtypeStruct((m, A_WIDTH), BF16), jax.ShapeDtypeStruct((m, A_WIDTH), BF16)],
        compiler_params=_cparams("parallel"),
        name="rope_qkv",
    )(z_att, cos, sin)


def _lambda(lam_ref, lam_init):
    l1 = jnp.exp(jnp.sum(lam_ref[0:1, :] * lam_ref[1:2, :], axis=-1, keepdims=True))
    l2 = jnp.exp(jnp.sum(lam_ref[2:3, :] * lam_ref[3:4, :], axis=-1, keepdims=True))
    return l1 - l2 + lam_init


def _subln(o, subln_ref, lam_init):
    y = o * lax.rsqrt(jnp.mean(o * o, axis=-1, keepdims=True) + SUBLN_EPS)
    return y * subln_ref[...] * (1.0 - lam_init)


def _attn_prompt_kernel(q_ref, k_ref, v_ref, lam_ref, subln_ref, o_ref, *, tq, lam_init):
    qb = pl.program_id(1)
    q = q_ref[...]
    qidx = qb * tq + lax.broadcasted_iota(jnp.int32, (tq, tq), 0)
    kiota = lax.broadcasted_iota(jnp.int32, (tq, tq), 1)
    dn = (((1,), (1,)), ((), ()))

    def body(kb, carry, masked):
        off = pl.multiple_of(kb * tq, tq)
        k = k_ref[pl.ds(off, tq), :]
        v = v_ref[pl.ds(off, tq), :]
        if masked:
            kidx = off + kiota
            mask = (kidx <= qidx) & (kidx >= SEQ0)
        out = []
        for c in range(2):
            m, l, acc = carry[c]
            s = lax.dot_general(q[:, c * A_DH:(c + 1) * A_DH], k[:, c * A_DH:(c + 1) * A_DH], dn,
                                preferred_element_type=F32)
            if masked:
                s = jnp.where(mask, s, NEG_INF)
            m_new = jnp.maximum(m, jnp.max(s, axis=-1, keepdims=True))
            corr = jnp.exp(m - m_new)
            p = jnp.exp(s - m_new)
            l = l * corr + jnp.sum(p, axis=-1, keepdims=True)
            acc = acc * corr + jnp.dot(p.astype(BF16), v, preferred_element_type=F32)
            out.append((m_new, l, acc))
        return tuple(out)

    init = tuple((jnp.full((tq, 1), NEG_INF, F32), jnp.zeros((tq, 1), F32), jnp.zeros((tq, A_VD), F32))
                 for _ in range(2))
    carry = body(0, init, True)
    carry = lax.fori_loop(1, qb, functools.partial(body, masked=False), carry)
    (_, l1, a1), (_, l2, a2) = lax.cond(qb > 0, lambda c: body(qb, c, True), lambda c: c, carry)
    o = a1 / l1 - _lambda(lam_ref, lam_init) * (a2 / l2)
    o_ref[...] = _subln(o, subln_ref, lam_init).astype(o_ref.dtype)


def attn_prompt(qb, kb, vb, lamvec, subln, lam_init, tq=640):
    m = qb.shape[0]
    return pl.pallas_call(
        functools.partial(_attn_prompt_kernel, tq=tq, lam_init=lam_init),
        grid=(A_HEADS, m // tq),
        in_specs=[pl.BlockSpec((tq, 128), lambda h, i: (i, h)),
                  pl.BlockSpec((m, 128), lambda h, i: (0, h)),
                  pl.BlockSpec((m, 128), lambda h, i: (0, h)),
                  pl.BlockSpec((4, A_DH), lambda h, i: (0, 0)),
                  pl.BlockSpec((1, A_VD), lambda h, i: (0, 0))],
        out_specs=pl.BlockSpec((tq, 128), lambda h, i: (i, h)),
        out_shape=jax.ShapeDtypeStruct((m, A_WIDTH), BF16),
        compiler_params=_cparams("parallel", "arbitrary"),
        name="attn_prompt",
    )(qb, kb, vb, lamvec, subln.reshape(1, A_VD))


PAGES_PER_STEP = 8


def _attn_decode_kernel(pt_ref, q_ref, kn_ref, vn_ref, *rest, lam_init):
    pages = rest[:2 * PAGES_PER_STEP]
    lam_ref, subln_ref, o_ref, m_s, l_s, acc_s = rest[2 * PAGES_PER_STEP:]
    p = pl.program_id(1)

    @pl.when(p == 0)
    def _():
        m_s[...] = jnp.full_like(m_s, NEG_INF)
        l_s[...] = jnp.zeros_like(l_s)
        acc_s[...] = jnp.zeros_like(acc_s)

    nk = PAGE_SIZE * A_HEADS

    def page(ref):
        return ref[0].reshape(nk, A_VD).astype(BF16)

    own_head = (lax.broadcasted_iota(jnp.int32, (2 * A_HEADS, nk), 0) // 2
                == lax.broadcasted_iota(jnp.int32, (2 * A_HEADS, nk), 1) % A_HEADS)
    q = q_ref[0]
    m, l, acc = m_s[...], l_s[...], acc_s[...]
    ss = [jnp.where(own_head, lax.dot_general(q, page(pages[2 * i]), _NT_DIMS, preferred_element_type=F32), NEG_INF)
          for i in range(PAGES_PER_STEP)]
    m_new = m
    for s in ss:
        m_new = jnp.maximum(m_new, jnp.max(s, axis=-1, keepdims=True))
    corr = jnp.exp(m - m_new)
    l = l * corr
    acc = acc * corr
    for i, s in enumerate(ss):
        pr = jnp.exp(s - m_new)
        l = l + jnp.sum(pr, axis=-1, keepdims=True)
        acc = acc + jnp.dot(pr.astype(BF16), page(pages[2 * i + 1]), preferred_element_type=F32)
    m = m_new
    m_s[...], l_s[...], acc_s[...] = m, l, acc

    @pl.when(p == pl.num_programs(1) - 1)
    def _():
        kn = kn_ref[0].astype(BF16).astype(F32)
        vn = vn_ref[0].astype(BF16).astype(F32)
        s = jnp.sum(q.astype(F32) * kn, axis=-1, keepdims=True)
        m_new = jnp.maximum(m, s)
        corr = jnp.exp(m - m_new)
        pr = jnp.exp(s - m_new)
        lf = l * corr + pr
        o = (acc * corr + pr.astype(BF16).astype(F32) * vn) / lf
        lam = _lambda(lam_ref, lam_init)
        outs = [_subln(o[2 * h:2 * h + 1] - lam * o[2 * h + 1:2 * h + 2], subln_ref, lam_init)
                for h in range(A_HEADS)]
        o_ref[0] = jnp.concatenate(outs, axis=1).astype(o_ref.dtype)


def attn_decode(qm, k_new, v_new, cache_k, cache_v, page_table, lamvec, subln, lam_init):
    bd = qm.shape[0]
    n_pages = page_table.shape[1]
    assert n_pages % PAGES_PER_STEP == 0
    page_specs = []
    for i in range(PAGES_PER_STEP):
        idx = lambda b, p, pt, i=i: (0, pt[b, p * PAGES_PER_STEP + i], 0, 0, 0)
        spec = pl.BlockSpec((None, 1, PAGE_SIZE, A_HEADS, 128), idx)
        page_specs += [spec, spec]
    per_seq = lambda r, w: pl.BlockSpec((1, r, w), lambda b, p, pt: (b, 0, 0))
    grid_spec = pltpu.PrefetchScalarGridSpec(
        num_scalar_prefetch=1,
        grid=(bd, n_pages // PAGES_PER_STEP),
        in_specs=[per_seq(2 * A_HEADS, A_VD)] * 3 + page_specs
                 + [pl.BlockSpec((4, A_DH), lambda b, p, pt: (0, 0)), pl.BlockSpec((1, A_VD), lambda b, p, pt: (0, 0))],
        out_specs=per_seq(1, A_WIDTH),
        scratch_shapes=[pltpu.VMEM((2 * A_HEADS, 1), F32), pltpu.VMEM((2 * A_HEADS, 1), F32),
                        pltpu.VMEM((2 * A_HEADS, A_VD), F32)],
    )
    args = [qm, k_new, v_new] + [cache_k, cache_v] * PAGES_PER_STEP + [lamvec, subln.reshape(1, A_VD)]
    return pl.pallas_call(
        functools.partial(_attn_decode_kernel, lam_init=lam_init),
        grid_spec=grid_spec,
        out_shape=jax.ShapeDtypeStruct((bd, 1, A_WIDTH), BF16),
        compiler_params=_cparams("parallel", "arbitrary"),
        name="attn_decode",
    )(page_table, *args)


def _top_values(s, k):
    n = s.shape[0]
    row = lax.broadcasted_iota(jnp.int32, s.shape, 0).astype(F32)
    vals = []
    for _ in range(k):
        m = jnp.max(s, axis=0, keepdims=True)
        first = jnp.min(jnp.where(s == m, row, float(n)), axis=0, keepdims=True)
        s = jnp.where(row == first, -jnp.inf, s)
        vals.append(m)
    return jnp.concatenate(vals, axis=0)


def _peer_select_kernel(q_ref, keys_ref, c1_ref, e2_ref, thr_ref):
    dn = (((1,), (1,)), ((), ()))

    def head(h, carry):
        qo = pl.multiple_of(h * P_DQ, P_DQ)
        s1 = lax.dot_general(keys_ref[0, h], q_ref[:, pl.ds(qo, 128)], dn,
                             precision=HIGHEST, preferred_element_type=F32)
        s2 = lax.dot_general(keys_ref[1, h], q_ref[:, pl.ds(qo + 128, 128)], dn,
                             precision=HIGHEST, preferred_element_type=F32)
        t1 = _top_values(s1, P_TOPK)
        t2 = _top_values(s2, P_TOPK)
        e1 = jnp.exp(s1 - t1[0:1])
        e2 = jnp.exp(s2 - t2[0:1])
        et1 = jnp.exp(t1 - t1[0:1])
        et2 = jnp.exp(t2 - t2[0:1])
        nb = [P_TOPK // (a + 1) for a in range(P_TOPK)]
        cand = jnp.concatenate([et1[a:a + 1] * et2[:nb[a]] for a in range(P_TOPK)], axis=0)
        top = _top_values(cand, P_TOPK)
        inv_z = 1.0 / jnp.sum(top, axis=0, keepdims=True)
        scaled = jnp.concatenate([(et1[a:a + 1] * inv_z) * et2[:nb[a]] for a in range(P_TOPK)], axis=0)
        thr = jnp.min(jnp.where(cand >= top[P_TOPK - 1:P_TOPK], scaled, jnp.inf), axis=0, keepdims=True)
        ro = pl.multiple_of(h * 128, 128)
        c1_ref[pl.ds(ro, 128), :] = e1 * inv_z
        e2_ref[pl.ds(ro, 128), :] = e2
        thr_ref[pl.ds(h, 1), :] = thr
        return carry

    lax.fori_loop(0, P_HEADS, head, 0)


def peer_select(qp, keys, layer, tm=640):
    m = qp.shape[0]
    col = lambda r: pl.BlockSpec((r, tm), lambda i: (0, i))
    return pl.pallas_call(
        _peer_select_kernel,
        grid=(m // tm,),
        in_specs=[pl.BlockSpec((tm, P_HEADS * P_DQ), lambda i: (i, 0)),
                  pl.BlockSpec((None, 2, P_HEADS, P_NKEYS, 128), lambda i: (layer, 0, 0, 0, 0))],
        out_specs=[col(P_HEADS * 128), col(P_HEADS * 128), col(P_HEADS)],
        out_shape=[jax.ShapeDtypeStruct((P_HEADS * 128, m), F32), jax.ShapeDtypeStruct((P_HEADS * 128, m), F32),
                   jax.ShapeDtypeStruct((P_HEADS, m), F32)],
        compiler_params=_cparams("parallel"),
        name="peer_select",
    )(qp, keys)


def _peer_dense_kernel(h_ref, c1_ref, e2_ref, thr_ref, u_ref, v_ref, o_ref, *, te):
    j = pl.program_id(1)

    @pl.when(j == 0)
    def _():
        o_ref[...] = jnp.zeros_like(o_ref)

    sub = 256
    total = None
    for cb in range(te // sub):
        rows = slice(cb * sub, (cb + 1) * sub)
        g = lax.dot_general(u_ref[rows, :].astype(BF16), h_ref[...], _NT_DIMS, preferred_element_type=F32)
        act = 0.5 * g * (1.0 + lax.erf(g * (2.0 ** -0.5)))
        parts = []
        for ib in range(sub // 128):
            i1 = j * (te // 128) + cb * (sub // 128) + ib
            w = jnp.zeros((128, g.shape[1]), F32)
            for h in range(P_HEADS):
                p = e2_ref[h * 128:(h + 1) * 128, :] * c1_ref[pl.ds(h * 128 + i1, 1), :]
                w = w + jnp.where(p >= thr_ref[h:h + 1, :], p, 0.0)
            parts.append(act[ib * 128:(ib + 1) * 128, :] * w)
        a = jnp.concatenate(parts, axis=0).T.astype(BF16)
        y = jnp.dot(a, v_ref[rows, :].astype(BF16), preferred_element_type=F32)
        total = y if total is None else total + y
    o_ref[...] += total


_NT_DIMS = (((1,), (1,)), ((), ()))


def peer_dense(hn, c1, e2, thr, u_tabs, v_tabs, layer, tm=640, te=512):
    m, d = hn.shape
    ne = u_tabs.shape[1]
    col = lambda r: pl.BlockSpec((r, tm), lambda i, j: (0, i))
    tab = pl.BlockSpec((None, te, d), lambda i, j: (layer, j, 0))
    return pl.pallas_call(
        functools.partial(_peer_dense_kernel, te=te),
        grid=(m // tm, ne // te),
        in_specs=[pl.BlockSpec((tm, d), lambda i, j: (i, 0)),
                  col(P_HEADS * 128), col(P_HEADS * 128), col(P_HEADS), tab, tab],
        out_specs=pl.BlockSpec((tm, d), lambda i, j: (i, 0)),
        out_shape=jax.ShapeDtypeStruct((m, d), F32),
        compiler_params=_cparams("parallel", "arbitrary"),
        name="peer_dense",
    )(hn, c1, e2, thr, u_tabs, v_tabs)


SEQ_PASSES = 1


def _dotp(a, b, dims=(((1,), (0,)), ((), ())), passes=None):
    passes = SEQ_PASSES if passes is None else passes
    dot = lambda x, y: lax.dot_general(x, y, dims, preferred_element_type=F32)
    a_hi = a.astype(BF16)
    b_hi = b.astype(BF16)
    if passes == 1:
        return dot(a_hi, b_hi)
    a_lo = (a - a_hi.astype(F32)).astype(BF16)
    b_lo = (b - b_hi.astype(F32)).astype(BF16)
    return dot(a_hi, b_hi) + dot(a_lo, b_hi) + dot(a_hi, b_lo)


_NT = (((1,), (1,)), ((), ()))
_TN = (((0,), (0,)), ((), ()))


def _tri_masks():
    i = lax.broadcasted_iota(jnp.int32, (CHUNK, CHUNK), 0)
    j = lax.broadcasted_iota(jnp.int32, (CHUNK, CHUNK), 1)
    return i >= j, i > j


def _invert_unit_lower_batch(a_ref, at_ref, xt_ref, ti_ref, nmat, pair_blockdiag=False):
    zpad = jnp.zeros((128 - nmat, 128), F32)
    for t in range(CHUNK):
        slab = a_ref[pl.ds(t, nmat, stride=CHUNK), :]
        at_ref[t * CHUNK:(t + 1) * CHUNK, :] = jnp.concatenate([slab, zpad], axis=0).T[:CHUNK, :]
    sub = lax.broadcasted_iota(jnp.int32, (8, 128), 0)
    for tb in range(CHUNK // 8):
        def row(ti, carry, tb=tb):
            t = tb * 8 + ti
            acc = [jnp.zeros((8, 128), F32) for _ in range(tb)] + [(sub == ti).astype(F32)]
            for sb in range(tb + 1):
                for s in range(8):
                    sg = sb * 8 + s
                    a = at_ref[pl.ds(t * CHUNK + sg, 1), :]
                    for cb in range(sb + 1):
                        acc[cb] = acc[cb] - a * xt_ref[sg * CHUNK + cb * 8:sg * CHUNK + cb * 8 + 8, :]
            for cb in range(tb + 1):
                xt_ref[pl.ds(pl.multiple_of(t * CHUNK + cb * 8, 8), 8), :] = acc[cb]
            return carry
        lax.fori_loop(0, 8, row, 0)
    zrow = jnp.zeros((128 - CHUNK, 128), F32)
    odd = lax.broadcasted_iota(jnp.int32, (nmat, 128), 0) % 2 == 1
    for t in range(CHUNK):
        slab = xt_ref[t * CHUNK:(t + 1) * CHUNK, :]
        out = jnp.concatenate([slab, zrow], axis=0).T[:nmat, :]
        if pair_blockdiag:
            out = jnp.where(odd, pltpu.roll(out, CHUNK, 1), out)
        ti_ref[pl.ds(t, nmat, stride=CHUNK), :] = out


def _softplus(x):
    return jnp.maximum(x, 0.0) + jnp.log(1.0 + jnp.exp(-jnp.abs(x)))


def _sigmoid(x):
    return 1.0 / (1.0 + jnp.exp(-x))


GDN_ROWS = 2 * CHUNK
GDN_NMAT = 2 * C_V_HEADS
GDN_KH = 2


def _gdn_kernel(z_ref, ba_ref, bat_ref, convw_ref, alog_ref, dtb_ref, alogt_ref, dtbt_ref, gnorm_ref,
                o_ref, s_ref,
                halo_ref, q_s, k_s, kt_s, v_s, gcb_ref, bb_ref, gct_ref, a_ref, at_ref, xt_ref, ti_ref, att_ref):
    step = pl.program_id(0)
    nrow = GDN_ROWS

    @pl.when(step == 0)
    def _():
        s_ref[...] = jnp.zeros_like(s_ref)
        halo_ref[...] = jnp.zeros_like(halo_ref)
        xt_ref[...] = jnp.zeros_like(xt_ref)
        a_ref[...] = jnp.zeros_like(a_ref)
        att_ref[...] = jnp.zeros_like(att_ref)

    valid = (step * nrow + lax.broadcasted_iota(jnp.int32, (nrow, 1), 0)) >= SEQ0
    valid_t = (step * nrow + lax.broadcasted_iota(jnp.int32, (1, nrow), 1)) >= SEQ0
    beta = jnp.where(valid, _sigmoid(ba_ref[:, :C_V_HEADS]), 0.0)
    g = jnp.where(valid, -jnp.exp(alog_ref[...]) * _softplus(ba_ref[:, C_V_HEADS:] + dtb_ref[...]), 0.0)
    g_t = jnp.where(valid_t, -jnp.exp(alogt_ref[...]) * _softplus(bat_ref[C_V_HEADS:, :] + dtbt_ref[...]), 0.0)
    causal, strict = _tri_masks()
    tri = causal.astype(F32)
    hi = lambda a, b, dims=(((1,), (0,)), ((), ())): lax.dot_general(a, b, dims, precision=HIGHEST,
                                                                      preferred_element_type=F32)
    gc = jnp.concatenate([hi(tri, g[c * CHUNK:(c + 1) * CHUNK]) for c in range(2)], axis=0)
    gct_ref[...] = jnp.concatenate([hi(g_t[:, c * CHUNK:(c + 1) * CHUNK], tri, _NT) for c in range(2)], axis=1)
    for h in range(C_V_HEADS):
        gcb_ref[h] = jnp.broadcast_to(gc[:, h:h + 1], (nrow, 128))
        bb_ref[h] = jnp.broadcast_to(beta[:, h:h + 1], (nrow, 128))

    def conv_act(col0):
        x = z_ref[:, col0:col0 + 128]
        xe = jnp.concatenate([halo_ref[:, col0:col0 + 128], x], axis=0)
        w = convw_ref[:, col0:col0 + 128]
        y = (w[3:4] * x + w[2:3] * pltpu.roll(xe, 1, 0)[8:] + w[1:2] * pltpu.roll(xe, 2, 0)[8:]
             + w[0:1] * pltpu.roll(xe, 3, 0)[8:])
        return y * _sigmoid(y)

    def l2n(x):
        return x * lax.rsqrt(jnp.sum(x * x, axis=-1, keepdims=True) + 1e-6)

    for h in range(C_K_HEADS):
        q_s[:, h * 128:(h + 1) * 128] = l2n(conv_act(h * 128)) * (C_DK ** -0.5)
        kh_tile = l2n(conv_act(C_KEY + h * 128))
        k_s[:, h * 128:(h + 1) * 128] = kh_tile
        kt_s[h * 128:(h + 1) * 128, :] = kh_tile.T
    for h in range(C_V_HEADS):
        v_s[:, h * 128:(h + 1) * 128] = conv_act(2 * C_KEY + h * 128)
    halo_ref[...] = z_ref[nrow - 8:nrow, :C_CONV_CH]

    def build(kh, carry):
        ko = pl.multiple_of(kh * 128, 128)
        kt = kt_s[pl.ds(ko, 128), :]
        for c in range(2):
            rows = slice(c * CHUNK, (c + 1) * CHUNK)
            kq = jnp.concatenate([k_s[rows, pl.ds(ko, 128)], q_s[rows, pl.ds(ko, 128)]], axis=0)
            g2 = _dotp(kq, kt[:, c * CHUNK:(c + 1) * CHUNK])
            p0 = pl.multiple_of((c * C_K_HEADS + kh) * 128, 128)
            for r in range(2):
                vh = 2 * kh + r
                diff = gcb_ref[vh][rows, :CHUNK] - gct_ref[pl.ds(vh, 1), c * CHUNK:(c + 1) * CHUNK]
                dec = jnp.where(causal, jnp.exp(jnp.where(causal, diff, 0.0)), 0.0)
                a_ref[pl.ds(p0 + r * CHUNK, CHUNK), :CHUNK] = jnp.where(
                    strict, bb_ref[vh][rows, :CHUNK] * g2[:CHUNK] * dec, 0.0)
                att_ref[pl.ds(p0 + r * CHUNK, CHUNK), r * CHUNK:(r + 1) * CHUNK] = jnp.where(
                    causal, g2[CHUNK:] * dec, 0.0)
        return carry

    lax.fori_loop(0, C_K_HEADS, build, 0)
    _invert_unit_lower_batch(a_ref, at_ref, xt_ref, ti_ref, GDN_NMAT, pair_blockdiag=True)

    def bdiag(blocks):
        r, w = blocks[0].shape
        return jnp.concatenate(
            [jnp.concatenate([b if j == i else jnp.zeros((r, w), F32) for j in range(len(blocks))], axis=1)
             for i, b in enumerate(blocks)], axis=0)

    def pair(gi, carry):
        khs = [gi * GDN_KH + x for x in range(GDN_KH)]
        kos = [pl.multiple_of(kh * 128, 128) for kh in khs]
        kts = [kt_s[pl.ds(ko, 128), :] for ko in kos]
        s4 = jnp.concatenate([jnp.concatenate([s_ref[2 * kh], s_ref[2 * kh + 1]], axis=1) for kh in khs], axis=0)
        outs = []
        for c in range(2):
            rows = slice(c * CHUNK, (c + 1) * CHUNK)
            kq = bdiag([jnp.concatenate([k_s[rows, pl.ds(ko, 128)], q_s[rows, pl.ds(ko, 128)]], axis=0) for ko in kos])
            kqs = _dotp(kq, s4)
            rhs, eg, gl, gcc = [], [], [], []
            for x, kh in enumerate(khs):
                v2 = v_s[rows, pl.ds(pl.multiple_of(kh * 256, 256), 256)]
                for r in range(2):
                    g = gcb_ref[2 * kh + r][rows, :]
                    e = jnp.exp(g)
                    ks = kqs[x * 128:x * 128 + CHUNK, r * C_DV:(r + 1) * C_DV]
                    rhs.append(bb_ref[2 * kh + r][rows, :] * (v2[:, r * C_DV:(r + 1) * C_DV] - e * ks))
                    gcc.append(g)
                    eg.append(e)
                    gl.append(g[CHUNK - 1:CHUNK, :])
            p0s = [pl.multiple_of((c * C_K_HEADS + kh) * 128, 128) for kh in khs]
            vn = _dotp(bdiag([ti_ref[pl.ds(p0, 128), :] for p0 in p0s]), jnp.concatenate(rhs, axis=0))
            av = _dotp(bdiag([att_ref[pl.ds(p0, 128), :] for p0 in p0s]), vn)
            oc, dvn, glast = [], [], []
            for x, kh in enumerate(khs):
                o2 = []
                for r in range(2):
                    i = 2 * x + r
                    o = eg[i] * kqs[x * 128 + CHUNK:(x + 1) * 128, r * C_DV:(r + 1) * C_DV] + av[i * CHUNK:(i + 1) * CHUNK]
                    o2.append(o * lax.rsqrt(jnp.mean(o * o, axis=-1, keepdims=True) + NORM_EPS) * gnorm_ref[...])
                gate2 = z_ref[rows, pl.ds(pl.multiple_of(C_CONV_CH + kh * 256, 256), 256)]
                oc.append((jnp.concatenate(o2, axis=1) * (gate2 * _sigmoid(gate2))).astype(o_ref.dtype))
                dvn.append(jnp.concatenate([jnp.exp(gl[2 * x + r] - gcc[2 * x + r]) * vn[(2 * x + r) * CHUNK:(2 * x + r + 1) * CHUNK]
                                            for r in range(2)], axis=1))
                glast.append(jnp.broadcast_to(jnp.concatenate([jnp.exp(gl[2 * x]), jnp.exp(gl[2 * x + 1])], axis=1),
                                              (C_DK, 2 * C_DV)))
            outs.append(oc)
            s4 = s4 * jnp.concatenate(glast, axis=0) + _dotp(
                bdiag([kt[:, c * CHUNK:(c + 1) * CHUNK] for kt in kts]), jnp.concatenate(dvn, axis=0))
        for x, kh in enumerate(khs):
            s_ref[2 * kh] = s4[x * C_DK:(x + 1) * C_DK, :C_DV]
            s_ref[2 * kh + 1] = s4[x * C_DK:(x + 1) * C_DK, C_DV:]
            for c in range(2):
                o_ref[c * CHUNK:(c + 1) * CHUNK, pl.ds(pl.multiple_of(kh * 256, 256), 256)] = outs[c][x]
        return carry

    lax.fori_loop(0, C_K_HEADS // GDN_KH, pair, 0)


def gdn_prompt(z, ba, conv_w, a_log, dt_bias, gdn_norm):
    m = z.shape[0]
    nrow = GDN_ROWS
    full = lambda shape: pl.BlockSpec(shape, lambda i: (0,) * len(shape))
    mat = lambda: pltpu.VMEM((GDN_NMAT * CHUNK, 128), F32)
    return pl.pallas_call(
        _gdn_kernel,
        grid=(m // nrow,),
        in_specs=[pl.BlockSpec((nrow, C_CONV_CH + C_VAL), lambda i: (i, 0)),
                  pl.BlockSpec((nrow, 2 * C_V_HEADS), lambda i: (i, 0)),
                  pl.BlockSpec((2 * C_V_HEADS, nrow), lambda i: (0, i)),
                  full((C_CONV, C_CONV_CH)), full((1, C_V_HEADS)), full((1, C_V_HEADS)),
                  full((C_V_HEADS, 1)), full((C_V_HEADS, 1)), full((1, C_DV))],
        out_specs=[pl.BlockSpec((nrow, C_VAL), lambda i: (i, 0)),
                   full((C_V_HEADS, C_DK, C_DV))],
        out_shape=[jax.ShapeDtypeStruct((m, C_VAL), BF16),
                   jax.ShapeDtypeStruct((C_V_HEADS, C_DK, C_DV), F32)],
        scratch_shapes=[pltpu.VMEM((8, C_CONV_CH), F32),
                        pltpu.VMEM((nrow, C_KEY), F32), pltpu.VMEM((nrow, C_KEY), F32), pltpu.VMEM((C_KEY, nrow), F32),
                        pltpu.VMEM((nrow, C_VAL), F32),
                        pltpu.VMEM((C_V_HEADS, nrow, 128), F32), pltpu.VMEM((C_V_HEADS, nrow, 128), F32),
                        pltpu.VMEM((C_V_HEADS, nrow), F32),
                        mat(), mat(), mat(), mat(), mat()],
        compiler_params=_cparams("arbitrary"),
        name="gdn_prompt",
    )(z, ba, ba.T, conv_w, a_log.reshape(1, -1), dt_bias.reshape(1, -1),
      a_log.reshape(-1, 1), dt_bias.reshape(-1, 1), gdn_norm.reshape(1, -1))


RW_CHUNKS = 5
RW_ROWS = RW_CHUNKS * CHUNK
RW_NMAT = RW_CHUNKS * B_HEADS
RW_HEADS = 4


def _rwkv_kernel(z_ref, mu_ref, w0_ref, wup_ref, a0_ref, aup_ref, gup_ref, kk_ref, ka_ref, rk_ref, lng_ref, lnb_ref,
                 y_ref, s_ref,
                 halo_ref, at_s, bt_s, kt_s, rt_s, v_s, g_s, bon_s, pc_s,
                 a_ref, att_ref, xt_ref, ti_ref, lak_ref, arbk_ref):
    step = pl.program_id(0)
    nrow = RW_ROWS

    @pl.when(step == 0)
    def _():
        s_ref[...] = jnp.zeros_like(s_ref)
        halo_ref[...] = jnp.zeros_like(halo_ref)
        xt_ref[...] = jnp.zeros_like(xt_ref)
        a_ref[...] = jnp.zeros_like(a_ref)

    valid = (step * nrow + lax.broadcasted_iota(jnp.int32, (nrow, 1), 0)) >= SEQ0
    z = z_ref[...]
    zprev = pltpu.roll(jnp.concatenate([halo_ref[...], z], axis=0), 1, 0)[8:]
    halo_ref[...] = z[nrow - 8:nrow]
    zs = z + mu_ref[...] * (zprev - z)
    o1, o2, o3 = B_WIDTH, 2 * B_WIDTH, 3 * B_WIDTH
    o4 = o3 + B_W_RANK
    o5 = o4 + B_A_RANK
    r = zs[:, :o1]
    k = zs[:, o1:o2]
    v = jnp.where(valid, zs[:, o2:o3], 0.0)
    w = -_softplus(-(w0_ref[...] + _dotp(jnp.tanh(zs[:, o3:o4]), wup_ref[...]))) - 0.5
    ld = jnp.where(valid, -jnp.exp(w), 0.0)
    a = _sigmoid(a0_ref[...] + _dotp(zs[:, o4:o5], aup_ref[...]))
    g_s[...] = _dotp(_sigmoid(zs[:, o5:]), gup_ref[...])
    k2 = jnp.where(valid, k * (1.0 + (a - 1.0) * ka_ref[...]), 0.0)
    bon_s[...] = r * k2 * rk_ref[...]
    v_s[...] = v
    kkraw = k * kk_ref[...]
    causal, strict = _tri_masks()
    tri = causal.astype(F32)
    cum = jnp.concatenate(
        [lax.dot_general(tri, ld[c * CHUNK:(c + 1) * CHUNK], (((1,), (0,)), ((), ())), precision=HIGHEST,
                         preferred_element_type=F32) for c in range(RW_CHUNKS)], axis=0)
    for c in range(RW_CHUNKS):
        pc_s[c * 8:(c + 1) * 8, :] = jnp.broadcast_to(jnp.exp(cum[(c + 1) * CHUNK - 1:(c + 1) * CHUNK]), (8, B_WIDTH))
    em = jnp.exp(-cum)
    kt_s[...] = k2 * em
    rt_s[...] = r * jnp.exp(cum)
    ea = jnp.exp(cum - ld)
    for h in range(B_HEADS):
        sl = slice(h * B_DH, (h + 1) * B_DH)
        kh = kkraw[:, sl]
        nrm = jnp.maximum(jnp.sqrt(jnp.sum(kh * kh, axis=-1, keepdims=True)), 1e-12)
        kkn = jnp.where(valid, kh / nrm, 0.0)
        at_s[:, sl] = -kkn * ea[:, sl]
        bt_s[:, sl] = kkn * a[:, sl] * em[:, sl]

    def operands(po, c, rr):
        rows = slice(c * CHUNK, (c + 1) * CHUNK)
        sl = slice(rr * B_DH, (rr + 1) * B_DH)
        ar = jnp.concatenate([at_s[rows, pl.ds(po, 128)][:, sl], rt_s[rows, pl.ds(po, 128)][:, sl]], axis=0)
        bk = jnp.concatenate([bt_s[rows, pl.ds(po, 128)][:, sl], kt_s[rows, pl.ds(po, 128)][:, sl]], axis=0)
        return ar, bk

    def build(p, carry):
        po = pl.multiple_of(p * 128, 128)
        for c in range(RW_CHUNKS):
            for rr in range(2):
                ar, bk = operands(po, c, rr)
                gm = _dotp(ar, bk, _NT)
                m0 = pl.multiple_of((c * B_HEADS + 2 * p + rr) * CHUNK, CHUNK)
                pr0 = pl.multiple_of((c * (B_HEADS // 2) + p) * CHUNK, CHUNK)
                a_ref[pl.ds(m0, CHUNK), :CHUNK] = jnp.where(strict, -gm[:CHUNK, :CHUNK], 0.0)
                lak_ref[pl.ds(pr0, CHUNK), rr * CHUNK:(rr + 1) * CHUNK] = jnp.where(strict, gm[:CHUNK, CHUNK:], 0.0)
                arbk_ref[pl.ds(pr0, CHUNK), rr * CHUNK:(rr + 1) * CHUNK] = jnp.where(causal, gm[CHUNK:, :CHUNK], 0.0)
                arbk_ref[pl.ds(pr0, CHUNK), (2 + rr) * CHUNK:(3 + rr) * CHUNK] = jnp.where(causal, gm[CHUNK:, CHUNK:], 0.0)
        return carry

    lax.fori_loop(0, B_HEADS // 2, build, 0)
    _invert_unit_lower_batch(a_ref, att_ref, xt_ref, ti_ref, RW_NMAT, pair_blockdiag=True)

    wl = RW_HEADS * B_DH
    same_head = (lax.broadcasted_iota(jnp.int32, (wl, wl), 0) // B_DH
                 == lax.broadcasted_iota(jnp.int32, (wl, wl), 1) // B_DH)

    def bdh(x):
        return jnp.where(same_head, jnp.concatenate([x] * RW_HEADS, axis=0), 0.0)

    def pair(p, carry):
        po = pl.multiple_of(p * wl, wl)
        zero = jnp.zeros((B_DH, B_DH), F32)
        sbd = jnp.concatenate(
            [jnp.concatenate([s_ref[RW_HEADS * p + i] if i == j else zero for j in range(RW_HEADS)], axis=1)
             for i in range(RW_HEADS)], axis=0)
        ys = []
        for c in range(RW_CHUNKS):
            rows = slice(c * CHUNK, (c + 1) * CHUNK)
            win = pl.ds(po, wl)
            vp = v_s[rows, win]
            ar = jnp.concatenate([at_s[rows, win], rt_s[rows, win]], axis=0)
            bk = jnp.concatenate([bt_s[rows, win], kt_s[rows, win]], axis=0)
            pc = pc_s[c * 8:c * 8 + 1, win]
            ti_t, lak_t, arb_t, ark_t = [], [], [], []
            for x in range(RW_HEADS // 2):
                m0 = pl.multiple_of((c * B_HEADS + RW_HEADS * p + 2 * x) * CHUNK, CHUNK)
                pr0 = pl.multiple_of((c * (B_HEADS // 2) + (RW_HEADS // 2) * p + x) * CHUNK, CHUNK)
                ti_t.append(ti_ref[pl.ds(m0, CHUNK), :] + ti_ref[pl.ds(m0 + CHUNK, CHUNK), :])
                lak_t.append(lak_ref[pl.ds(pr0, CHUNK), :])
                arb_t.append(arbk_ref[pl.ds(pr0, CHUNK), :128])
                ark_t.append(arbk_ref[pl.ds(pr0, CHUNK), 128:])
            ars = _dotp(ar, sbd, _NT)
            bdv = bdh(vp)
            u = _dotp(jnp.concatenate(ti_t, axis=1),
                      bdh(ars[:CHUNK] + _dotp(jnp.concatenate(lak_t, axis=1), bdv)))
            y = ars[CHUNK:] + _dotp(jnp.concatenate(arb_t + ark_t, axis=1), jnp.concatenate([bdh(u), bdv], axis=0))
            sbd = sbd * pc + jnp.where(same_head, _dotp(jnp.concatenate([u, vp], axis=0), bk * pc, _TN), 0.0)
            gp = g_s[rows, win]
            bp = bon_s[rows, win]
            lg = lng_ref[:, win]
            lb = lnb_ref[:, win]
            outs = []
            for rr in range(RW_HEADS):
                sl = slice(rr * B_DH, (rr + 1) * B_DH)
                yh = y[:, sl]
                mean = jnp.mean(yh, axis=-1, keepdims=True)
                var = jnp.mean(jnp.square(yh - mean), axis=-1, keepdims=True)
                yn = (yh - mean) * lax.rsqrt(var + B_GN_EPS) * lg[:, sl] + lb[:, sl]
                bonus = jnp.sum(bp[:, sl], axis=-1, keepdims=True) * vp[:, sl]
                outs.append((yn + bonus) * gp[:, sl])
            ys.append(jnp.concatenate(outs, axis=1).astype(y_ref.dtype))
        for i in range(RW_HEADS):
            s_ref[RW_HEADS * p + i] = sbd[i * B_DH:(i + 1) * B_DH, i * B_DH:(i + 1) * B_DH]
        for c in range(RW_CHUNKS):
            y_ref[c * CHUNK:(c + 1) * CHUNK, pl.ds(po, wl)] = ys[c]
        return carry

    lax.fori_loop(0, B_HEADS // RW_HEADS, pair, 0)


def rwkv_prompt(zb, mu, w0, w_up, a0, a_up, g_up, k_k, k_a, r_k, ln_g, ln_b):
    m = zb.shape[0]
    nrow = RW_ROWS
    full = lambda shape: pl.BlockSpec(shape, lambda i: (0,) * len(shape))
    vec = lambda x: x.reshape(1, -1)
    wide = lambda: pltpu.VMEM((nrow, B_WIDTH), F32)
    mat = lambda rows: pltpu.VMEM((rows, 128), F32)
    return pl.pallas_call(
        _rwkv_kernel,
        grid=(m // nrow,),
        in_specs=[pl.BlockSpec((nrow, B_PROJ), lambda i: (i, 0)),
                  full((1, B_PROJ)), full((1, B_WIDTH)), full((B_W_RANK, B_WIDTH)), full((1, B_WIDTH)),
                  full((B_A_RANK, B_WIDTH)), full((B_G_RANK, B_WIDTH)), full((1, B_WIDTH)), full((1, B_WIDTH)),
                  full((1, B_WIDTH)), full((1, B_WIDTH)), full((1, B_WIDTH))],
        out_specs=[pl.BlockSpec((nrow, B_WIDTH), lambda i: (i, 0)), full((B_HEADS, B_DH, B_DH))],
        out_shape=[jax.ShapeDtypeStruct((m, B_WIDTH), BF16), jax.ShapeDtypeStruct((B_HEADS, B_DH, B_DH), F32)],
        scratch_shapes=[pltpu.VMEM((8, B_PROJ), F32),
                        wide(), wide(), wide(), wide(), wide(), wide(), wide(),
                        pltpu.VMEM((8 * RW_CHUNKS, B_WIDTH), F32),
                        mat(RW_NMAT * CHUNK), mat(CHUNK * CHUNK), mat(CHUNK * CHUNK), mat(RW_NMAT * CHUNK),
                        mat(RW_NMAT // 2 * CHUNK), pltpu.VMEM((RW_NMAT // 2 * CHUNK, 256), F32)],
        compiler_params=_cparams("arbitrary"),
        name="rwkv_prompt",
    )(zb, vec(mu), vec(w0), w_up, vec(a0), a_up, g_up, vec(k_k), vec(k_a), vec(r_k), vec(ln_g), vec(ln_b))


def lambda_init(layer):
    return 0.8 - 0.6 * math.exp(-0.3 * layer)


def _row_to_col(row):
    n = row.shape[1]
    eye = lax.broadcasted_iota(jnp.int32, (n, n), 0) == lax.broadcasted_iota(jnp.int32, (n, n), 1)
    return jnp.sum(jnp.where(eye, row, 0.0), axis=1, keepdims=True)


def _col_to_row(col):
    n = col.shape[0]
    eye = lax.broadcasted_iota(jnp.int32, (n, n), 0) == lax.broadcasted_iota(jnp.int32, (n, n), 1)
    return jnp.sum(jnp.where(eye, col, 0.0), axis=0, keepdims=True)


def _gdn_step_kernel(z_ref, ba_ref, buf_ref, s_ref, convw_ref, alog_ref, dtb_ref, gnorm_ref,
                     o_ref, so_ref, bufo_ref):
    mixed = z_ref[0][:, :C_CONV_CH]
    buf = buf_ref[0]
    w = convw_ref[...]
    conv = jnp.sum(buf * w[:C_CONV - 1], axis=0, keepdims=True) + mixed * w[C_CONV - 1:]
    act = conv * _sigmoid(conv)
    bufo_ref[0] = jnp.concatenate([buf[1:], mixed], axis=0)
    ba = ba_ref[0]
    beta = _sigmoid(ba[:, :C_V_HEADS])
    eg = jnp.exp(-jnp.exp(alog_ref[...]) * _softplus(ba[:, C_V_HEADS:] + dtb_ref[...]))
    l2n = lambda t: t * lax.rsqrt(jnp.sum(t * t, axis=-1, keepdims=True) + 1e-6)
    outs = []
    for h in range(C_V_HEADS):
        kh = h // (C_V_HEADS // C_K_HEADS)
        q = _row_to_col(l2n(act[:, kh * C_DK:(kh + 1) * C_DK]) * (C_DK ** -0.5))
        k = _row_to_col(l2n(act[:, C_KEY + kh * C_DK:C_KEY + (kh + 1) * C_DK]))
        v = act[:, 2 * C_KEY + h * C_DV:2 * C_KEY + (h + 1) * C_DV]
        s = s_ref[0, h] * eg[:, h:h + 1]
        kv = jnp.sum(s * k, axis=0, keepdims=True)
        s = s + k * ((v - kv) * beta[:, h:h + 1])
        so_ref[0, h] = s
        o = jnp.sum(s * q, axis=0, keepdims=True)
        gate = z_ref[0][:, C_CONV_CH + h * C_DV:C_CONV_CH + (h + 1) * C_DV]
        on = o * lax.rsqrt(jnp.mean(o * o, axis=-1, keepdims=True) + NORM_EPS) * gnorm_ref[...]
        outs.append(on * (gate * _sigmoid(gate)))
    o_ref[0] = jnp.concatenate(outs, axis=1).astype(o_ref.dtype)


def gdn_step(z, ba, buf, s0, conv_w, a_log, dt_bias, gnorm):
    bd = z.shape[0]
    full = lambda shape: pl.BlockSpec(shape, lambda b: (0,) * len(shape))
    per = lambda *shape: pl.BlockSpec((1,) + shape, lambda b: (b,) + (0,) * len(shape))
    return pl.pallas_call(
        _gdn_step_kernel,
        grid=(bd,),
        in_specs=[per(1, C_CONV_CH + C_VAL), per(1, 2 * C_V_HEADS), per(C_CONV - 1, C_CONV_CH),
                  per(C_V_HEADS, C_DK, C_DV), full((C_CONV, C_CONV_CH)), full((1, C_V_HEADS)),
                  full((1, C_V_HEADS)), full((1, C_DV))],
        out_specs=[per(1, C_VAL), per(C_V_HEADS, C_DK, C_DV), per(C_CONV - 1, C_CONV_CH)],
        out_shape=[jax.ShapeDtypeStruct((bd, 1, C_VAL), BF16),
                   jax.ShapeDtypeStruct((bd, C_V_HEADS, C_DK, C_DV), F32),
                   jax.ShapeDtypeStruct((bd, C_CONV - 1, C_CONV_CH), F32)],
        compiler_params=_cparams("parallel"),
        name="gdn_step",
    )(z.reshape(bd, 1, -1), ba.reshape(bd, 1, -1), buf, s0, conv_w, a_log.reshape(1, -1),
      dt_bias.reshape(1, -1), gnorm.reshape(1, -1))


def _rwkv_step_kernel(z_ref, prev_ref, s_ref, mu_ref, w0_ref, wup_ref, a0_ref, aup_ref, gup_ref, kk_ref, ka_ref,
                      rk_ref, lng_ref, lnb_ref, y_ref, so_ref, r_s, k_s, v_s, kk_s, a_s, w_s):
    z = z_ref[...]
    zs = z + mu_ref[...] * (prev_ref[...] - z)
    o1, o2, o3 = B_WIDTH, 2 * B_WIDTH, 3 * B_WIDTH
    o4 = o3 + B_W_RANK
    o5 = o4 + B_A_RANK
    k = zs[:, o1:o2]
    w = -_softplus(-(w0_ref[...] + _dotp(jnp.tanh(zs[:, o3:o4]), wup_ref[...]))) - 0.5
    a = _sigmoid(a0_ref[...] + _dotp(zs[:, o4:o5], aup_ref[...]))
    g = _dotp(_sigmoid(zs[:, o5:]), gup_ref[...])
    r_s[...] = zs[:, :o1]
    k_s[...] = k * (1.0 + (a - 1.0) * ka_ref[...])
    v_s[...] = zs[:, o2:o3]
    kk_s[...] = k * kk_ref[...]
    a_s[...] = a
    w_s[...] = jnp.exp(-jnp.exp(w))

    def seq(b, carry):
        row = lambda ref: ref[pl.ds(b, 1), :]
        r, k2, v, kkraw, ab, dec = row(r_s), row(k_s), row(v_s), row(kk_s), row(a_s), row(w_s)
        ys = []
        for h in range(B_HEADS):
            sl = slice(h * B_DH, (h + 1) * B_DH)
            kkh = kkraw[:, sl]
            kkh = kkh / jnp.maximum(jnp.sqrt(jnp.sum(kkh * kkh, axis=-1, keepdims=True)), 1e-12)
            s = s_ref[b, h]
            sa = jnp.sum(s * (-kkh), axis=1, keepdims=True)
            s = s * dec[:, sl] + sa * (kkh * ab[:, sl]) + _row_to_col(v[:, sl]) * k2[:, sl]
            so_ref[b, h] = s
            y = _col_to_row(jnp.sum(s * r[:, sl], axis=1, keepdims=True))
            mean = jnp.mean(y, axis=-1, keepdims=True)
            var = jnp.mean(jnp.square(y - mean), axis=-1, keepdims=True)
            ys.append((y - mean) * lax.rsqrt(var + B_GN_EPS) * lng_ref[:, sl] + lnb_ref[:, sl]
                      + jnp.sum(r[:, sl] * k2[:, sl] * rk_ref[:, sl], axis=-1, keepdims=True) * v[:, sl])
        y_ref[pl.ds(b, 1), :] = jnp.concatenate(ys, axis=1)
        return carry

    lax.fori_loop(0, z.shape[0], seq, 0)
    y_ref[...] = y_ref[...] * g


def rwkv_step(zb, prev, s0, mu, w0, w_up, a0, a_up, g_up, k_k, k_a, r_k, ln_g, ln_b):
    bd = zb.shape[0]
    vec = lambda x: x.reshape(1, -1)
    wide = lambda: pltpu.VMEM((bd, B_WIDTH), F32)
    return pl.pallas_call(
        _rwkv_step_kernel,
        out_shape=[jax.ShapeDtypeStruct((bd, B_WIDTH), F32),
                   jax.ShapeDtypeStruct((bd, B_HEADS, B_DH, B_DH), F32)],
        scratch_shapes=[wide() for _ in range(6)],
        compiler_params=pltpu.CompilerParams(vmem_limit_bytes=VMEM_LIMIT),
        name="rwkv_step",
    )(zb, prev, s0, vec(mu), vec(w0), w_up, vec(a0), a_up, g_up, vec(k_k), vec(k_a), vec(r_k), vec(ln_g), vec(ln_b))


def kernel(x_prompt, x_sample, cache_k, cache_v, page_table, state_wkv, state_shift, state_gdn, state_conv, meta, norm_mix, norm_ffn, norm_final, w_in_even, w_out_even, lam_q1, lam_k1, lam_q2, lam_k2, subln, rw_mu, rw_w0, rw_w_up, rw_a0, rw_a_up, rw_g_up, rw_k_k, rw_k_a, rw_r_k, rw_ln_g, rw_ln_b, w_in_odd, conv_w, a_log, dt_bias, gdn_norm, w_out_odd, peer_wq, peer_keys, peer_u, peer_v):
    x = jnp.concatenate([x_sample.reshape(DEC_BATCH, D_MODEL),
                         jnp.zeros((SEQ0 - DEC_BATCH, D_MODEL), F32),
                         meta.astype(F32), x_prompt.reshape(SEQ, D_MODEL)], axis=0)
    past_len = page_table.shape[1] * PAGE_SIZE
    pos = jnp.concatenate([jnp.full((DEC_BATCH,), past_len, jnp.int32),
                           jnp.zeros((SEQ0 - DEC_BATCH,), jnp.int32),
                           jnp.arange(T_PROMPT, dtype=jnp.int32)])
    cos, sin = _rope_tables(pos)
    nd = DEC_BATCH

    def peer(xin, layer):
        hn = rmsnorm_rows(xin, norm_ffn[layer])
        qp = matmul_cols(hn, peer_wq[layer], 0, P_HEADS * P_DQ, tn=1024)
        c1, e2, thr = peer_select(qp, peer_keys, layer)
        return peer_dense(hn, c1, e2, thr, peer_u, peer_v, layer)

    h = rmsnorm_rows(x, norm_mix[0])
    z_att = matmul_cols(h, w_in_even[0], 0, 3 * A_WIDTH, tn=1024)
    zb = matmul_cols(h, w_in_even[0], 3 * A_WIDTH, B_PROJ, tn=256)
    qb, kf, kb, vb = rope_qkv(z_att, cos, sin)
    lamvec = jnp.stack([lam_q1[0], lam_k1[0], lam_q2[0], lam_k2[0]])
    att = attn_prompt(qb, kb, vb, lamvec, subln[0], lambda_init(0))
    rw, wkv_p = rwkv_prompt(zb, rw_mu[0], rw_w0[0], rw_w_up[0], rw_a0[0], rw_a_up[0], rw_g_up[0],
                            rw_k_k[0], rw_k_a[0], rw_r_k[0], rw_ln_g[0], rw_ln_b[0])
    v_f = z_att[:, 2 * A_WIDTH:]
    per_map = lambda t: jnp.repeat(t[:nd].reshape(nd, A_HEADS, A_VD), 2, axis=1)
    own_map = (jnp.arange(A_VD, dtype=jnp.int32)[None, :] // A_DH) == (jnp.arange(2 * A_HEADS, dtype=jnp.int32)[:, None] % 2)
    att_d = attn_decode(jnp.where(own_map[None], per_map(qb), 0), per_map(kf), per_map(v_f),
                        cache_k, cache_v, page_table, lamvec, subln[0], lambda_init(0)).reshape(nd, A_WIDTH)
    rw_d, wkv_s = rwkv_step(zb[:nd], state_shift[0], state_wkv[0], rw_mu[0], rw_w0[0], rw_w_up[0], rw_a0[0],
                            rw_a_up[0], rw_g_up[0], rw_k_k[0], rw_k_a[0], rw_r_k[0], rw_ln_g[0], rw_ln_b[0])
    mix = jnp.concatenate([att.at[:nd].set(att_d.astype(BF16)), rw.at[:nd].set(rw_d.astype(BF16))], axis=1)
    x = matmul_cols(mix, w_out_even[0], 0, D_MODEL, tn=1024, residual=x)

    x, h = rmsnorm_rows(x, norm_mix[1], add=peer(x, 0), with_sum=True)
    z1 = matmul_cols(h, w_in_odd[0], 0, C_CONV_CH + C_VAL, tn=1024)
    ba = matmul_cols(h, w_in_odd[0][:, C_CONV_CH + C_VAL:], 0, 2 * C_V_HEADS, tn=2 * C_V_HEADS)
    o, gdn_p = gdn_prompt(z1, ba, conv_w[0], a_log[0], dt_bias[0], gdn_norm[0])
    o_d, gdn_s, conv_s = gdn_step(z1[:nd], ba[:nd], state_conv[0], state_gdn[0], conv_w[0], a_log[0],
                                  dt_bias[0], gdn_norm[0])
    x = matmul_cols(o.at[:nd].set(o_d.reshape(nd, C_VAL)), w_out_odd[0], 0, D_MODEL, tn=512, residual=x)
    xf = rmsnorm_rows(x, norm_final, add=peer(x, 1), out_dtype=F32)

    y_prompt = xf[SEQ0 + N_META:].reshape(1, SEQ, D_MODEL)
    y_sample = xf[:nd].reshape(nd, 1, D_MODEL)
    k_p = kf[SEQ0:].reshape(1, 1, T_PROMPT, A_HEADS, 2 * A_DH)
    v_p = v_f[SEQ0:].reshape(1, 1, T_PROMPT, A_HEADS, A_VD)
    k_s = kf[:nd].reshape(1, nd, 1, A_HEADS, 2 * A_DH)
    v_s = v_f[:nd].reshape(1, nd, 1, A_HEADS, A_VD)
    shift_p = zb[R_ROWS - 1:].reshape(1, 1, B_PROJ)
    shift_s = zb[:nd].reshape(1, nd, B_PROJ)
    conv_p = z1[R_ROWS - (C_CONV - 1):, :C_CONV_CH].reshape(1, 1, C_CONV - 1, C_CONV_CH)
    return (y_prompt, y_sample, k_p, v_p, k_s, v_s,
            wkv_p.reshape(1, 1, B_HEADS, B_DH, B_DH), wkv_s.reshape(1, nd, B_HEADS, B_DH, B_DH),
            shift_p, shift_s,
            gdn_p.reshape(1, 1, C_V_HEADS, C_DK, C_DV), gdn_s.reshape(1, nd, C_V_HEADS, C_DK, C_DV),
            conv_p, conv_s.reshape(1, nd, C_CONV - 1, C_CONV_CH))
```

```python
import functools
import math

import jax
import jax.numpy as jnp
from jax import lax
from jax.experimental import pallas as pl
from jax.experimental.pallas import tpu as pltpu

D_MODEL = 2048
SEQ = 8192
DEC_BATCH = 32
N_META = 16
PAGE_SIZE = 128
NORM_EPS = 1e-6
SUBLN_EPS = 1e-5
ROPE_THETA = 10000.0
NEG_INF = -1e30

A_HEADS = 8
A_DH = 64
A_VD = 128
A_WIDTH = 1024

B_HEADS = 16
B_DH = 64
B_WIDTH = 1024
B_W_RANK = 64
B_A_RANK = 64
B_G_RANK = 128
B_PROJ = 3328
B_GN_EPS = 64e-5

C_K_HEADS = 16
C_V_HEADS = 32
C_DK = 128
C_DV = 128
C_KEY = 2048
C_VAL = 4096
C_CONV = 4
C_CONV_CH = 8192

P_HEADS = 8
P_NKEYS = 128
P_TOPK = 16
P_DQ = 256

T_PROMPT = N_META + SEQ
R_ROWS = 8320
SEQ0 = R_ROWS - T_PROMPT
CHUNK = 64

VMEM_LIMIT = 56 * 1024 * 1024

F32 = jnp.float32
BF16 = jnp.bfloat16
HIGHEST = lax.Precision.HIGHEST


def _cparams(*sem):
    return pltpu.CompilerParams(dimension_semantics=sem, vmem_limit_bytes=VMEM_LIMIT)


def _rmsnorm_kernel(*refs, has_add, with_sum):
    x = refs[0][...]
    if has_add:
        x = x + refs[1][...]
    g_ref = refs[1 + has_add]
    outs = refs[2 + has_add:]
    if with_sum:
        outs[0][...] = x
    y = x * lax.rsqrt(jnp.mean(x * x, axis=-1, keepdims=True) + NORM_EPS)
    outs[-1][...] = (y * g_ref[...]).astype(outs[-1].dtype)


def rmsnorm_rows(x, g, add=None, with_sum=False, out_dtype=BF16, tm=640):
    m, d = x.shape
    row = pl.BlockSpec((tm, d), lambda i: (i, 0))
    ins = [x] + ([add] if add is not None else [])
    out_shape = [jax.ShapeDtypeStruct((m, d), out_dtype)]
    if with_sum:
        out_shape.insert(0, jax.ShapeDtypeStruct((m, d), F32))
    res = pl.pallas_call(
        functools.partial(_rmsnorm_kernel, has_add=add is not None, with_sum=with_sum),
        grid=(m // tm,),
        in_specs=[row] * len(ins) + [pl.BlockSpec((1, d), lambda i: (0, 0))],
        out_specs=[row] * len(out_shape),
        out_shape=out_shape,
        compiler_params=_cparams("parallel"),
        name="rmsnorm_rows",
    )(*ins, g.reshape(1, d))
    return res if with_sum else res[0]


def _matmul_kernel(a_ref, w_ref, *rest, has_res):
    if has_res:
        r_ref, o_ref, wb_ref = rest
    else:
        o_ref, wb_ref = rest

    @pl.when(pl.program_id(1) == 0)
    def _():
        wb_ref[...] = w_ref[...].astype(BF16)

    acc = jnp.dot(a_ref[...], wb_ref[...], preferred_element_type=F32)
    if has_res:
        acc = acc + r_ref[...]
    o_ref[...] = acc


def matmul_cols(a, w, col0, n, tn, tm=640, residual=None):
    m, k = a.shape
    assert w.shape[0] == k and n % tn == 0 and col0 % tn == 0 and m % tm == 0
    cb0 = col0 // tn
    in_specs = [pl.BlockSpec((tm, k), lambda j, i: (i, 0)),
                pl.BlockSpec((k, tn), lambda j, i: (0, cb0 + j))]
    args = [a, w]
    if residual is not None:
        in_specs.append(pl.BlockSpec((tm, tn), lambda j, i: (i, j)))
        args.append(residual)
    return pl.pallas_call(
        functools.partial(_matmul_kernel, has_res=residual is not None),
        grid=(n // tn, m // tm),
        in_specs=in_specs,
        out_specs=pl.BlockSpec((tm, tn), lambda j, i: (i, j)),
        out_shape=jax.ShapeDtypeStruct((m, n), F32),
        scratch_shapes=[pltpu.VMEM((k, tn), BF16)],
        compiler_params=_cparams("parallel", "arbitrary"),
        name="matmul_cols",
    )(*args)


def _rope_tables(pos):
    half = A_DH // 2
    inv_freq = ROPE_THETA ** (-jnp.arange(half, dtype=F32) / half)
    ang = pos.astype(F32)[:, None] * inv_freq[None, :]
    cos = jnp.tile(jnp.cos(ang), (1, 4))
    sin = jnp.sin(ang)
    sin = jnp.tile(jnp.concatenate([-sin, sin], axis=1), (1, 2))
    return cos, sin


def _rope_kernel(z_ref, cos_ref, sin_ref, qb_ref, kf_ref, kb_ref, vb_ref):
    cos = cos_ref[...]
    sin = sin_ref[...]
    lane = lax.broadcasted_iota(jnp.int32, cos.shape, 1)
    first_half = (lane % A_DH) < (A_DH // 2)

    def rope(x):
        partner = jnp.where(first_half, pltpu.roll(x, 128 - A_DH // 2, 1), pltpu.roll(x, A_DH // 2, 1))
        return x * cos + partner * sin

    for h in range(A_HEADS):
        sl = slice(h * 128, (h + 1) * 128)
        q = rope(z_ref[:, sl])
        qb_ref[:, sl] = (q * (A_DH ** -0.5)).astype(BF16)
        k = rope(z_ref[:, A_WIDTH + h * 128:A_WIDTH + (h + 1) * 128])
        kf_ref[:, sl] = k
        kb_ref[:, sl] = k.astype(BF16)
    vb_ref[...] = z_ref[:, 2 * A_WIDTH:3 * A_WIDTH].astype(BF16)


def rope_qkv(z_att, cos, sin, tm=640):
    m = z_att.shape[0]
    row = lambda w: pl.BlockSpec((tm, w), lambda i: (i, 0))
    return pl.pallas_call(
        _rope_kernel,
        grid=(m // tm,),
        in_specs=[row(3 * A_WIDTH), row(128), row(128)],
        out_specs=[row(A_WIDTH)] * 4,
        out_shape=[jax.ShapeDtypeStruct((m, A_WIDTH), BF16), jax.ShapeDtypeStruct((m, A_WIDTH), F32),
                   jax.ShapeDtypeStruct((m, A_WIDTH), BF16), jax.ShapeDtypeStruct((m, A_WIDTH), BF16)],
        compiler_params=_cparams("parallel"),
        name="rope_qkv",
    )(z_att, cos, sin)


def _lambda(lam_ref, lam_init):
    l1 = jnp.exp(jnp.sum(lam_ref[0:1, :] * lam_ref[1:2, :], axis=-1, keepdims=True))
    l2 = jnp.exp(jnp.sum(lam_ref[2:3, :] * lam_ref[3:4, :], axis=-1, keepdims=True))
    return l1 - l2 + lam_init


def _subln(o, subln_ref, lam_init):
    y = o * lax.rsqrt(jnp.mean(o * o, axis=-1, keepdims=True) + SUBLN_EPS)
    return y * subln_ref[...] * (1.0 - lam_init)


def _attn_prompt_kernel(q_ref, k_ref, v_ref, lam_ref, subln_ref, o_ref, *, tq, lam_init):
    qb = pl.program_id(1)
    q = q_ref[...]
    qidx = qb * tq + lax.broadcasted_iota(jnp.int32, (tq, tq), 0)
    kiota = lax.broadcasted_iota(jnp.int32, (tq, tq), 1)
    dn = (((1,), (1,)), ((), ()))

    def body(kb, carry, masked):
        off = pl.multiple_of(kb * tq, tq)
        k = k_ref[pl.ds(off, tq), :]
        v = v_ref[pl.ds(off, tq), :]
        if masked:
            kidx = off + kiota
            mask = (kidx <= qidx) & (kidx >= SEQ0)
        out = []
        for c in range(2):
            m, l, acc = carry[c]
            s = lax.dot_general(q[:, c * A_DH:(c + 1) * A_DH], k[:, c * A_DH:(c + 1) * A_DH], dn,
                                preferred_element_type=F32)
            if masked:
                s = jnp.where(mask, s, NEG_INF)
            m_new = jnp.maximum(m, jnp.max(s, axis=-1, keepdims=True))
            corr = jnp.exp(m - m_new)
            p = jnp.exp(s - m_new)
            l = l * corr + jnp.sum(p, axis=-1, keepdims=True)
            acc = acc * corr + jnp.dot(p.astype(BF16), v, preferred_element_type=F32)
            out.append((m_new, l, acc))
        return tuple(out)

    init = tuple((jnp.full((tq, 1), NEG_INF, F32), jnp.zeros((tq, 1), F32), jnp.zeros((tq, A_VD), F32))
                 for _ in range(2))
    carry = body(0, init, True)
    carry = lax.fori_loop(1, qb, functools.partial(body, masked=False), carry)
    (_, l1, a1), (_, l2, a2) = lax.cond(qb > 0, lambda c: body(qb, c, True), lambda c: c, carry)
    o = a1 / l1 - _lambda(lam_ref, lam_init) * (a2 / l2)
    o_ref[...] = _subln(o, subln_ref, lam_init).astype(o_ref.dtype)


def attn_prompt(qb, kb, vb, lamvec, subln, lam_init, tq=640):
    m = qb.shape[0]
    return pl.pallas_call(
        functools.partial(_attn_prompt_kernel, tq=tq, lam_init=lam_init),
        grid=(A_HEADS, m // tq),
        in_specs=[pl.BlockSpec((tq, 128), lambda h, i: (i, h)),
                  pl.BlockSpec((m, 128), lambda h, i: (0, h)),
                  pl.BlockSpec((m, 128), lambda h, i: (0, h)),
                  pl.BlockSpec((4, A_DH), lambda h, i: (0, 0)),
                  pl.BlockSpec((1, A_VD), lambda h, i: (0, 0))],
        out_specs=pl.BlockSpec((tq, 128), lambda h, i: (i, h)),
        out_shape=jax.ShapeDtypeStruct((m, A_WIDTH), BF16),
        compiler_params=_cparams("parallel", "arbitrary"),
        name="attn_prompt",
    )(qb, kb, vb, lamvec, subln.reshape(1, A_VD))


PAGES_PER_STEP = 8


def _attn_decode_kernel(pt_ref, q_ref, kn_ref, vn_ref, *rest, lam_init):
    pages = rest[:2 * PAGES_PER_STEP]
    lam_ref, subln_ref, o_ref, m_s, l_s, acc_s = rest[2 * PAGES_PER_STEP:]
    p = pl.program_id(1)

    @pl.when(p == 0)
    def _():
        m_s[...] = jnp.full_like(m_s, NEG_INF)
        l_s[...] = jnp.zeros_like(l_s)
        acc_s[...] = jnp.zeros_like(acc_s)

    nk = PAGE_SIZE * A_HEADS

    def page(ref):
        return ref[0].reshape(nk, A_VD).astype(BF16)

    own_head = (lax.broadcasted_iota(jnp.int32, (2 * A_HEADS, nk), 0) // 2
                == lax.broadcasted_iota(jnp.int32, (2 * A_HEADS, nk), 1) % A_HEADS)
    q = q_ref[0]
    m, l, acc = m_s[...], l_s[...], acc_s[...]
    ss = [jnp.where(own_head, lax.dot_general(q, page(pages[2 * i]), _NT_DIMS, preferred_element_type=F32), NEG_INF)
          for i in range(PAGES_PER_STEP)]
    m_new = m
    for s in ss:
        m_new = jnp.maximum(m_new, jnp.max(s, axis=-1, keepdims=True))
    corr = jnp.exp(m - m_new)
    l = l * corr
    acc = acc * corr
    for i, s in enumerate(ss):
        pr = jnp.exp(s - m_new)
        l = l + jnp.sum(pr, axis=-1, keepdims=True)
        acc = acc + jnp.dot(pr.astype(BF16), page(pages[2 * i + 1]), preferred_element_type=F32)
    m = m_new
    m_s[...], l_s[...], acc_s[...] = m, l, acc

    @pl.when(p == pl.num_programs(1) - 1)
    def _():
        kn = kn_ref[0].astype(BF16).astype(F32)
        vn = vn_ref[0].astype(BF16).astype(F32)
        s = jnp.sum(q.astype(F32) * kn, axis=-1, keepdims=True)
        m_new = jnp.maximum(m, s)
        corr = jnp.exp(m - m_new)
        pr = jnp.exp(s - m_new)
        lf = l * corr + pr
        o = (acc * corr + pr.astype(BF16).astype(F32) * vn) / lf
        lam = _lambda(lam_ref, lam_init)
        outs = [_subln(o[2 * h:2 * h + 1] - lam * o[2 * h + 1:2 * h + 2], subln_ref, lam_init)
                for h in range(A_HEADS)]
        o_ref[0] = jnp.concatenate(outs, axis=1).astype(o_ref.dtype)


def attn_decode(qm, k_new, v_new, cache_k, cache_v, page_table, lamvec, subln, lam_init):
    bd = qm.shape[0]
    n_pages = page_table.shape[1]
    assert n_pages % PAGES_PER_STEP == 0
    page_specs = []
    for i in range(PAGES_PER_STEP):
        idx = lambda b, p, pt, i=i: (0, pt[b, p * PAGES_PER_STEP + i], 0, 0, 0)
        spec = pl.BlockSpec((None, 1, PAGE_SIZE, A_HEADS, 128), idx)
        page_specs += [spec, spec]
    per_seq = lambda r, w: pl.BlockSpec((1, r, w), lambda b, p, pt: (b, 0, 0))
    grid_spec = pltpu.PrefetchScalarGridSpec(
        num_scalar_prefetch=1,
        grid=(bd, n_pages // PAGES_PER_STEP),
        in_specs=[per_seq(2 * A_HEADS, A_VD)] * 3 + page_specs
                 + [pl.BlockSpec((4, A_DH), lambda b, p, pt: (0, 0)), pl.BlockSpec((1, A_VD), lambda b, p, pt: (0, 0))],
        out_specs=per_seq(1, A_WIDTH),
        scratch_shapes=[pltpu.VMEM((2 * A_HEADS, 1), F32), pltpu.VMEM((2 * A_HEADS, 1), F32),
                        pltpu.VMEM((2 * A_HEADS, A_VD), F32)],
    )
    args = [qm, k_new, v_new] + [cache_k, cache_v] * PAGES_PER_STEP + [lamvec, subln.reshape(1, A_VD)]
    return pl.pallas_call(
        functools.partial(_attn_decode_kernel, lam_init=lam_init),
        grid_spec=grid_spec,
        out_shape=jax.ShapeDtypeStruct((bd, 1, A_WIDTH), BF16),
        compiler_params=_cparams("parallel", "arbitrary"),
        name="attn_decode",
    )(page_table, *args)


def _top_values(s, k):
    n = s.shape[0]
    row = lax.broadcasted_iota(jnp.int32, s.shape, 0).astype(F32)
    vals = []
    for _ in range(k):
        m = jnp.max(s, axis=0, keepdims=True)
        first = jnp.min(jnp.where(s == m, row, float(n)), axis=0, keepdims=True)
        s = jnp.where(row == first, -jnp.inf, s)
        vals.append(m)
    return jnp.concatenate(vals, axis=0)


def _peer_select_kernel(q_ref, keys_ref, c1_ref, e2_ref, thr_ref):
    dn = (((1,), (1,)), ((), ()))

    def head(h, carry):
        qo = pl.multiple_of(h * P_DQ, P_DQ)
        s1 = lax.dot_general(keys_ref[0, h], q_ref[:, pl.ds(qo, 128)], dn,
                             precision=HIGHEST, preferred_element_type=F32)
        s2 = lax.dot_general(keys_ref[1, h], q_ref[:, pl.ds(qo + 128, 128)], dn,
                             precision=HIGHEST, preferred_element_type=F32)
        t1 = _top_values(s1, P_TOPK)
        t2 = _top_values(s2, P_TOPK)
        e1 = jnp.exp(s1 - t1[0:1])
        e2 = jnp.exp(s2 - t2[0:1])
        et1 = jnp.exp(t1 - t1[0:1])
        et2 = jnp.exp(t2 - t2[0:1])
        nb = [P_TOPK // (a + 1) for a in range(P_TOPK)]
        cand = jnp.concatenate([et1[a:a + 1] * et2[:nb[a]] for a in range(P_TOPK)], axis=0)
        top = _top_values(cand, P_TOPK)
        inv_z = 1.0 / jnp.sum(top, axis=0, keepdims=True)
        scaled = jnp.concatenate([(et1[a:a + 1] * inv_z) * et2[:nb[a]] for a in range(P_TOPK)], axis=0)
        thr = jnp.min(jnp.where(cand >= top[P_TOPK - 1:P_TOPK], scaled, jnp.inf), axis=0, keepdims=True)
        ro = pl.multiple_of(h * 128, 128)
        c1_ref[pl.ds(ro, 128), :] = e1 * inv_z
        e2_ref[pl.ds(ro, 128), :] = e2
        thr_ref[pl.ds(h, 1), :] = thr
        return carry

    lax.fori_loop(0, P_HEADS, head, 0)


def peer_select(qp, keys, layer, tm=640):
    m = qp.shape[0]
    col = lambda r: pl.BlockSpec((r, tm), lambda i: (0, i))
    return pl.pallas_call(
        _peer_select_kernel,
        grid=(m // tm,),
        in_specs=[pl.BlockSpec((tm, P_HEADS * P_DQ), lambda i: (i, 0)),
                  pl.BlockSpec((None, 2, P_HEADS, P_NKEYS, 128), lambda i: (layer, 0, 0, 0, 0))],
        out_specs=[col(P_HEADS * 128), col(P_HEADS * 128), col(P_HEADS)],
        out_shape=[jax.ShapeDtypeStruct((P_HEADS * 128, m), F32), jax.ShapeDtypeStruct((P_HEADS * 128, m), F32),
                   jax.ShapeDtypeStruct((P_HEADS, m), F32)],
        compiler_params=_cparams("parallel"),
        name="peer_select",
    )(qp, keys)


def _peer_dense_kernel(h_ref, c1_ref, e2_ref, thr_ref, u_ref, v_ref, o_ref, *, te):
    j = pl.program_id(1)

    @pl.when(j == 0)
    def _():
        o_ref[...] = jnp.zeros_like(o_ref)

    sub = 256
    total = None
    for cb in range(te // sub):
        rows = slice(cb * sub, (cb + 1) * sub)
        g = lax.dot_general(u_ref[rows, :].astype(BF16), h_ref[...], _NT_DIMS, preferred_element_type=F32)
        act = 0.5 * g * (1.0 + lax.erf(g * (2.0 ** -0.5)))
        parts = []
        for ib in range(sub // 128):
            i1 = j * (te // 128) + cb * (sub // 128) + ib
            w = jnp.zeros((128, g.shape[1]), F32)
            for h in range(P_HEADS):
                p = e2_ref[h * 128:(h + 1) * 128, :] * c1_ref[pl.ds(h * 128 + i1, 1), :]
                w = w + jnp.where(p >= thr_ref[h:h + 1, :], p, 0.0)
            parts.append(act[ib * 128:(ib + 1) * 128, :] * w)
        a = jnp.concatenate(parts, axis=0).T.astype(BF16)
        y = jnp.dot(a, v_ref[rows, :].astype(BF16), preferred_element_type=F32)
        total = y if total is None else total + y
    o_ref[...] += total


_NT_DIMS = (((1,), (1,)), ((), ()))


def peer_dense(hn, c1, e2, thr, u_tabs, v_tabs, layer, tm=640, te=512):
    m, d = hn.shape
    ne = u_tabs.shape[1]
    col = lambda r: pl.BlockSpec((r, tm), lambda i, j: (0, i))
    tab = pl.BlockSpec((None, te, d), lambda i, j: (layer, j, 0))
    return pl.pallas_call(
        functools.partial(_peer_dense_kernel, te=te),
        grid=(m // tm, ne // te),
        in_specs=[pl.BlockSpec((tm, d), lambda i, j: (i, 0)),
                  col(P_HEADS * 128), col(P_HEADS * 128), col(P_HEADS), tab, tab],
        out_specs=pl.BlockSpec((tm, d), lambda i, j: (i, 0)),
        out_shape=jax.ShapeDtypeStruct((m, d), F32),
        compiler_params=_cparams("parallel", "arbitrary"),
        name="peer_dense",
    )(hn, c1, e2, thr, u_tabs, v_tabs)


SEQ_PASSES = 1


def _dotp(a, b, dims=(((1,), (0,)), ((), ())), passes=None):
    passes = SEQ_PASSES if passes is None else passes
    dot = lambda x, y: lax.dot_general(x, y, dims, preferred_element_type=F32)
    a_hi = a.astype(BF16)
    b_hi = b.astype(BF16)
    if passes == 1:
        return dot(a_hi, b_hi)
    a_lo = (a - a_hi.astype(F32)).astype(BF16)
    b_lo = (b - b_hi.astype(F32)).astype(BF16)
    return dot(a_hi, b_hi) + dot(a_lo, b_hi) + dot(a_hi, b_lo)


_NT = (((1,), (1,)), ((), ()))
_TN = (((0,), (0,)), ((), ()))


def _tri_masks():
    i = lax.broadcasted_iota(jnp.int32, (CHUNK, CHUNK), 0)
    j = lax.broadcasted_iota(jnp.int32, (CHUNK, CHUNK), 1)
    return i >= j, i > j


def _invert_unit_lower_batch(a_ref, at_ref, xt_ref, ti_ref, nmat, pair_blockdiag=False):
    zpad = jnp.zeros((128 - nmat, 128), F32)
    for t in range(CHUNK):
        slab = a_ref[pl.ds(t, nmat, stride=CHUNK), :]
        at_ref[t * CHUNK:(t + 1) * CHUNK, :] = jnp.concatenate([slab, zpad], axis=0).T[:CHUNK, :]
    sub = lax.broadcasted_iota(jnp.int32, (8, 128), 0)
    for tb in range(CHUNK // 8):
        def row(ti, carry, tb=tb):
            t = tb * 8 + ti
            acc = [jnp.zeros((8, 128), F32) for _ in range(tb)] + [(sub == ti).astype(F32)]
            for sb in range(tb + 1):
                for s in range(8):
                    sg = sb * 8 + s
                    a = at_ref[pl.ds(t * CHUNK + sg, 1), :]
                    for cb in range(sb + 1):
                        acc[cb] = acc[cb] - a * xt_ref[sg * CHUNK + cb * 8:sg * CHUNK + cb * 8 + 8, :]
            for cb in range(tb + 1):
                xt_ref[pl.ds(pl.multiple_of(t * CHUNK + cb * 8, 8), 8), :] = acc[cb]
            return carry
        lax.fori_loop(0, 8, row, 0)
    zrow = jnp.zeros((128 - CHUNK, 128), F32)
    odd = lax.broadcasted_iota(jnp.int32, (nmat, 128), 0) % 2 == 1
    for t in range(CHUNK):
        slab = xt_ref[t * CHUNK:(t + 1) * CHUNK, :]
        out = jnp.concatenate([slab, zrow], axis=0).T[:nmat, :]
        if pair_blockdiag:
            out = jnp.where(odd, pltpu.roll(out, CHUNK, 1), out)
        ti_ref[pl.ds(t, nmat, stride=CHUNK), :] = out


def _softplus(x):
    return jnp.maximum(x, 0.0) + jnp.log(1.0 + jnp.exp(-jnp.abs(x)))


def _sigmoid(x):
    return 1.0 / (1.0 + jnp.exp(-x))


GDN_ROWS = 2 * CHUNK
GDN_NMAT = 2 * C_V_HEADS
GDN_KH = 4


def _gdn_kernel(z_ref, ba_ref, bat_ref, convw_ref, alog_ref, dtb_ref, alogt_ref, dtbt_ref, gnorm_ref,
                o_ref, s_ref,
                halo_ref, q_s, k_s, kt_s, v_s, gcb_ref, bb_ref, gct_ref, a_ref, at_ref, xt_ref, ti_ref, att_ref):
    step = pl.program_id(0)
    nrow = GDN_ROWS

    @pl.when(step == 0)
    def _():
        s_ref[...] = jnp.zeros_like(s_ref)
        halo_ref[...] = jnp.zeros_like(halo_ref)
        xt_ref[...] = jnp.zeros_like(xt_ref)
        a_ref[...] = jnp.zeros_like(a_ref)
        att_ref[...] = jnp.zeros_like(att_ref)

    valid = (step * nrow + lax.broadcasted_iota(jnp.int32, (nrow, 1), 0)) >= SEQ0
    valid_t = (step * nrow + lax.broadcasted_iota(jnp.int32, (1, nrow), 1)) >= SEQ0
    beta = jnp.where(valid, _sigmoid(ba_ref[:, :C_V_HEADS]), 0.0)
    g = jnp.where(valid, -jnp.exp(alog_ref[...]) * _softplus(ba_ref[:, C_V_HEADS:] + dtb_ref[...]), 0.0)
    g_t = jnp.where(valid_t, -jnp.exp(alogt_ref[...]) * _softplus(bat_ref[C_V_HEADS:, :] + dtbt_ref[...]), 0.0)
    causal, strict = _tri_masks()
    tri = causal.astype(F32)
    hi = lambda a, b, dims=(((1,), (0,)), ((), ())): lax.dot_general(a, b, dims, precision=HIGHEST,
                                                                      preferred_element_type=F32)
    gc = jnp.concatenate([hi(tri, g[c * CHUNK:(c + 1) * CHUNK]) for c in range(2)], axis=0)
    gct_ref[...] = jnp.concatenate([hi(g_t[:, c * CHUNK:(c + 1) * CHUNK], tri, _NT) for c in range(2)], axis=1)
    for h in range(C_V_HEADS):
        gcb_ref[h] = jnp.broadcast_to(gc[:, h:h + 1], (nrow, 128))
        bb_ref[h] = jnp.broadcast_to(beta[:, h:h + 1], (nrow, 128))

    def conv_act(col0):
        x = z_ref[:, col0:col0 + 128]
        xe = jnp.concatenate([halo_ref[:, col0:col0 + 128], x], axis=0)
        w = convw_ref[:, col0:col0 + 128]
        y = (w[3:4] * x + w[2:3] * pltpu.roll(xe, 1, 0)[8:] + w[1:2] * pltpu.roll(xe, 2, 0)[8:]
             + w[0:1] * pltpu.roll(xe, 3, 0)[8:])
        return y * _sigmoid(y)

    def l2n(x):
        return x * lax.rsqrt(jnp.sum(x * x, axis=-1, keepdims=True) + 1e-6)

    for h in range(C_K_HEADS):
        q_s[:, h * 128:(h + 1) * 128] = l2n(conv_act(h * 128)) * (C_DK ** -0.5)
        kh_tile = l2n(conv_act(C_KEY + h * 128))
        k_s[:, h * 128:(h + 1) * 128] = kh_tile
        kt_s[h * 128:(h + 1) * 128, :] = kh_tile.T
    for h in range(C_V_HEADS):
        v_s[:, h * 128:(h + 1) * 128] = conv_act(2 * C_KEY + h * 128)
    halo_ref[...] = z_ref[nrow - 8:nrow, :C_CONV_CH]

    def build(kh, carry):
        ko = pl.multiple_of(kh * 128, 128)
        kt = kt_s[pl.ds(ko, 128), :]
        for c in range(2):
            rows = slice(c * CHUNK, (c + 1) * CHUNK)
            kq = jnp.concatenate([k_s[rows, pl.ds(ko, 128)], q_s[rows, pl.ds(ko, 128)]], axis=0)
            g2 = _dotp(kq, kt[:, c * CHUNK:(c + 1) * CHUNK])
            p0 = pl.multiple_of((c * C_K_HEADS + kh) * 128, 128)
            for r in range(2):
                vh = 2 * kh + r
                diff = gcb_ref[vh][rows, :CHUNK] - gct_ref[pl.ds(vh, 1), c * CHUNK:(c + 1) * CHUNK]
                dec = jnp.where(causal, jnp.exp(jnp.where(causal, diff, 0.0)), 0.0)
                a_ref[pl.ds(p0 + r * CHUNK, CHUNK), :CHUNK] = jnp.where(
                    strict, bb_ref[vh][rows, :CHUNK] * g2[:CHUNK] * dec, 0.0)
                att_ref[pl.ds(p0 + r * CHUNK, CHUNK), r * CHUNK:(r + 1) * CHUNK] = jnp.where(
                    causal, g2[CHUNK:] * dec, 0.0)
        return carry

    lax.fori_loop(0, C_K_HEADS, build, 0)
    _invert_unit_lower_batch(a_ref, at_ref, xt_ref, ti_ref, GDN_NMAT, pair_blockdiag=True)

    def bdiag(blocks):
        r, w = blocks[0].shape
        return jnp.concatenate(
            [jnp.concatenate([b if j == i else jnp.zeros((r, w), F32) for j in range(len(blocks))], axis=1)
             for i, b in enumerate(blocks)], axis=0)

    def pair(gi, carry):
        khs = [gi * GDN_KH + x for x in range(GDN_KH)]
        kos = [pl.multiple_of(kh * 128, 128) for kh in khs]
        kts = [kt_s[pl.ds(ko, 128), :] for ko in kos]
        s4 = jnp.concatenate([jnp.concatenate([s_ref[2 * kh], s_ref[2 * kh + 1]], axis=1) for kh in khs], axis=0)
        outs = []
        for c in range(2):
            rows = slice(c * CHUNK, (c + 1) * CHUNK)
            kq = bdiag([jnp.concatenate([k_s[rows, pl.ds(ko, 128)], q_s[rows, pl.ds(ko, 128)]], axis=0) for ko in kos])
            kqs = _dotp(kq, s4)
            rhs, eg, gl, gcc = [], [], [], []
            for x, kh in enumerate(khs):
                v2 = v_s[rows, pl.ds(pl.multiple_of(kh * 256, 256), 256)]
                for r in range(2):
                    g = gcb_ref[2 * kh + r][rows, :]
                    e = jnp.exp(g)
                    ks = kqs[x * 128:x * 128 + CHUNK, r * C_DV:(r + 1) * C_DV]
                    rhs.append(bb_ref[2 * kh + r][rows, :] * (v2[:, r * C_DV:(r + 1) * C_DV] - e * ks))
                    gcc.append(g)
                    eg.append(e)
                    gl.append(g[CHUNK - 1:CHUNK, :])
            p0s = [pl.multiple_of((c * C_K_HEADS + kh) * 128, 128) for kh in khs]
            vn = _dotp(bdiag([ti_ref[pl.ds(p0, 128), :] for p0 in p0s]), jnp.concatenate(rhs, axis=0))
            av = _dotp(bdiag([att_ref[pl.ds(p0, 128), :] for p0 in p0s]), vn)
            oc, dvn, glast = [], [], []
            for x, kh in enumerate(khs):
                o2 = []
                for r in range(2):
                    i = 2 * x + r
                    o = eg[i] * kqs[x * 128 + CHUNK:(x + 1) * 128, r * C_DV:(r + 1) * C_DV] + av[i * CHUNK:(i + 1) * CHUNK]
                    o2.append(o * lax.rsqrt(jnp.mean(o * o, axis=-1, keepdims=True) + NORM_EPS) * gnorm_ref[...])
                gate2 = z_ref[rows, pl.ds(pl.multiple_of(C_CONV_CH + kh * 256, 256), 256)]
                oc.append((jnp.concatenate(o2, axis=1) * (gate2 * _sigmoid(gate2))).astype(o_ref.dtype))
                dvn.append(jnp.concatenate([jnp.exp(gl[2 * x + r] - gcc[2 * x + r]) * vn[(2 * x + r) * CHUNK:(2 * x + r + 1) * CHUNK]
                                            for r in range(2)], axis=1))
                glast.append(jnp.broadcast_to(jnp.concatenate([jnp.exp(gl[2 * x]), jnp.exp(gl[2 * x + 1])], axis=1),
                                              (C_DK, 2 * C_DV)))
            outs.append(oc)
            s4 = s4 * jnp.concatenate(glast, axis=0) + _dotp(
                bdiag([kt[:, c * CHUNK:(c + 1) * CHUNK] for kt in kts]), jnp.concatenate(dvn, axis=0))
        for x, kh in enumerate(khs):
            s_ref[2 * kh] = s4[x * C_DK:(x + 1) * C_DK, :C_DV]
            s_ref[2 * kh + 1] = s4[x * C_DK:(x + 1) * C_DK, C_DV:]
            for c in range(2):
                o_ref[c * CHUNK:(c + 1) * CHUNK, pl.ds(pl.multiple_of(kh * 256, 256), 256)] = outs[c][x]
        return carry

    lax.fori_loop(0, C_K_HEADS // GDN_KH, pair, 0)


def gdn_prompt(z, ba, conv_w, a_log, dt_bias, gdn_norm):
    m = z.shape[0]
    nrow = GDN_ROWS
    full = lambda shape: pl.BlockSpec(shape, lambda i: (0,) * len(shape))
    mat = lambda: pltpu.VMEM((GDN_NMAT * CHUNK, 128), F32)
    return pl.pallas_call(
        _gdn_kernel,
        grid=(m // nrow,),
        in_specs=[pl.BlockSpec((nrow, C_CONV_CH + C_VAL), lambda i: (i, 0)),
                  pl.BlockSpec((nrow, 2 * C_V_HEADS), lambda i: (i, 0)),
                  pl.BlockSpec((2 * C_V_HEADS, nrow), lambda i: (0, i)),
                  full((C_CONV, C_CONV_CH)), full((1, C_V_HEADS)), full((1, C_V_HEADS)),
                  full((C_V_HEADS, 1)), full((C_V_HEADS, 1)), full((1, C_DV))],
        out_specs=[pl.BlockSpec((nrow, C_VAL), lambda i: (i, 0)),
                   full((C_V_HEADS, C_DK, C_DV))],
        out_shape=[jax.ShapeDtypeStruct((m, C_VAL), BF16),
                   jax.ShapeDtypeStruct((C_V_HEADS, C_DK, C_DV), F32)],
        scratch_shapes=[pltpu.VMEM((8, C_CONV_CH), F32),
                        pltpu.VMEM((nrow, C_KEY), F32), pltpu.VMEM((nrow, C_KEY), F32), pltpu.VMEM((C_KEY, nrow), F32),
                        pltpu.VMEM((nrow, C_VAL), F32),
                        pltpu.VMEM((C_V_HEADS, nrow, 128), F32), pltpu.VMEM((C_V_HEADS, nrow, 128), F32),
                        pltpu.VMEM((C_V_HEADS, nrow), F32),
                        mat(), mat(), mat(), mat(), mat()],
        compiler_params=_cparams("arbitrary"),
        name="gdn_prompt",
    )(z, ba, ba.T, conv_w, a_log.reshape(1, -1), dt_bias.reshape(1, -1),
      a_log.reshape(-1, 1), dt_bias.reshape(-1, 1), gdn_norm.reshape(1, -1))


RW_CHUNKS = 5
RW_ROWS = RW_CHUNKS * CHUNK
RW_NMAT = RW_CHUNKS * B_HEADS
RW_HEADS = 8


def _rwkv_kernel(z_ref, mu_ref, w0_ref, wup_ref, a0_ref, aup_ref, gup_ref, kk_ref, ka_ref, rk_ref, lng_ref, lnb_ref,
                 y_ref, s_ref,
                 halo_ref, at_s, bt_s, kt_s, rt_s, v_s, g_s, bon_s, pc_s,
                 a_ref, att_ref, xt_ref, ti_ref, lak_ref, arbk_ref):
    step = pl.program_id(0)
    nrow = RW_ROWS

    @pl.when(step == 0)
    def _():
        s_ref[...] = jnp.zeros_like(s_ref)
        halo_ref[...] = jnp.zeros_like(halo_ref)
        xt_ref[...] = jnp.zeros_like(xt_ref)
        a_ref[...] = jnp.zeros_like(a_ref)

    valid = (step * nrow + lax.broadcasted_iota(jnp.int32, (nrow, 1), 0)) >= SEQ0
    z = z_ref[...]
    zprev = pltpu.roll(jnp.concatenate([halo_ref[...], z], axis=0), 1, 0)[8:]
    halo_ref[...] = z[nrow - 8:nrow]
    zs = z + mu_ref[...] * (zprev - z)
    o1, o2, o3 = B_WIDTH, 2 * B_WIDTH, 3 * B_WIDTH
    o4 = o3 + B_W_RANK
    o5 = o4 + B_A_RANK
    r = zs[:, :o1]
    k = zs[:, o1:o2]
    v = jnp.where(valid, zs[:, o2:o3], 0.0)
    w = -_softplus(-(w0_ref[...] + _dotp(jnp.tanh(zs[:, o3:o4]), wup_ref[...]))) - 0.5
    ld = jnp.where(valid, -jnp.exp(w), 0.0)
    a = _sigmoid(a0_ref[...] + _dotp(zs[:, o4:o5], aup_ref[...]))
    g_s[...] = _dotp(_sigmoid(zs[:, o5:]), gup_ref[...])
    k2 = jnp.where(valid, k * (1.0 + (a - 1.0) * ka_ref[...]), 0.0)
    bon_s[...] = r * k2 * rk_ref[...]
    v_s[...] = v
    kkraw = k * kk_ref[...]
    causal, strict = _tri_masks()
    tri = causal.astype(F32)
    cum = jnp.concatenate(
        [lax.dot_general(tri, ld[c * CHUNK:(c + 1) * CHUNK], (((1,), (0,)), ((), ())), precision=HIGHEST,
                         preferred_element_type=F32) for c in range(RW_CHUNKS)], axis=0)
    for c in range(RW_CHUNKS):
        pc_s[c * 8:(c + 1) * 8, :] = jnp.broadcast_to(jnp.exp(cum[(c + 1) * CHUNK - 1:(c + 1) * CHUNK]), (8, B_WIDTH))
    em = jnp.exp(-cum)
    kt_s[...] = k2 * em
    rt_s[...] = r * jnp.exp(cum)
    ea = jnp.exp(cum - ld)
    for h in range(B_HEADS):
        sl = slice(h * B_DH, (h + 1) * B_DH)
        kh = kkraw[:, sl]
        nrm = jnp.maximum(jnp.sqrt(jnp.sum(kh * kh, axis=-1, keepdims=True)), 1e-12)
        kkn = jnp.where(valid, kh / nrm, 0.0)
        at_s[:, sl] = -kkn * ea[:, sl]
        bt_s[:, sl] = kkn * a[:, sl] * em[:, sl]

    def operands(po, c, rr):
        rows = slice(c * CHUNK, (c + 1) * CHUNK)
        sl = slice(rr * B_DH, (rr + 1) * B_DH)
        ar = jnp.concatenate([at_s[rows, pl.ds(po, 128)][:, sl], rt_s[rows, pl.ds(po, 128)][:, sl]], axis=0)
        bk = jnp.concatenate([bt_s[rows, pl.ds(po, 128)][:, sl], kt_s[rows, pl.ds(po, 128)][:, sl]], axis=0)
        return ar, bk

    def build(p, carry):
        po = pl.multiple_of(p * 128, 128)
        for c in range(RW_CHUNKS):
            for rr in range(2):
                ar, bk = operands(po, c, rr)
                gm = _dotp(ar, bk, _NT)
                m0 = pl.multiple_of((c * B_HEADS + 2 * p + rr) * CHUNK, CHUNK)
                pr0 = pl.multiple_of((c * (B_HEADS // 2) + p) * CHUNK, CHUNK)
                a_ref[pl.ds(m0, CHUNK), :CHUNK] = jnp.where(strict, -gm[:CHUNK, :CHUNK], 0.0)
                lak_ref[pl.ds(pr0, CHUNK), rr * CHUNK:(rr + 1) * CHUNK] = jnp.where(strict, gm[:CHUNK, CHUNK:], 0.0)
                arbk_ref[pl.ds(pr0, CHUNK), rr * CHUNK:(rr + 1) * CHUNK] = jnp.where(causal, gm[CHUNK:, :CHUNK], 0.0)
                arbk_ref[pl.ds(pr0, CHUNK), (2 + rr) * CHUNK:(3 + rr) * CHUNK] = jnp.where(causal, gm[CHUNK:, CHUNK:], 0.0)
        return carry

    lax.fori_loop(0, B_HEADS // 2, build, 0)
    _invert_unit_lower_batch(a_ref, att_ref, xt_ref, ti_ref, RW_NMAT, pair_blockdiag=True)

    wl = RW_HEADS * B_DH
    same_head = (lax.broadcasted_iota(jnp.int32, (wl, wl), 0) // B_DH
                 == lax.broadcasted_iota(jnp.int32, (wl, wl), 1) // B_DH)

    def bdh(x):
        return jnp.where(same_head, jnp.concatenate([x] * RW_HEADS, axis=0), 0.0)

    def pair(p, carry):
        po = pl.multiple_of(p * wl, wl)
        zero = jnp.zeros((B_DH, B_DH), F32)
        sbd = jnp.concatenate(
            [jnp.concatenate([s_ref[RW_HEADS * p + i] if i == j else zero for j in range(RW_HEADS)], axis=1)
             for i in range(RW_HEADS)], axis=0)
        ys = []
        for c in range(RW_CHUNKS):
            rows = slice(c * CHUNK, (c + 1) * CHUNK)
            win = pl.ds(po, wl)
            vp = v_s[rows, win]
            ar = jnp.concatenate([at_s[rows, win], rt_s[rows, win]], axis=0)
            bk = jnp.concatenate([bt_s[rows, win], kt_s[rows, win]], axis=0)
            pc = pc_s[c * 8:c * 8 + 1, win]
            ti_t, lak_t, arb_t, ark_t = [], [], [], []
            for x in range(RW_HEADS // 2):
                m0 = pl.multiple_of((c * B_HEADS + RW_HEADS * p + 2 * x) * CHUNK, CHUNK)
                pr0 = pl.multiple_of((c * (B_HEADS // 2) + (RW_HEADS // 2) * p + x) * CHUNK, CHUNK)
                ti_t.append(ti_ref[pl.ds(m0, CHUNK), :] + ti_ref[pl.ds(m0 + CHUNK, CHUNK), :])
                lak_t.append(lak_ref[pl.ds(pr0, CHUNK), :])
                arb_t.append(arbk_ref[pl.ds(pr0, CHUNK), :128])
                ark_t.append(arbk_ref[pl.ds(pr0, CHUNK), 128:])
            ars = _dotp(ar, sbd, _NT)
            bdv = bdh(vp)
            u = _dotp(jnp.concatenate(ti_t, axis=1),
                      bdh(ars[:CHUNK] + _dotp(jnp.concatenate(lak_t, axis=1), bdv)))
            y = ars[CHUNK:] + _dotp(jnp.concatenate(arb_t + ark_t, axis=1), jnp.concatenate([bdh(u), bdv], axis=0))
            sbd = sbd * pc + jnp.where(same_head, _dotp(jnp.concatenate([u, vp], axis=0), bk * pc, _TN), 0.0)
            gp = g_s[rows, win]
            bp = bon_s[rows, win]
            lg = lng_ref[:, win]
            lb = lnb_ref[:, win]
            outs = []
            for rr in range(RW_HEADS):
                sl = slice(rr * B_DH, (rr + 1) * B_DH)
                yh = y[:, sl]
                mean = jnp.mean(yh, axis=-1, keepdims=True)
                var = jnp.mean(jnp.square(yh - mean), axis=-1, keepdims=True)
                yn = (yh - mean) * lax.rsqrt(var + B_GN_EPS) * lg[:, sl] + lb[:, sl]
                bonus = jnp.sum(bp[:, sl], axis=-1, keepdims=True) * vp[:, sl]
                outs.append((yn + bonus) * gp[:, sl])
            ys.append(jnp.concatenate(outs, axis=1).astype(y_ref.dtype))
        for i in range(RW_HEADS):
            s_ref[RW_HEADS * p + i] = sbd[i * B_DH:(i + 1) * B_DH, i * B_DH:(i + 1) * B_DH]
        for c in range(RW_CHUNKS):
            y_ref[c * CHUNK:(c + 1) * CHUNK, pl.ds(po, wl)] = ys[c]
        return carry

    lax.fori_loop(0, B_HEADS // RW_HEADS, pair, 0)


def rwkv_prompt(zb, mu, w0, w_up, a0, a_up, g_up, k_k, k_a, r_k, ln_g, ln_b):
    m = zb.shape[0]
    nrow = RW_ROWS
    full = lambda shape: pl.BlockSpec(shape, lambda i: (0,) * len(shape))
    vec = lambda x: x.reshape(1, -1)
    wide = lambda: pltpu.VMEM((nrow, B_WIDTH), F32)
    mat = lambda rows: pltpu.VMEM((rows, 128), F32)
    return pl.pallas_call(
        _rwkv_kernel,
        grid=(m // nrow,),
        in_specs=[pl.BlockSpec((nrow, B_PROJ), lambda i: (i, 0)),
                  full((1, B_PROJ)), full((1, B_WIDTH)), full((B_W_RANK, B_WIDTH)), full((1, B_WIDTH)),
                  full((B_A_RANK, B_WIDTH)), full((B_G_RANK, B_WIDTH)), full((1, B_WIDTH)), full((1, B_WIDTH)),
                  full((1, B_WIDTH)), full((1, B_WIDTH)), full((1, B_WIDTH))],
        out_specs=[pl.BlockSpec((nrow, B_WIDTH), lambda i: (i, 0)), full((B_HEADS, B_DH, B_DH))],
        out_shape=[jax.ShapeDtypeStruct((m, B_WIDTH), BF16), jax.ShapeDtypeStruct((B_HEADS, B_DH, B_DH), F32)],
        scratch_shapes=[pltpu.VMEM((8, B_PROJ), F32),
                        wide(), wide(), wide(), wide(), wide(), wide(), wide(),
                        pltpu.VMEM((8 * RW_CHUNKS, B_WIDTH), F32),
                        mat(RW_NMAT * CHUNK), mat(CHUNK * CHUNK), mat(CHUNK * CHUNK), mat(RW_NMAT * CHUNK),
                        mat(RW_NMAT // 2 * CHUNK), pltpu.VMEM((RW_NMAT // 2 * CHUNK, 256), F32)],
        compiler_params=_cparams("arbitrary"),
        name="rwkv_prompt",
    )(zb, vec(mu), vec(w0), w_up, vec(a0), a_up, g_up, vec(k_k), vec(k_a), vec(r_k), vec(ln_g), vec(ln_b))


def lambda_init(layer):
    return 0.8 - 0.6 * math.exp(-0.3 * layer)


def _row_to_col(row):
    n = row.shape[1]
    eye = lax.broadcasted_iota(jnp.int32, (n, n), 0) == lax.broadcasted_iota(jnp.int32, (n, n), 1)
    return jnp.sum(jnp.where(eye, row, 0.0), axis=1, keepdims=True)


def _col_to_row(col):
    n = col.shape[0]
    eye = lax.broadcasted_iota(jnp.int32, (n, n), 0) == lax.broadcasted_iota(jnp.int32, (n, n), 1)
    return jnp.sum(jnp.where(eye, col, 0.0), axis=0, keepdims=True)


def _gdn_step_kernel(z_ref, ba_ref, buf_ref, s_ref, convw_ref, alog_ref, dtb_ref, gnorm_ref,
                     o_ref, so_ref, bufo_ref):
    mixed = z_ref[0][:, :C_CONV_CH]
    buf = buf_ref[0]
    w = convw_ref[...]
    conv = jnp.sum(buf * w[:C_CONV - 1], axis=0, keepdims=True) + mixed * w[C_CONV - 1:]
    act = conv * _sigmoid(conv)
    bufo_ref[0] = jnp.concatenate([buf[1:], mixed], axis=0)
    ba = ba_ref[0]
    beta = _sigmoid(ba[:, :C_V_HEADS])
    eg = jnp.exp(-jnp.exp(alog_ref[...]) * _softplus(ba[:, C_V_HEADS:] + dtb_ref[...]))
    l2n = lambda t: t * lax.rsqrt(jnp.sum(t * t, axis=-1, keepdims=True) + 1e-6)
    outs = []
    for h in range(C_V_HEADS):
        kh = h // (C_V_HEADS // C_K_HEADS)
        q = _row_to_col(l2n(act[:, kh * C_DK:(kh + 1) * C_DK]) * (C_DK ** -0.5))
        k = _row_to_col(l2n(act[:, C_KEY + kh * C_DK:C_KEY + (kh + 1) * C_DK]))
        v = act[:, 2 * C_KEY + h * C_DV:2 * C_KEY + (h + 1) * C_DV]
        s = s_ref[0, h] * eg[:, h:h + 1]
        kv = jnp.sum(s * k, axis=0, keepdims=True)
        s = s + k * ((v - kv) * beta[:, h:h + 1])
        so_ref[0, h] = s
        o = jnp.sum(s * q, axis=0, keepdims=True)
        gate = z_ref[0][:, C_CONV_CH + h * C_DV:C_CONV_CH + (h + 1) * C_DV]
        on = o * lax.rsqrt(jnp.mean(o * o, axis=-1, keepdims=True) + NORM_EPS) * gnorm_ref[...]
        outs.append(on * (gate * _sigmoid(gate)))
    o_ref[0] = jnp.concatenate(outs, axis=1).astype(o_ref.dtype)


def gdn_step(z, ba, buf, s0, conv_w, a_log, dt_bias, gnorm):
    bd = z.shape[0]
    full = lambda shape: pl.BlockSpec(shape, lambda b: (0,) * len(shape))
    per = lambda *shape: pl.BlockSpec((1,) + shape, lambda b: (b,) + (0,) * len(shape))
    return pl.pallas_call(
        _gdn_step_kernel,
        grid=(bd,),
        in_specs=[per(1, C_CONV_CH + C_VAL), per(1, 2 * C_V_HEADS), per(C_CONV - 1, C_CONV_CH),
                  per(C_V_HEADS, C_DK, C_DV), full((C_CONV, C_CONV_CH)), full((1, C_V_HEADS)),
                  full((1, C_V_HEADS)), full((1, C_DV))],
        out_specs=[per(1, C_VAL), per(C_V_HEADS, C_DK, C_DV), per(C_CONV - 1, C_CONV_CH)],
        out_shape=[jax.ShapeDtypeStruct((bd, 1, C_VAL), BF16),
                   jax.ShapeDtypeStruct((bd, C_V_HEADS, C_DK, C_DV), F32),
                   jax.ShapeDtypeStruct((bd, C_CONV - 1, C_CONV_CH), F32)],
        compiler_params=_cparams("parallel"),
        name="gdn_step",
    )(z.reshape(bd, 1, -1), ba.reshape(bd, 1, -1), buf, s0, conv_w, a_log.reshape(1, -1),
      dt_bias.reshape(1, -1), gnorm.reshape(1, -1))


def _rwkv_step_kernel(z_ref, prev_ref, s_ref, mu_ref, w0_ref, wup_ref, a0_ref, aup_ref, gup_ref, kk_ref, ka_ref,
                      rk_ref, lng_ref, lnb_ref, y_ref, so_ref, r_s, k_s, v_s, kk_s, a_s, w_s):
    z = z_ref[...]
    zs = z + mu_ref[...] * (prev_ref[...] - z)
    o1, o2, o3 = B_WIDTH, 2 * B_WIDTH, 3 * B_WIDTH
    o4 = o3 + B_W_RANK
    o5 = o4 + B_A_RANK
    k = zs[:, o1:o2]
    w = -_softplus(-(w0_ref[...] + _dotp(jnp.tanh(zs[:, o3:o4]), wup_ref[...]))) - 0.5
    a = _sigmoid(a0_ref[...] + _dotp(zs[:, o4:o5], aup_ref[...]))
    g = _dotp(_sigmoid(zs[:, o5:]), gup_ref[...])
    r_s[...] = zs[:, :o1]
    k_s[...] = k * (1.0 + (a - 1.0) * ka_ref[...])
    v_s[...] = zs[:, o2:o3]
    kk_s[...] = k * kk_ref[...]
    a_s[...] = a
    w_s[...] = jnp.exp(-jnp.exp(w))

    def seq(b, carry):
        row = lambda ref: ref[pl.ds(b, 1), :]
        r, k2, v, kkraw, ab, dec = row(r_s), row(k_s), row(v_s), row(kk_s), row(a_s), row(w_s)
        ys = []
        for h in range(B_HEADS):
            sl = slice(h * B_DH, (h + 1) * B_DH)
            kkh = kkraw[:, sl]
            kkh = kkh / jnp.maximum(jnp.sqrt(jnp.sum(kkh * kkh, axis=-1, keepdims=True)), 1e-12)
            s = s_ref[b, h]
            sa = jnp.sum(s * (-kkh), axis=1, keepdims=True)
            s = s * dec[:, sl] + sa * (kkh * ab[:, sl]) + _row_to_col(v[:, sl]) * k2[:, sl]
            so_ref[b, h] = s
            y = _col_to_row(jnp.sum(s * r[:, sl], axis=1, keepdims=True))
            mean = jnp.mean(y, axis=-1, keepdims=True)
            var = jnp.mean(jnp.square(y - mean), axis=-1, keepdims=True)
            ys.append((y - mean) * lax.rsqrt(var + B_GN_EPS) * lng_ref[:, sl] + lnb_ref[:, sl]
                      + jnp.sum(r[:, sl] * k2[:, sl] * rk_ref[:, sl], axis=-1, keepdims=True) * v[:, sl])
        y_ref[pl.ds(b, 1), :] = jnp.concatenate(ys, axis=1)
        return carry

    lax.fori_loop(0, z.shape[0], seq, 0)
    y_ref[...] = y_ref[...] * g


def rwkv_step(zb, prev, s0, mu, w0, w_up, a0, a_up, g_up, k_k, k_a, r_k, ln_g, ln_b):
    bd = zb.shape[0]
    vec = lambda x: x.reshape(1, -1)
    wide = lambda: pltpu.VMEM((bd, B_WIDTH), F32)
    return pl.pallas_call(
        _rwkv_step_kernel,
        out_shape=[jax.ShapeDtypeStruct((bd, B_WIDTH), F32),
                   jax.ShapeDtypeStruct((bd, B_HEADS, B_DH, B_DH), F32)],
        scratch_shapes=[wide() for _ in range(6)],
        compiler_params=pltpu.CompilerParams(vmem_limit_bytes=VMEM_LIMIT),
        name="rwkv_step",
    )(zb, prev, s0, vec(mu), vec(w0), w_up, vec(a0), a_up, g_up, vec(k_k), vec(k_a), vec(r_k), vec(ln_g), vec(ln_b))


def kernel(x_prompt, x_sample, cache_k, cache_v, page_table, state_wkv, state_shift, state_gdn, state_conv, meta, norm_mix, norm_ffn, norm_final, w_in_even, w_out_even, lam_q1, lam_k1, lam_q2, lam_k2, subln, rw_mu, rw_w0, rw_w_up, rw_a0, rw_a_up, rw_g_up, rw_k_k, rw_k_a, rw_r_k, rw_ln_g, rw_ln_b, w_in_odd, conv_w, a_log, dt_bias, gdn_norm, w_out_odd, peer_wq, peer_keys, peer_u, peer_v):
    x = jnp.concatenate([x_sample.reshape(DEC_BATCH, D_MODEL),
                         jnp.zeros((SEQ0 - DEC_BATCH, D_MODEL), F32),
                         meta.astype(F32), x_prompt.reshape(SEQ, D_MODEL)], axis=0)
    past_len = page_table.shape[1] * PAGE_SIZE
    pos = jnp.concatenate([jnp.full((DEC_BATCH,), past_len, jnp.int32),
                           jnp.zeros((SEQ0 - DEC_BATCH,), jnp.int32),
                           jnp.arange(T_PROMPT, dtype=jnp.int32)])
    cos, sin = _rope_tables(pos)
    nd = DEC_BATCH

    def peer(xin, layer):
        hn = rmsnorm_rows(xin, norm_ffn[layer])
        qp = matmul_cols(hn, peer_wq[layer], 0, P_HEADS * P_DQ, tn=1024)
        c1, e2, thr = peer_select(qp, peer_keys, layer)
        return peer_dense(hn, c1, e2, thr, peer_u, peer_v, layer)

    h = rmsnorm_rows(x, norm_mix[0])
    z_att = matmul_cols(h, w_in_even[0], 0, 3 * A_WIDTH, tn=1024)
    zb = matmul_cols(h, w_in_even[0], 3 * A_WIDTH, B_PROJ, tn=256)
    qb, kf, kb, vb = rope_qkv(z_att, cos, sin)
    lamvec = jnp.stack([lam_q1[0], lam_k1[0], lam_q2[0], lam_k2[0]])
    att = attn_prompt(qb, kb, vb, lamvec, subln[0], lambda_init(0))
    rw, wkv_p = rwkv_prompt(zb, rw_mu[0], rw_w0[0], rw_w_up[0], rw_a0[0], rw_a_up[0], rw_g_up[0],
                            rw_k_k[0], rw_k_a[0], rw_r_k[0], rw_ln_g[0], rw_ln_b[0])
    v_f = z_att[:, 2 * A_WIDTH:]
    per_map = lambda t: jnp.repeat(t[:nd].reshape(nd, A_HEADS, A_VD), 2, axis=1)
    own_map = (jnp.arange(A_VD, dtype=jnp.int32)[None, :] // A_DH) == (jnp.arange(2 * A_HEADS, dtype=jnp.int32)[:, None] % 2)
    att_d = attn_decode(jnp.where(own_map[None], per_map(qb), 0), per_map(kf), per_map(v_f),
                        cache_k, cache_v, page_table, lamvec, subln[0], lambda_init(0)).reshape(nd, A_WIDTH)
    rw_d, wkv_s = rwkv_step(zb[:nd], state_shift[0], state_wkv[0], rw_mu[0], rw_w0[0], rw_w_up[0], rw_a0[0],
                            rw_a_up[0], rw_g_up[0], rw_k_k[0], rw_k_a[0], rw_r_k[0], rw_ln_g[0], rw_ln_b[0])
    mix = jnp.concatenate([att.at[:nd].set(att_d.astype(BF16)), rw.at[:nd].set(rw_d.astype(BF16))], axis=1)
    x = matmul_cols(mix, w_out_even[0], 0, D_MODEL, tn=1024, residual=x)

    x, h = rmsnorm_rows(x, norm_mix[1], add=peer(x, 0), with_sum=True)
    z1 = matmul_cols(h, w_in_odd[0], 0, C_CONV_CH + C_VAL, tn=1024)
    ba = matmul_cols(h, w_in_odd[0][:, C_CONV_CH + C_VAL:], 0, 2 * C_V_HEADS, tn=2 * C_V_HEADS)
    o, gdn_p = gdn_prompt(z1, ba, conv_w[0], a_log[0], dt_bias[0], gdn_norm[0])
    o_d, gdn_s, conv_s = gdn_step(z1[:nd], ba[:nd], state_conv[0], state_gdn[0], conv_w[0], a_log[0],
                                  dt_bias[0], gdn_norm[0])
    x = matmul_cols(o.at[:nd].set(o_d.reshape(nd, C_VAL)), w_out_odd[0], 0, D_MODEL, tn=512, residual=x)
    xf = rmsnorm_rows(x, norm_final, add=peer(x, 1), out_dtype=F32)

    y_prompt = xf[SEQ0 + N_META:].reshape(1, SEQ, D_MODEL)
    y_sample = xf[:nd].reshape(nd, 1, D_MODEL)
    k_p = kf[SEQ0:].reshape(1, 1, T_PROMPT, A_HEADS, 2 * A_DH)
    v_p = v_f[SEQ0:].reshape(1, 1, T_PROMPT, A_HEADS, A_VD)
    k_s = kf[:nd].reshape(1, nd, 1, A_HEADS, 2 * A_DH)
    v_s = v_f[:nd].reshape(1, nd, 1, A_HEADS, A_VD)
    shift_p = zb[R_ROWS - 1:].reshape(1, 1, B_PROJ)
    shift_s = zb[:nd].reshape(1, nd, B_PROJ)
    conv_p = z1[R_ROWS - (C_CONV - 1):, :C_CONV_CH].reshape(1, 1, C_CONV - 1, C_CONV_CH)
    return (y_prompt, y_sample, k_p, v_p, k_s, v_s,
            wkv_p.reshape(1, 1, B_HEADS, B_DH, B_DH), wkv_s.reshape(1, nd, B_HEADS, B_DH, B_DH),
            shift_p, shift_s,
            gdn_p.reshape(1, 1, C_V_HEADS, C_DK, C_DV), gdn_s.reshape(1, nd, C_V_HEADS, C_DK, C_DV),
            conv_p, conv_s.reshape(1, nd, C_CONV - 1, C_CONV_CH))
```
